```python
import math
import jax, jax.numpy as jnp
from jax import lax
import numpy as np

D_MODEL = 2048
BATCH = 4
SEQ = 2048
DEPTH = 4

MEM_LEN = 256
GRID_W = 64
N_MIXERS = 2
EPS = 1e-6
D_MIX = D_MODEL
D_XA = D_MIX // 4
XA_HEADS = 4
XA_DH = D_XA // XA_HEADS
D_SSD = D_MIX - D_XA
SSD_HEADDIM = 64
SSD_HEADS = D_SSD // SSD_HEADDIM
SSD_GROUPS = 4
SSD_STATE = 128
SSD_CONV = 5
SSD_CHUNK = 128
SSD_GN = SSD_GROUPS * SSD_STATE
SSD_CONV_DIM = D_SSD + 2 * SSD_GN
SSD_IN = D_SSD + SSD_CONV_DIM + 2 * SSD_HEADS + D_XA
D_NA = D_MIX - D_XA
NA_DH = 128
NA_HEADS = D_NA // NA_DH
NA_KR = 8
NA_KC = 16
NA_QCB = 16
NA_KBAND = 2 * NA_KC
NA_IN = 3 * D_NA + D_XA
N_EXPERTS = 16
EC_FACTOR = 2
D_EXPERT = D_MODEL // 2

kernel_name = "hybrid_ssd_natten_ecmoe_encoder"


def rmsnorm(x, g):
    xf = x.astype(jnp.float32)
    y = xf * lax.rsqrt(jnp.mean(xf * xf, axis=-1, keepdims=True) + EPS)
    return (y * g.astype(jnp.float32)).astype(x.dtype)


def depthwise_conv_centred(u, w, b):
    K, C = w.shape
    out = lax.conv_general_dilated(u, w[:, None, :].astype(u.dtype), window_strides=(1,),
                                   padding=[(K // 2, K // 2)],
                                   dimension_numbers=("NWC", "WIO", "NWC"),
                                   feature_group_count=C)
    return out + b.astype(u.dtype)


def memory_attention(q, mem_k, mem_v):
    s = jnp.einsum("bthd,bmhd->bhtm", q, mem_k).astype(jnp.float32) * (XA_DH ** -0.5)
    p = jax.nn.softmax(s, axis=-1).astype(mem_v.dtype)
    o = jnp.einsum("bhtm,bmhd->bthd", p, mem_v)
    return o.reshape(q.shape[0], q.shape[1], D_XA)


def ssd_chunked(x, dt, A, Bm, Cm):
    B_, T, H, P = x.shape
    G, N = Bm.shape[2], Bm.shape[3]
    Hg = H // G
    L = SSD_CHUNK
    nc = T // L
    xdt = (x * dt[..., None].astype(x.dtype)).reshape(B_, nc, L, G, Hg, P)
    dA = (dt * A).reshape(B_, nc, L, G, Hg)
    Bc = Bm.reshape(B_, nc, L, G, N)
    Cc = Cm.reshape(B_, nc, L, G, N)
    A_cum = jnp.cumsum(dA, axis=2)
    mask = np.tril(np.ones((L, L), dtype=bool))[None, None, :, :, None, None]
    seg = A_cum[:, :, :, None] - A_cum[:, :, None, :]
    Lmat = jnp.exp(jnp.where(mask, seg, -jnp.inf))
    CB = jnp.einsum("bclgn,bcsgn->bclsg", Cc, Bc)
    y_diag = jnp.einsum("bclsg,bclsgh,bcsghp->bclghp", CB, Lmat, xdt)
    decay_states = jnp.exp(A_cum[:, :, -1:] - A_cum)
    states = jnp.einsum("bclgn,bclgh,bclghp->bcghpn", Bc, decay_states, xdt)
    chunk_decay = jnp.exp(A_cum[:, :, -1])

    def step(h, inp):
        st, dec = inp
        return h * dec[..., None, None].astype(h.dtype) + st, h

    h0 = jnp.zeros((B_, G, Hg, P, N), dtype=states.dtype)
    _, prev = lax.scan(step, h0, (jnp.moveaxis(states, 1, 0), jnp.moveaxis(chunk_decay, 1, 0)))
    prev = jnp.moveaxis(prev, 0, 1)
    y_off = jnp.einsum("bclgn,bcghpn,bclgh->bclghp", Cc, prev, jnp.exp(A_cum))
    return (y_diag + y_off).reshape(B_, T, H, P)


def ssd_mixer(h, mem_k, mem_v, w_in, conv_w, conv_b, dt_bias, a_log, d_skip, gate_norm_g, w_out):
    B_, T, _ = h.shape
    proj = jnp.einsum("btd,de->bte", h, w_in)
    z, xbc, dt_raw, xq = jnp.split(
        proj, [D_SSD, D_SSD + SSD_CONV_DIM, D_SSD + SSD_CONV_DIM + 2 * SSD_HEADS], axis=-1)
    xbc = jax.nn.silu(depthwise_conv_centred(xbc, conv_w, conv_b))
    xs, Bm, Cm = jnp.split(xbc, [D_SSD, D_SSD + SSD_GN], axis=-1)
    xs = xs.reshape(B_, T, SSD_HEADS, SSD_HEADDIM)
    Bm = Bm.reshape(B_, T, SSD_GROUPS, SSD_STATE)
    Cm = Cm.reshape(B_, T, SSD_GROUPS, SSD_STATE)
    dt = jax.nn.softplus(dt_raw.astype(jnp.float32).reshape(B_, T, 2, SSD_HEADS)
                         + dt_bias.astype(jnp.float32))
    A = -jnp.exp(a_log.astype(jnp.float32))
    y_f = ssd_chunked(xs, dt[:, :, 0], A[0], Bm, Cm)
    y_b = ssd_chunked(xs[:, ::-1], dt[:, ::-1, 1], A[1], Bm[:, ::-1], Cm[:, ::-1])[:, ::-1]
    y = y_f + y_b + d_skip[:, None].astype(xs.dtype) * xs
    y = (y.reshape(B_, T, D_SSD) * jax.nn.silu(z)).astype(h.dtype)
    y = rmsnorm(y.reshape(B_, T, SSD_GROUPS, D_SSD // SSD_GROUPS),
                gate_norm_g.reshape(SSD_GROUPS, D_SSD // SSD_GROUPS)).reshape(B_, T, D_SSD)
    o_x = memory_attention(xq.reshape(B_, T, XA_HEADS, XA_DH), mem_k, mem_v).astype(h.dtype)
    return jnp.einsum("bte,ed->btd", jnp.concatenate([y, o_x], axis=-1), w_out)


def neighbourhood_attention(q, k, v, rpb):
    B_, T, H, Dh = q.shape
    rows = T // GRID_W
    kr = min(NA_KR, rows)
    q = q.reshape(B_, rows, GRID_W, H, Dh)
    k = k.reshape(B_, rows, GRID_W, H, Dh)
    v = v.reshape(B_, rows, GRID_W, H, Dh)
    n_cb = GRID_W // NA_QCB
    band_start = np.clip(np.arange(n_cb) * NA_QCB - NA_KC // 2, 0, GRID_W - NA_KBAND)
    key_cols = band_start[:, None] + np.arange(NA_KBAND)
    q_cols = np.arange(GRID_W).reshape(n_cb, NA_QCB)
    win_start = np.clip(q_cols - NA_KC // 2, 0, GRID_W - NA_KC)
    kc = key_cols[:, None, :]
    col_mask = (kc >= win_start[..., None]) & (kc < win_start[..., None] + NA_KC)
    dc_idx = np.clip(kc - q_cols[..., None] + NA_KC - 1, 0, 2 * NA_KC - 2)
    scale = Dh ** -0.5

    def row_block(args):
        r, q_r = args
        rs = jnp.clip(r - kr // 2, 0, rows - kr)
        k_rows = lax.dynamic_slice_in_dim(k, rs, kr, axis=1)
        v_rows = lax.dynamic_slice_in_dim(v, rs, kr, axis=1)
        k_band = k_rows[:, :, key_cols]
        v_band = v_rows[:, :, key_cols]
        qb = q_r.reshape(B_, n_cb, NA_QCB, H, Dh)
        s = jnp.einsum("bjqhd,brjkhd->bhjqrk", qb, k_band).astype(jnp.float32) * scale
        dr_idx = rs + jnp.arange(kr) - r + NA_KR - 1
        bias = rpb[:, dr_idx[None, None, :, None], dc_idx[:, :, None, :]]
        s = s + bias[None].astype(jnp.float32)
        s = jnp.where(col_mask[None, None, :, :, None, :], s, -1e30)
        p = jax.nn.softmax(s.reshape(B_, H, n_cb, NA_QCB, kr * NA_KBAND), axis=-1)
        p = p.reshape(B_, H, n_cb, NA_QCB, kr, NA_KBAND).astype(v.dtype)
        o = jnp.einsum("bhjqrk,brjkhd->bjqhd", p, v_band)
        return o.reshape(B_, GRID_W, H, Dh)

    outs = lax.map(row_block, (jnp.arange(rows), jnp.moveaxis(q, 1, 0)))
    return jnp.moveaxis(outs, 0, 1).reshape(B_, T, H * Dh)


def na_mixer(h, mem_k, mem_v, w_in, rpb, w_out):
    B_, T, _ = h.shape
    proj = jnp.einsum("btd,de->bte", h, w_in)
    q, k, v, xq = jnp.split(proj, [D_NA, 2 * D_NA, 3 * D_NA], axis=-1)
    shp = (B_, T, NA_HEADS, NA_DH)
    o_na = neighbourhood_attention(q.reshape(shp), k.reshape(shp), v.reshape(shp), rpb).astype(h.dtype)
    o_x = memory_attention(xq.reshape(B_, T, XA_HEADS, XA_DH), mem_k, mem_v).astype(h.dtype)
    return jnp.einsum("bte,ed->btd", jnp.concatenate([o_na, o_x], axis=-1), w_out)


def ec_moe(h, w_router, w1, w3, w2):
    B_, T, _ = h.shape
    cap = EC_FACTOR * T // N_EXPERTS
    logits = jnp.einsum("btd,de->bte", h, w_router).astype(jnp.float32)
    aff = jax.nn.softmax(logits, axis=-1)
    gate, idx = lax.top_k(jnp.swapaxes(aff, 1, 2), cap)
    bidx = jnp.arange(B_)[:, None, None]
    xs = h[bidx, idx]
    a = jnp.einsum("becd,edf->becf", xs, w1)
    b = jnp.einsum("becd,edf->becf", xs, w3)
    y = jnp.einsum("becf,efd->becd", jax.nn.silu(a) * b, w2)
    y = y * gate[..., None].astype(y.dtype)
    return jnp.zeros_like(h).at[bidx, idx].add(y.astype(h.dtype))


def setup_inputs(seed: int = 0) -> dict:
    key = jax.random.key(seed)
    ks = jax.random.split(key, 24)
    n_ssd = (DEPTH + N_MIXERS - 1) // N_MIXERS
    n_na = DEPTH // N_MIXERS
    f32 = jnp.float32

    def nrm(k, shape, scale):
        return jax.random.normal(k, shape, dtype=f32) * scale

    dt0 = jnp.exp(jax.random.uniform(ks[9], (n_ssd, 2, SSD_HEADS), dtype=f32,
                                     minval=math.log(1e-3), maxval=math.log(1e-1)))
    return {
        "x": nrm(ks[0], (BATCH, SEQ, D_MODEL), 1.0),
        "mem": nrm(ks[1], (BATCH, MEM_LEN, D_MODEL), 1.0),
        "norm_mix_g": 1.0 + nrm(ks[2], (DEPTH, D_MODEL), 0.02),
        "norm_ffn_g": 1.0 + nrm(ks[3], (DEPTH, D_MODEL), 0.02),
        "norm_final_g": 1.0 + nrm(ks[4], (D_MODEL,), 0.02),
        "mem_norm_g": 1.0 + nrm(ks[5], (D_MODEL,), 0.02),
        "ssd_w_in": nrm(ks[6], (n_ssd, D_MODEL, SSD_IN), D_MODEL ** -0.5),
        "ssd_conv_w": nrm(ks[7], (n_ssd, SSD_CONV, SSD_CONV_DIM), SSD_CONV ** -0.5),
        "ssd_conv_b": nrm(ks[8], (n_ssd, SSD_CONV_DIM), 0.02),
        "ssd_dt_bias": dt0 + jnp.log(-jnp.expm1(-dt0)),
        "ssd_a_log": jnp.log(jax.random.uniform(ks[10], (n_ssd, 2, SSD_HEADS), dtype=f32,
                                                minval=1.0, maxval=16.0)),
        "ssd_d": 1.0 + nrm(ks[11], (n_ssd, SSD_HEADS), 0.02),
        "ssd_gate_norm_g": 1.0 + nrm(ks[12], (n_ssd, D_SSD), 0.02),
        "ssd_w_out": nrm(ks[13], (n_ssd, D_MIX, D_MODEL), D_MIX ** -0.5),
        "na_w_in": nrm(ks[14], (n_na, D_MODEL, NA_IN), D_MODEL ** -0.5),
        "na_rpb": nrm(ks[15], (n_na, NA_HEADS, 2 * NA_KR - 1, 2 * NA_KC - 1), 0.1),
        "na_w_out": nrm(ks[16], (n_na, D_MIX, D_MODEL), D_MIX ** -0.5),
        "xa_w_kv": nrm(ks[17], (DEPTH, D_MODEL, 2 * D_XA), D_MODEL ** -0.5),
        "moe_w_router": nrm(ks[18], (DEPTH, D_MODEL, N_EXPERTS), D_MODEL ** -0.5),
        "moe_w1": nrm(ks[19], (DEPTH, N_EXPERTS, D_MODEL, D_EXPERT), D_MODEL ** -0.5),
        "moe_w3": nrm(ks[20], (DEPTH, N_EXPERTS, D_MODEL, D_EXPERT), D_MODEL ** -0.5),
        "moe_w2": nrm(ks[21], (DEPTH, N_EXPERTS, D_EXPERT, D_MODEL), D_EXPERT ** -0.5),
    }


def reference(x, mem, norm_mix_g, norm_ffn_g, norm_final_g, mem_norm_g,
              ssd_w_in, ssd_conv_w, ssd_conv_b, ssd_dt_bias, ssd_a_log, ssd_d,
              ssd_gate_norm_g, ssd_w_out, na_w_in, na_rpb, na_w_out, xa_w_kv,
              moe_w_router, moe_w1, moe_w3, moe_w2):
    B_, M = mem.shape[0], mem.shape[1]
    mem_n = rmsnorm(mem, mem_norm_g)
    for i in range(DEPTH):
        j = i // N_MIXERS
        kv = jnp.einsum("bmd,de->bme", mem_n, xa_w_kv[i]).reshape(B_, M, 2, XA_HEADS, XA_DH)
        mem_k, mem_v = kv[:, :, 0], kv[:, :, 1]
        h = rmsnorm(x, norm_mix_g[i])
        if i % N_MIXERS == 0:
            mix = ssd_mixer(h, mem_k, mem_v, ssd_w_in[j], ssd_conv_w[j], ssd_conv_b[j],
                            ssd_dt_bias[j], ssd_a_log[j], ssd_d[j], ssd_gate_norm_g[j], ssd_w_out[j])
        else:
            mix = na_mixer(h, mem_k, mem_v, na_w_in[j], na_rpb[j], na_w_out[j])
        x = x + mix.astype(x.dtype)
        h = rmsnorm(x, norm_ffn_g[i])
        x = x + ec_moe(h, moe_w_router[i], moe_w1[i], moe_w3[i], moe_w2[i])
    return rmsnorm(x, norm_final_g)
```

```python
import functools

import jax
import jax.numpy as jnp
from jax import lax
from jax.experimental import pallas as pl
from jax.experimental.pallas import tpu as pltpu

F32 = jnp.float32
BF16 = jnp.bfloat16
I32 = jnp.int32
HIGHEST = lax.Precision.HIGHEST

EPS = 1e-6
LANES = 128
VMEM_LIMIT_BYTES = 56 * 1024 * 1024

XA_HEADS = 4
XA_DH = 128
SSD_HEADDIM = 64
SSD_GROUPS = 4
SSD_STATE = 128
SSD_CONV = 5
SSD_CHUNK = 128
NA_DH = 128
NA_KR = 8
NA_KC = 16
GRID_W = 64
EC_FACTOR = 2
N_MIXERS = 2


def _params(n_grid_dims):
    return pltpu.CompilerParams(
        dimension_semantics=("arbitrary",) * n_grid_dims,
        vmem_limit_bytes=VMEM_LIMIT_BYTES)


def _nt_dot(a, b):
    return lax.dot_general(a, b, (((1,), (1,)), ((), ())), preferred_element_type=F32)


def _rmsnorm_body(x_ref, g_ref, o_ref):
    x = x_ref[...]
    ms = jnp.mean(x * x, axis=-1, keepdims=True)
    o_ref[...] = (x * lax.rsqrt(ms + EPS) * g_ref[...]).astype(o_ref.dtype)


def rmsnorm_rows(x2d, g, out_dtype, tm=256):
    m, d = x2d.shape
    return pl.pallas_call(
        _rmsnorm_body,
        grid=(m // tm,),
        in_specs=[pl.BlockSpec((tm, d), lambda i: (i, 0)),
                  pl.BlockSpec((1, d), lambda i: (0, 0))],
        out_specs=pl.BlockSpec((tm, d), lambda i: (i, 0)),
        out_shape=jax.ShapeDtypeStruct((m, d), out_dtype),
        compiler_params=_params(1),
        name="rmsnorm",
    )(x2d, g.reshape(1, d))


def _matmul_body(*refs, n_in, has_res):
    a_refs = refs[:n_in]
    w_refs = refs[n_in:2 * n_in]
    res_ref = refs[2 * n_in] if has_res else None
    o_ref = refs[2 * n_in + has_res]
    wbf_refs = refs[2 * n_in + has_res + 1:]

    @pl.when(pl.program_id(1) == 0)
    def _():
        for w_ref, wbf_ref in zip(w_refs, wbf_refs):
            lead = (0,) * (len(w_ref.shape) - 2)
            wbf_ref[...] = w_ref[lead + (slice(None), slice(None))].astype(BF16)

    acc = None
    for a_ref, wbf_ref in zip(a_refs, wbf_refs):
        t = jnp.dot(a_ref[...], wbf_ref[...], preferred_element_type=F32)
        acc = t if acc is None else acc + t
    if has_res:
        acc = acc + res_ref[...]
    o_ref[...] = acc.astype(o_ref.dtype)


def matmul_ws(a_list, w_list, n_cols, *, tn, tm, out_dtype, res=None, name="matmul"):
    m = a_list[0].shape[0]
    n_in = len(a_list)
    in_specs = [pl.BlockSpec((tm, a.shape[1]), lambda n, i: (i, 0)) for a in a_list]
    scratch = []
    for _, blk, imap in w_list:
        in_specs.append(pl.BlockSpec(blk, lambda n, i, imap=imap: imap(n)))
        scratch.append(pltpu.VMEM(blk[-2:], BF16))
    args = list(a_list) + [w for w, _, _ in w_list]
    if res is not None:
        in_specs.append(pl.BlockSpec((tm, tn), lambda n, i: (i, n)))
        args.append(res)
    return pl.pallas_call(
        functools.partial(_matmul_body, n_in=n_in, has_res=res is not None),
        grid=(n_cols // tn, m // tm),
        in_specs=in_specs,
        out_specs=pl.BlockSpec((tm, tn), lambda n, i: (i, n)),
        out_shape=jax.ShapeDtypeStruct((m, n_cols), out_dtype),
        scratch_shapes=scratch,
        compiler_params=_params(2),
        name=name,
    )(*args)


def _conv_silu_body(u_ref, w_ref, b_ref, o_ref, pad_ref, *, t_len, k_conv, rows):
    halo = 8
    ch = u_ref.shape[-1]
    pad_ref[0:halo, :] = jnp.zeros((halo, ch), F32)
    pad_ref[t_len + halo:t_len + 2 * halo, :] = jnp.zeros((halo, ch), F32)
    pad_ref[halo:t_len + halo, :] = u_ref[0]
    w = w_ref[0]
    b = b_ref[0]
    first = halo - k_conv // 2

    def body(i, carry):
        r0 = pl.multiple_of(i * rows, rows)
        win = pad_ref[pl.ds(r0, rows + 2 * halo), :]
        acc = b + w[0:1, :] * win[first:first + rows]
        for k in range(1, k_conv):
            acc = acc + w[k:k + 1, :] * win[first + k:first + k + rows]
        o_ref[0, pl.ds(r0, rows), :] = acc * jax.nn.sigmoid(acc)
        return carry

    lax.fori_loop(0, t_len // rows, body, 0)


def conv_silu(proj3, conv_w, conv_b, layer, col0, n_ch, tc=512, rows=128):
    b_sz, t_len, _ = proj3.shape
    k_conv = conv_w.shape[1]
    c0 = col0 // tc
    return pl.pallas_call(
        functools.partial(_conv_silu_body, t_len=t_len, k_conv=k_conv, rows=rows),
        grid=(b_sz, n_ch // tc),
        in_specs=[pl.BlockSpec((1, t_len, tc), lambda b, j: (b, 0, c0 + j)),
                  pl.BlockSpec((1, k_conv, tc), lambda b, j: (layer, 0, j)),
                  pl.BlockSpec((1, 1, tc), lambda b, j: (layer, 0, j))],
        out_specs=pl.BlockSpec((1, t_len, tc), lambda b, j: (b, 0, j)),
        out_shape=jax.ShapeDtypeStruct((b_sz, t_len, n_ch), F32),
        scratch_shapes=[pltpu.VMEM((t_len + 16, tc), F32)],
        compiler_params=_params(2),
        name="ssd_conv_silu",
    )(proj3, conv_w, conv_b.reshape(conv_b.shape[0], 1, conv_b.shape[1]))


def _softplus(x):
    return jnp.maximum(x, 0.0) + jnp.log1p(jnp.exp(-jnp.abs(x)))


def _ssd_body(xs_ref, b_ref, c_ref, dt_ref, bias_ref, alog_ref, y_ref, state_ref,
              *, chunk, n_heads, head_dim, n_groups):
    d = pl.program_id(0)
    step = pl.program_id(2)
    L = chunk
    hpg = n_heads // n_groups
    gw = hpg * head_dim

    @pl.when(step == 0)
    def _():
        state_ref[...] = jnp.zeros_like(state_ref)

    dt = _softplus(dt_ref[0] + bias_ref[0])
    a_neg = -jnp.exp(alog_ref[0])
    d_a = dt * a_neg
    row = lax.broadcasted_iota(I32, (L, L), 0)
    col = lax.broadcasted_iota(I32, (L, L), 1)
    fwd = d == 0
    keep = (row - col) * (1 - 2 * d) >= 0
    x_cum = jnp.dot(keep.astype(F32), d_a, precision=HIGHEST, preferred_element_type=F32)
    x_cum_t = x_cum.T
    total = jnp.where(fwd, x_cum[L - 1:L, :], x_cum[0:1, :])
    e_out = jnp.exp(x_cum)
    e_in = jnp.exp(total - x_cum)
    e_tot = jnp.exp(total)

    lane = lax.broadcasted_iota(I32, (L, LANES), 1)
    lo = lane < head_dim
    lo_row = lo[0:1, :]

    def expand(arr, h0):
        return jnp.where(lo if arr.shape[0] == L else lo_row, arr[:, h0:h0 + 1], arr[:, h0 + 1:h0 + 2])

    for g in range(n_groups):
        b_g = b_ref[0, :, g * SSD_STATE:(g + 1) * SSD_STATE]
        c_g = c_ref[0, :, g * SSD_STATE:(g + 1) * SSD_STATE].astype(BF16)
        b_gt = b_g.T.astype(BF16)
        cb = jnp.dot(c_g, b_gt, preferred_element_type=F32)
        y_off = jnp.dot(c_g, state_ref[g].astype(BF16), preferred_element_type=F32)
        x_in = []
        dec = []
        for pp in range(hpg // 2):
            h0 = g * hpg + 2 * pp
            c0 = h0 * head_dim
            xdt = xs_ref[0, :, c0:c0 + LANES] * expand(dt, h0)
            m0 = cb * jnp.exp(jnp.where(keep, x_cum[:, h0:h0 + 1] - x_cum_t[h0:h0 + 1, :], -jnp.inf))
            m1 = cb * jnp.exp(jnp.where(keep, x_cum[:, h0 + 1:h0 + 2] - x_cum_t[h0 + 1:h0 + 2, :], -jnp.inf))
            lhs = jnp.concatenate([m0.astype(BF16), m1.astype(BF16)], axis=1)
            rhs = jnp.concatenate([jnp.where(lo, xdt, 0.0).astype(BF16),
                                   jnp.where(lo, 0.0, xdt).astype(BF16)], axis=0)
            y_diag = jnp.dot(lhs, rhs, preferred_element_type=F32)
            lc = 2 * pp * head_dim
            y_ref[0, 0, :, c0:c0 + LANES] = y_diag + y_off[:, lc:lc + LANES] * expand(e_out, h0)
            x_in.append((xdt * expand(e_in, h0)).astype(BF16))
            dec.append(expand(e_tot, h0))
        new_t = jnp.dot(b_gt, jnp.concatenate(x_in, axis=1), preferred_element_type=F32)
        state_ref[g] = state_ref[g] * jnp.concatenate(dec, axis=1) + new_t


def ssd_scan(xbc, dt_raw, dt_bias, a_log, layer, *, n_heads):
    b_sz, t_len, _ = xbc.shape
    d_ssd = n_heads * SSD_HEADDIM
    gn = SSD_GROUPS * SSD_STATE
    L = SSD_CHUNK
    nc = t_len // L
    xb = d_ssd // gn

    def cidx(d, c):
        return c + d * (nc - 1 - 2 * c)

    return pl.pallas_call(
        functools.partial(_ssd_body, chunk=L, n_heads=n_heads, head_dim=SSD_HEADDIM,
                          n_groups=SSD_GROUPS),
        grid=(2, b_sz, nc),
        in_specs=[pl.BlockSpec((1, L, d_ssd), lambda d, b, c: (b, cidx(d, c), 0)),
                  pl.BlockSpec((1, L, gn), lambda d, b, c: (b, cidx(d, c), xb)),
                  pl.BlockSpec((1, L, gn), lambda d, b, c: (b, cidx(d, c), xb + 1)),
                  pl.BlockSpec((1, L, LANES), lambda d, b, c: (b, cidx(d, c), d)),
                  pl.BlockSpec((1, 1, LANES), lambda d, b, c: (layer * 2 + d, 0, 0)),
                  pl.BlockSpec((1, 1, LANES), lambda d, b, c: (layer * 2 + d, 0, 0))],
        out_specs=pl.BlockSpec((1, 1, L, d_ssd), lambda d, b, c: (d, b, cidx(d, c), 0)),
        out_shape=jax.ShapeDtypeStruct((2, b_sz, t_len, d_ssd), F32),
        scratch_shapes=[pltpu.VMEM((SSD_GROUPS, SSD_STATE, d_ssd // SSD_GROUPS), F32)],
        compiler_params=_params(3),
        name="ssd_scan",
    )(xbc, xbc, xbc, dt_raw, dt_bias, a_log)


def _ssd_gate_body(y_ref, xs_ref, z_ref, dskip_ref, g_ref, o_ref, *, n_groups):
    y = y_ref[0] + y_ref[1] + dskip_ref[...] * xs_ref[...]
    z = z_ref[...]
    y = y * (z * jax.nn.sigmoid(z))
    gw = y.shape[1] // n_groups
    for g in range(n_groups):
        yg = y[:, g * gw:(g + 1) * gw]
        ms = jnp.mean(yg * yg, axis=-1, keepdims=True)
        o_ref[:, g * gw:(g + 1) * gw] = (
            yg * lax.rsqrt(ms + EPS) * g_ref[:, g * gw:(g + 1) * gw]).astype(o_ref.dtype)


def ssd_gate(y2, xbc2, proj2, d_skip_cols, gate_g, d_ssd, tm=256):
    m = xbc2.shape[0]
    return pl.pallas_call(
        functools.partial(_ssd_gate_body, n_groups=SSD_GROUPS),
        grid=(m // tm,),
        in_specs=[pl.BlockSpec((2, tm, d_ssd), lambda i: (0, i, 0)),
                  pl.BlockSpec((tm, d_ssd), lambda i: (i, 0)),
                  pl.BlockSpec((tm, d_ssd), lambda i: (i, 0)),
                  pl.BlockSpec((1, d_ssd), lambda i: (0, 0)),
                  pl.BlockSpec((1, d_ssd), lambda i: (0, 0))],
        out_specs=pl.BlockSpec((tm, d_ssd), lambda i: (i, 0)),
        out_shape=jax.ShapeDtypeStruct((m, d_ssd), BF16),
        compiler_params=_params(1),
        name="ssd_gate_norm",
    )(y2, xbc2, proj2, d_skip_cols, gate_g)


def _softmax_rows(s):
    m = jnp.max(s, axis=-1, keepdims=True)
    e = jnp.exp(s - m)
    return e / jnp.sum(e, axis=-1, keepdims=True)


def _mem_attn_body(q_ref, kv_ref, o_ref, *, n_heads, dh):
    scale = dh ** -0.5
    for h in range(n_heads):
        q = q_ref[0, :, h * dh:(h + 1) * dh]
        k = kv_ref[0, :, h * dh:(h + 1) * dh]
        v = kv_ref[0, :, (n_heads + h) * dh:(n_heads + h + 1) * dh]
        p = _softmax_rows(_nt_dot(q, k) * scale).astype(BF16)
        o_ref[0, :, h * dh:(h + 1) * dh] = jnp.dot(p, v, preferred_element_type=F32).astype(o_ref.dtype)


def mem_attention(q3, q_col_block, kv3, tm=512):
    b_sz, t_len, _ = q3.shape
    d_xa = XA_HEADS * XA_DH
    mem_len = kv3.shape[1]
    return pl.pallas_call(
        functools.partial(_mem_attn_body, n_heads=XA_HEADS, dh=XA_DH),
        grid=(b_sz, t_len // tm),
        in_specs=[pl.BlockSpec((1, tm, d_xa), lambda b, i: (b, i, q_col_block)),
                  pl.BlockSpec((1, mem_len, 2 * d_xa), lambda b, i: (b, 0, 0))],
        out_specs=pl.BlockSpec((1, tm, d_xa), lambda b, i: (b, i, 0)),
        out_shape=jax.ShapeDtypeStruct((b_sz, t_len, d_xa), BF16),
        compiler_params=_params(2),
        name="mem_attention",
    )(q3, kv3)


def _na_bias_body(rpb_ref, o_ref, *, n_dr, width, kc):
    c_idx = lax.broadcasted_iota(I32, (width, LANES), 0)
    lane = lax.broadcasted_iota(I32, (width, LANES), 1)
    k_idx = jnp.where(lane < width, lane, lane - width)
    w_start = jnp.clip(c_idx - kc // 2, 0, width - kc)
    valid = (k_idx >= w_start) & (k_idx < w_start + kc)
    base = LANES - (kc - 1)
    rolled = []
    for d in range(n_dr):
        x = jnp.broadcast_to(rpb_ref[0, d:d + 1, :], (width, LANES))
        rolled.append((pltpu.roll(x, base, 1, stride=1, stride_axis=0),
                       pltpu.roll(x, (base + width) % LANES, 1, stride=1, stride_axis=0)))
    for d in range(n_dr - 1):
        o_ref[0, d] = jnp.where(valid, jnp.where(lane < width, rolled[d][0], rolled[d + 1][1]), -1e30)


def na_bias_table(rpb_padded):
    n_heads, n_dr, _ = rpb_padded.shape
    return pl.pallas_call(
        functools.partial(_na_bias_body, n_dr=n_dr, width=GRID_W, kc=NA_KC),
        grid=(n_heads,),
        in_specs=[pl.BlockSpec((1, n_dr, LANES), lambda h: (h, 0, 0))],
        out_specs=pl.BlockSpec((1, n_dr - 1, GRID_W, 2 * GRID_W), lambda h: (h, 0, 0, 0)),
        out_shape=jax.ShapeDtypeStruct((n_heads, n_dr - 1, GRID_W, 2 * GRID_W), F32),
        compiler_params=_params(1),
        name="na_bias_table",
    )(rpb_padded)


def _na_body(q_ref, k_ref, v_ref, tb_ref, o_ref, *, rows, width, kr, dh):
    scale = dh ** -0.5
    nkeys = kr * width

    def body(r, carry):
        rs = jnp.clip(r - kr // 2, 0, rows - kr)
        dr0 = rs - r + (NA_KR - 1)
        q = q_ref[0, pl.ds(pl.multiple_of(r * width, width), width), :]
        k0 = pl.multiple_of(rs * width, width)
        kw = k_ref[0, pl.ds(k0, nkeys), :]
        vw = v_ref[0, pl.ds(k0, nkeys), :]
        s = _nt_dot(q, kw) * scale
        s = jnp.concatenate(
            [s[:, j * LANES:(j + 1) * LANES] + tb_ref[0, dr0 + 2 * j] for j in range(nkeys // LANES)],
            axis=1)
        p = _softmax_rows(s).astype(BF16)
        o_ref[0, pl.ds(pl.multiple_of(r * width, width), width), :] = jnp.dot(
            p, vw, preferred_element_type=F32).astype(o_ref.dtype)
        return carry

    lax.fori_loop(0, rows, body, 0)


def na_attention(proj3, table, n_heads):
    b_sz, t_len, _ = proj3.shape
    rows = t_len // GRID_W
    kr = min(NA_KR, rows)
    n_dr2 = table.shape[1]
    return pl.pallas_call(
        functools.partial(_na_body, rows=rows, width=GRID_W, kr=kr, dh=NA_DH),
        grid=(b_sz, n_heads),
        in_specs=[pl.BlockSpec((1, t_len, NA_DH), lambda b, h: (b, 0, h)),
                  pl.BlockSpec((1, t_len, NA_DH), lambda b, h: (b, 0, n_heads + h)),
                  pl.BlockSpec((1, t_len, NA_DH), lambda b, h: (b, 0, 2 * n_heads + h)),
                  pl.BlockSpec((1, n_dr2, GRID_W, 2 * GRID_W), lambda b, h: (h, 0, 0, 0))],
        out_specs=pl.BlockSpec((1, t_len, NA_DH), lambda b, h: (b, 0, h)),
        out_shape=jax.ShapeDtypeStruct((b_sz, t_len, n_heads * NA_DH), BF16),
        compiler_params=_params(2),
        name="na_attention",
    )(proj3, proj3, proj3, table)


def _router_body(x_ref, g_ref, w_ref, h_ref, aff_ref):
    x = x_ref[...]
    ms = jnp.mean(x * x, axis=-1, keepdims=True)
    hn = x * lax.rsqrt(ms + EPS) * g_ref[...]
    h_ref[...] = hn.astype(BF16)
    logits = jnp.dot(hn, w_ref[0], precision=HIGHEST, preferred_element_type=F32)
    aff_ref[...] = _softmax_rows(logits)


def moe_router(x2d, g, w_router, layer, tm=256):
    m, d = x2d.shape
    n_exp = w_router.shape[-1]
    return pl.pallas_call(
        _router_body,
        grid=(m // tm,),
        in_specs=[pl.BlockSpec((tm, d), lambda i: (i, 0)),
                  pl.BlockSpec((1, d), lambda i: (0, 0)),
                  pl.BlockSpec((1, d, n_exp), lambda i: (layer, 0, 0))],
        out_specs=[pl.BlockSpec((tm, d), lambda i: (i, 0)),
                   pl.BlockSpec((tm, n_exp), lambda i: (i, 0))],
        out_shape=[jax.ShapeDtypeStruct((m, d), BF16),
                   jax.ShapeDtypeStruct((m, n_exp), F32)],
        compiler_params=_params(1),
        name="moe_router",
    )(x2d, g.reshape(1, d), w_router)


def _select_body(aff_ref, enc_ref, encrow_ref, *, cap, row_blk):
    a = aff_ref[0]
    t_len, n_exp = a.shape
    bits = lax.bitcast_convert_type(a, I32)
    zero = jnp.zeros((1, n_exp), I32)

    def count(pred):
        return jnp.sum(pred.astype(I32), axis=0, keepdims=True)

    def value_bit(i, prefix):
        cand = prefix | jnp.left_shift(jnp.int32(1), 30 - i)
        return jnp.where(count(bits >= cand) >= cap, cand, prefix)

    thresh = lax.fori_loop(0, 31, value_bit, zero)
    above = bits > thresh
    tied = bits == thresh
    need = cap - count(above)
    idx = lax.broadcasted_iota(I32, (t_len, n_exp), 0)
    n_idx_bits = (t_len - 1).bit_length()

    def index_bit(i, last):
        cand = last | jnp.left_shift(jnp.int32(1), n_idx_bits - 1 - i)
        return jnp.where(count(tied & (idx < cand)) < need, cand, last)

    last = lax.fori_loop(0, n_idx_bits, index_bit, zero)
    mask = above | (tied & (idx <= last))
    mask_bf = mask.astype(BF16)

    col = lax.broadcasted_iota(I32, (row_blk, t_len), 1)
    for i in range(t_len // row_blk):
        rowi = lax.broadcasted_iota(I32, (row_blk, t_len), 0) + i * row_blk
        before = jnp.dot((rowi > col).astype(BF16), mask_bf, preferred_element_type=F32)
        enc_ref[0, i * row_blk:(i + 1) * row_blk, :] = jnp.where(
            mask[i * row_blk:(i + 1) * row_blk], before + 1.0, 0.0).astype(I32)

    eye = (lax.broadcasted_iota(I32, (n_exp, n_exp), 0)
           == lax.broadcasted_iota(I32, (n_exp, n_exp), 1)).astype(BF16)
    enc_bf = enc_ref[0].astype(F32).astype(BF16)
    encrow_ref[0] = _nt_dot(eye, enc_bf).astype(I32)


def moe_select(aff3, cap):
    b_sz, t_len, n_exp = aff3.shape
    assert cap <= 256, "slot codes must stay exactly representable in bf16"
    return pl.pallas_call(
        functools.partial(_select_body, cap=cap, row_blk=256),
        grid=(b_sz,),
        in_specs=[pl.BlockSpec((1, t_len, n_exp), lambda b: (b, 0, 0))],
        out_specs=[pl.BlockSpec((1, t_len, n_exp), lambda b: (b, 0, 0)),
                   pl.BlockSpec((1, n_exp, t_len), lambda b: (b, 0, 0))],
        out_shape=[jax.ShapeDtypeStruct((b_sz, t_len, n_exp), I32),
                   jax.ShapeDtypeStruct((b_sz, n_exp, t_len), I32)],
        compiler_params=_params(1),
        name="moe_select",
    )(aff3)


def _gather_body(h_ref, encrow_ref, xs_ref, *, cap):
    e = pl.program_id(1)
    t_len = h_ref.shape[1]
    code = encrow_ref[0, pl.ds(e, 1), :]
    slot = lax.broadcasted_iota(I32, (cap, t_len), 0) + 1
    onehot = (slot == code).astype(BF16)
    xs_ref[0] = jnp.dot(onehot, h_ref[0], preferred_element_type=F32).astype(BF16)


def moe_gather(h3, encrow, cap):
    b_sz, t_len, d = h3.shape
    n_exp = encrow.shape[1]
    return pl.pallas_call(
        functools.partial(_gather_body, cap=cap),
        grid=(b_sz, n_exp),
        in_specs=[pl.BlockSpec((1, t_len, d), lambda b, e: (b, 0, 0)),
                  pl.BlockSpec((1, n_exp, t_len), lambda b, e: (b, 0, 0))],
        out_specs=pl.BlockSpec((1, cap, d), lambda b, e: (e, b, 0)),
        out_shape=jax.ShapeDtypeStruct((n_exp, b_sz * cap, d), BF16),
        compiler_params=_params(2),
        name="moe_gather",
    )(h3, encrow)


def _ffn_body(xs_ref, w1_ref, w3_ref, w2_ref, y_ref, acc_ref):
    f = pl.program_id(1)
    xs = xs_ref[0]
    a = jnp.dot(xs, w1_ref[0, 0].astype(BF16), preferred_element_type=F32)
    b = jnp.dot(xs, w3_ref[0, 0].astype(BF16), preferred_element_type=F32)
    hid = (a * jax.nn.sigmoid(a) * b).astype(BF16)
    part = jnp.dot(hid, w2_ref[0, 0].astype(BF16), preferred_element_type=F32)

    @pl.when(f == 0)
    def _():
        acc_ref[...] = part

    @pl.when(f > 0)
    def _():
        acc_ref[...] += part

    @pl.when(f == pl.num_programs(1) - 1)
    def _():
        y_ref[0] = acc_ref[...].astype(y_ref.dtype)


def moe_ffn(xs, w1, w3, w2, layer, tf=256):
    n_exp, rows, d = xs.shape
    d_exp = w1.shape[-1]
    return pl.pallas_call(
        _ffn_body,
        grid=(n_exp, d_exp // tf),
        in_specs=[pl.BlockSpec((1, rows, d), lambda e, f: (e, 0, 0)),
                  pl.BlockSpec((1, 1, d, tf), lambda e, f: (layer, e, 0, f)),
                  pl.BlockSpec((1, 1, d, tf), lambda e, f: (layer, e, 0, f)),
                  pl.BlockSpec((1, 1, tf, d), lambda e, f: (layer, e, f, 0))],
        out_specs=pl.BlockSpec((1, rows, d), lambda e, f: (e, 0, 0)),
        out_shape=jax.ShapeDtypeStruct((n_exp, rows, d), BF16),
        scratch_shapes=[pltpu.VMEM((rows, d), F32)],
        compiler_params=_params(2),
        name="moe_ffn",
    )(xs, w1, w3, w2)


def _scatter_body(x_ref, y_ref, enc_ref, aff_ref, o_ref, *, cap):
    o_ref[0] = x_ref[0]
    enc = enc_ref[0]
    aff = aff_ref[0]
    t_len, n_exp = enc.shape
    slot = lax.broadcasted_iota(I32, (t_len, cap), 1) + 1
    for e in range(n_exp):
        onehot = (enc[:, e:e + 1] == slot).astype(BF16)
        o_ref[0] += aff[:, e:e + 1] * jnp.dot(onehot, y_ref[e], preferred_element_type=F32)


def moe_scatter_add(x3, y, enc, aff3, cap, tn=512):
    b_sz, t_len, d = x3.shape
    n_exp = enc.shape[-1]
    return pl.pallas_call(
        functools.partial(_scatter_body, cap=cap),
        grid=(b_sz, d // tn),
        in_specs=[pl.BlockSpec((1, t_len, tn), lambda b, n: (b, 0, n)),
                  pl.BlockSpec((n_exp, cap, tn), lambda b, n: (0, b, n)),
                  pl.BlockSpec((1, t_len, n_exp), lambda b, n: (b, 0, 0)),
                  pl.BlockSpec((1, t_len, n_exp), lambda b, n: (b, 0, 0))],
        out_specs=pl.BlockSpec((1, t_len, tn), lambda b, n: (b, 0, n)),
        out_shape=jax.ShapeDtypeStruct((b_sz, t_len, d), F32),
        compiler_params=_params(2),
        name="moe_scatter_add",
    )(x3, y, enc, aff3)


def ec_moe_layer(x3, norm_g, w_router, w1, w3, w2, layer):
    b_sz, t_len, d = x3.shape
    n_exp = w_router.shape[-1]
    cap = EC_FACTOR * t_len // n_exp
    h2, aff2 = moe_router(x3.reshape(b_sz * t_len, d), norm_g, w_router, layer)
    aff3 = aff2.reshape(b_sz, t_len, n_exp)
    enc, encrow = moe_select(aff3, cap)
    xs = moe_gather(h2.reshape(b_sz, t_len, d), encrow, cap)
    y = moe_ffn(xs, w1, w3, w2, layer)
    return moe_scatter_add(x3, y, enc, aff3, cap)


def _layer_cols(w, layer, col0, n_cols):
    return lax.slice(w, (layer, 0, col0), (layer + 1, w.shape[1], col0 + n_cols))[0]


def ssd_mixer_layer(x2, h2, kv3, b_sz, t_len, j, ssd_w_in, conv_w, conv_b, dt_bias, a_log, d_skip,
                    gate_g, w_out):
    m, d = x2.shape
    n_heads = d_skip.shape[-1]
    d_ssd = n_heads * SSD_HEADDIM
    gn = SSD_GROUPS * SSD_STATE
    conv_dim = d_ssd + 2 * gn
    d_xa = XA_HEADS * XA_DH
    n_main = d_ssd + conv_dim
    tn = 1024
    proj = matmul_ws([h2], [(ssd_w_in, (1, d, tn), lambda n: (j, 0, n))], n_main,
                     tn=tn, tm=512, out_dtype=F32, name="ssd_in_proj")
    w_dt = _layer_cols(ssd_w_in, j, n_main, 2 * n_heads)
    w_dt = jnp.pad(w_dt.reshape(d, 2, n_heads), ((0, 0), (0, 0), (0, LANES - n_heads))).reshape(d, 2 * LANES)
    dt_raw = matmul_ws([h2], [(w_dt, (d, 2 * LANES), lambda n: (0, 0))], 2 * LANES,
                       tn=2 * LANES, tm=512, out_dtype=F32, name="ssd_dt_proj")
    w_q = _layer_cols(ssd_w_in, j, n_main + 2 * n_heads, d_xa)
    xq = matmul_ws([h2], [(w_q, (d, d_xa), lambda n: (0, 0))], d_xa,
                   tn=d_xa, tm=512, out_dtype=BF16, name="ssd_xq_proj")

    proj3 = proj.reshape(b_sz, t_len, n_main)
    xbc = conv_silu(proj3, conv_w, conv_b, j, d_ssd, conv_dim)
    pad_h = ((0, 0), (0, 0), (0, LANES - n_heads))
    y2 = ssd_scan(xbc, dt_raw.reshape(b_sz, t_len, 2 * LANES),
                  jnp.pad(dt_bias, pad_h).reshape(-1, 1, LANES),
                  jnp.pad(a_log, pad_h).reshape(-1, 1, LANES), j, n_heads=n_heads)
    y = ssd_gate(y2.reshape(2, m, d_ssd), xbc.reshape(m, conv_dim), proj,
                 jnp.repeat(d_skip[j], SSD_HEADDIM).reshape(1, d_ssd), gate_g[j].reshape(1, d_ssd), d_ssd)
    o_x = mem_attention(xq.reshape(b_sz, t_len, d_xa), 0, kv3).reshape(m, d_xa)
    return matmul_ws([y, o_x],
                     [(w_out, (1, d_ssd, 512), lambda n: (j, 0, n)),
                      (w_out, (1, d_xa, 512), lambda n: (j, d_ssd // d_xa, n))],
                     d, tn=512, tm=512, out_dtype=F32, res=x2, name="ssd_out_proj")


def na_mixer_layer(x2, h2, kv3, b_sz, t_len, j, na_w_in, na_rpb, w_out):
    m, d = x2.shape
    n_heads = na_rpb.shape[1]
    d_na = n_heads * NA_DH
    d_xa = XA_HEADS * XA_DH
    n_in = 3 * d_na + d_xa
    tn = 1024
    proj = matmul_ws([h2], [(na_w_in, (1, d, tn), lambda n: (j, 0, n))], n_in,
                     tn=tn, tm=512, out_dtype=BF16, name="na_in_proj")
    proj3 = proj.reshape(b_sz, t_len, n_in)
    rpb = na_rpb[j]
    table = na_bias_table(jnp.pad(rpb, ((0, 0), (0, 0), (0, LANES - rpb.shape[-1]))))
    o_na = na_attention(proj3, table, n_heads).reshape(m, d_na)
    o_x = mem_attention(proj3, 3 * d_na // d_xa, kv3).reshape(m, d_xa)
    return matmul_ws([o_na, o_x],
                     [(w_out, (1, d_na, 512), lambda n: (j, 0, n)),
                      (w_out, (1, d_xa, 512), lambda n: (j, d_na // d_xa, n))],
                     d, tn=512, tm=512, out_dtype=F32, res=x2, name="na_out_proj")


def kernel(x, mem, norm_mix_g, norm_ffn_g, norm_final_g, mem_norm_g, ssd_w_in, ssd_conv_w, ssd_conv_b,
           ssd_dt_bias, ssd_a_log, ssd_d, ssd_gate_norm_g, ssd_w_out, na_w_in, na_rpb, na_w_out, xa_w_kv,
           moe_w_router, moe_w1, moe_w3, moe_w2):
    b_sz, t_len, d = x.shape
    mem_len = mem.shape[1]
    depth = norm_mix_g.shape[0]
    m = b_sz * t_len
    d_kv = xa_w_kv.shape[-1]
    mem_n = rmsnorm_rows(mem.reshape(b_sz * mem_len, d), mem_norm_g, BF16)
    x2 = x.reshape(m, d)
    for i in range(depth):
        j = i // N_MIXERS
        kv3 = matmul_ws([mem_n], [(xa_w_kv, (1, d, d_kv), lambda n, i=i: (i, 0, 0))], d_kv,
                        tn=d_kv, tm=512, out_dtype=BF16, name="xa_kv_proj").reshape(b_sz, mem_len, d_kv)
        h2 = rmsnorm_rows(x2, norm_mix_g[i], BF16)
        if i % N_MIXERS == 0:
            x2 = ssd_mixer_layer(x2, h2, kv3, b_sz, t_len, j, ssd_w_in, ssd_conv_w, ssd_conv_b,
                                 ssd_dt_bias, ssd_a_log, ssd_d, ssd_gate_norm_g, ssd_w_out)
        else:
            x2 = na_mixer_layer(x2, h2, kv3, b_sz, t_len, j, na_w_in, na_rpb, na_w_out)
        x2 = ec_moe_layer(x2.reshape(b_sz, t_len, d), norm_ffn_g[i], moe_w_router, moe_w1, moe_w3,
                          moe_w2, i).reshape(m, d)
    return rmsnorm_rows(x2, norm_final_g, F32).reshape(b_sz, t_len, d)
```

```python
import functools

import jax
import jax.numpy as jnp
from jax import lax
from jax.experimental import pallas as pl
from jax.experimental.pallas import tpu as pltpu

F32 = jnp.float32
BF16 = jnp.bfloat16
I32 = jnp.int32
HIGHEST = lax.Precision.HIGHEST

EPS = 1e-6
LANES = 128
VMEM_LIMIT_BYTES = 56 * 1024 * 1024

XA_HEADS = 4
XA_DH = 128
SSD_HEADDIM = 64
SSD_GROUPS = 4
SSD_STATE = 128
SSD_CONV = 5
SSD_CHUNK = 128
NA_DH = 128
NA_KR = 8
NA_KC = 16
GRID_W = 64
EC_FACTOR = 2
N_MIXERS = 2


def _params(n_grid_dims):
    return pltpu.CompilerParams(
        dimension_semantics=("arbitrary",) * n_grid_dims,
        vmem_limit_bytes=VMEM_LIMIT_BYTES)


def _nt_dot(a, b):
    return lax.dot_general(a, b, (((1,), (1,)), ((), ())), preferred_element_type=F32)


def _rmsnorm_body(x_ref, g_ref, o_ref):
    x = x_ref[...]
    ms = jnp.mean(x * x, axis=-1, keepdims=True)
    o_ref[...] = (x * lax.rsqrt(ms + EPS) * g_ref[...]).astype(o_ref.dtype)


def rmsnorm_rows(x2d, g, out_dtype, tm=256):
    m, d = x2d.shape
    return pl.pallas_call(
        _rmsnorm_body,
        grid=(m // tm,),
        in_specs=[pl.BlockSpec((tm, d), lambda i: (i, 0)),
                  pl.BlockSpec((1, d), lambda i: (0, 0))],
        out_specs=pl.BlockSpec((tm, d), lambda i: (i, 0)),
        out_shape=jax.ShapeDtypeStruct((m, d), out_dtype),
        compiler_params=_params(1),
        name="rmsnorm",
    )(x2d, g.reshape(1, d))


def _matmul_body(*refs, n_in, has_res):
    a_refs = refs[:n_in]
    w_refs = refs[n_in:2 * n_in]
    res_ref = refs[2 * n_in] if has_res else None
    o_ref = refs[2 * n_in + has_res]
    wbf_refs = refs[2 * n_in + has_res + 1:]

    @pl.when(pl.program_id(1) == 0)
    def _():
        for w_ref, wbf_ref in zip(w_refs, wbf_refs):
            lead = (0,) * (len(w_ref.shape) - 2)
            wbf_ref[...] = w_ref[lead + (slice(None), slice(None))].astype(BF16)

    acc = None
    for a_ref, wbf_ref in zip(a_refs, wbf_refs):
        t = jnp.dot(a_ref[...], wbf_ref[...], preferred_element_type=F32)
        acc = t if acc is None else acc + t
    if has_res:
        acc = acc + res_ref[...]
    o_ref[...] = acc.astype(o_ref.dtype)


def matmul_ws(a_list, w_list, n_cols, *, tn, tm, out_dtype, res=None, name="matmul"):
    m = a_list[0].shape[0]
    n_in = len(a_list)
    in_specs = [pl.BlockSpec((tm, a.shape[1]), lambda n, i: (i, 0)) for a in a_list]
    scratch = []
    for _, blk, imap in w_list:
        in_specs.append(pl.BlockSpec(blk, lambda n, i, imap=imap: imap(n)))
        scratch.append(pltpu.VMEM(blk[-2:], BF16))
    args = list(a_list) + [w for w, _, _ in w_list]
    if res is not None:
        in_specs.append(pl.BlockSpec((tm, tn), lambda n, i: (i, n)))
        args.append(res)
    return pl.pallas_call(
        functools.partial(_matmul_body, n_in=n_in, has_res=res is not None),
        grid=(n_cols // tn, m // tm),
        in_specs=in_specs,
        out_specs=pl.BlockSpec((tm, tn), lambda n, i: (i, n)),
        out_shape=jax.ShapeDtypeStruct((m, n_cols), out_dtype),
        scratch_shapes=scratch,
        compiler_params=_params(2),
        name=name,
    )(*args)


def _conv_silu_body(u_ref, w_ref, b_ref, o_ref, pad_ref, *, t_len, k_conv, rows):
    halo = 8
    ch = u_ref.shape[-1]
    pad_ref[0:halo, :] = jnp.zeros((halo, ch), F32)
    pad_ref[t_len + halo:t_len + 2 * halo, :] = jnp.zeros((halo, ch), F32)
    pad_ref[halo:t_len + halo, :] = u_ref[0]
    w = w_ref[0]
    b = b_ref[0]
    first = halo - k_conv // 2

    def body(i, carry):
        r0 = pl.multiple_of(i * rows, rows)
        win = pad_ref[pl.ds(r0, rows + 2 * halo), :]
        acc = b + w[0:1, :] * win[first:first + rows]
        for k in range(1, k_conv):
            acc = acc + w[k:k + 1, :] * win[first + k:first + k + rows]
        o_ref[0, pl.ds(r0, rows), :] = acc * jax.nn.sigmoid(acc)
        return carry

    lax.fori_loop(0, t_len // rows, body, 0)


def conv_silu(proj3, conv_w, conv_b, layer, col0, n_ch, tc=512, rows=128):
    b_sz, t_len, _ = proj3.shape
    k_conv = conv_w.shape[1]
    c0 = col0 // tc
    return pl.pallas_call(
        functools.partial(_conv_silu_body, t_len=t_len, k_conv=k_conv, rows=rows),
        grid=(b_sz, n_ch // tc),
        in_specs=[pl.BlockSpec((1, t_len, tc), lambda b, j: (b, 0, c0 + j)),
                  pl.BlockSpec((1, k_conv, tc), lambda b, j: (layer, 0, j)),
                  pl.BlockSpec((1, 1, tc), lambda b, j: (layer, 0, j))],
        out_specs=pl.BlockSpec((1, t_len, tc), lambda b, j: (b, 0, j)),
        out_shape=jax.ShapeDtypeStruct((b_sz, t_len, n_ch), F32),
        scratch_shapes=[pltpu.VMEM((t_len + 16, tc), F32)],
        compiler_params=_params(2),
        name="ssd_conv_silu",
    )(proj3, conv_w, conv_b.reshape(conv_b.shape[0], 1, conv_b.shape[1]))


def _softplus(x):
    return jnp.maximum(x, 0.0) + jnp.log1p(jnp.exp(-jnp.abs(x)))


def _ssd_body(xs_ref, b_ref, c_ref, dt_ref, bias_ref, alog_ref, y_ref, state_ref,
              *, chunk, n_heads, head_dim, n_groups):
    d = pl.program_id(0)
    step = pl.program_id(2)
    L = chunk
    hpg = n_heads // n_groups

    @pl.when(step == 0)
    def _():
        state_ref[...] = jnp.zeros_like(state_ref)

    dt_t = _softplus(dt_ref[0].T[0:n_heads, :] + bias_ref[0])
    da_t = dt_t * (-jnp.exp(alog_ref[0]))
    row = lax.broadcasted_iota(I32, (L, L), 0)
    col = lax.broadcasted_iota(I32, (L, L), 1)
    sign = 1 - 2 * d
    fwd = d == 0
    keep = (row - col) * sign >= 0
    keep_t = (col - row) * sign >= 0
    x_t = jnp.dot(da_t, keep_t.astype(F32), precision=HIGHEST, preferred_element_type=F32)
    tot_t = jnp.where(fwd, x_t[:, L - 1:L], x_t[:, 0:1])
    w_t = dt_t * jnp.exp(tot_t - x_t)
    x_c = jnp.concatenate([x_t, jnp.zeros((LANES - n_heads, L), F32)], axis=0).T
    e_tot = jnp.exp(jnp.where(fwd, x_c[L - 1:L, :], x_c[0:1, :]))

    lo = lax.broadcasted_iota(I32, (L, LANES), 1) < head_dim
    lo_row = lo[0:1, :]

    for g in range(n_groups):
        b_gt = b_ref[0, :, g * SSD_STATE:(g + 1) * SSD_STATE].T
        c_g = c_ref[0, :, g * SSD_STATE:(g + 1) * SSD_STATE].astype(BF16)
        cb = jnp.dot(c_g, b_gt.astype(BF16), preferred_element_type=F32)
        y_off = jnp.dot(c_g, state_ref[g].astype(BF16), preferred_element_type=F32)
        for pp in range(hpg // 2):
            h0 = g * hpg + 2 * pp
            h1 = h0 + 1
            c0 = h0 * head_dim
            lc = 2 * pp * head_dim
            xs_pair = xs_ref[0, :, c0:c0 + LANES]
            rhs = jnp.concatenate([jnp.where(lo, xs_pair, 0.0).astype(BF16),
                                   jnp.where(lo, 0.0, xs_pair).astype(BF16)], axis=0)
            xc0 = jnp.broadcast_to(x_c[:, h0:h0 + 1], (L, L))
            xc1 = jnp.broadcast_to(x_c[:, h1:h1 + 1], (L, L))
            m0 = cb * jnp.exp(jnp.where(keep, xc0 - x_t[h0:h0 + 1, :], -jnp.inf)) * dt_t[h0:h0 + 1, :]
            m1 = cb * jnp.exp(jnp.where(keep, xc1 - x_t[h1:h1 + 1, :], -jnp.inf)) * dt_t[h1:h1 + 1, :]
            y_diag = jnp.dot(jnp.concatenate([m0.astype(BF16), m1.astype(BF16)], axis=1), rhs,
                             preferred_element_type=F32)
            e_out = jnp.where(lo, jnp.exp(xc0), jnp.exp(xc1))
            y_ref[0, 0, :, c0:c0 + LANES] = y_diag + y_off[:, lc:lc + LANES] * e_out
            lhs_b = jnp.concatenate([(b_gt * w_t[h0:h0 + 1, :]).astype(BF16),
                                     (b_gt * w_t[h1:h1 + 1, :]).astype(BF16)], axis=1)
            dec = jnp.where(lo_row, e_tot[:, h0:h0 + 1], e_tot[:, h1:h1 + 1])
            state_ref[g, :, lc:lc + LANES] = (state_ref[g, :, lc:lc + LANES] * dec
                                              + jnp.dot(lhs_b, rhs, preferred_element_type=F32))


def ssd_scan(xbc, dt_raw, dt_bias, a_log, layer, *, n_heads):
    b_sz, t_len, _ = xbc.shape
    d_ssd = n_heads * SSD_HEADDIM
    gn = SSD_GROUPS * SSD_STATE
    L = SSD_CHUNK
    assert L == LANES and 2 * SSD_HEADDIM == LANES and (n_heads // SSD_GROUPS) % 2 == 0
    nc = t_len // L
    xb = d_ssd // gn

    def cidx(d, c):
        return c + d * (nc - 1 - 2 * c)

    return pl.pallas_call(
        functools.partial(_ssd_body, chunk=L, n_heads=n_heads, head_dim=SSD_HEADDIM,
                          n_groups=SSD_GROUPS),
        grid=(2, b_sz, nc),
        in_specs=[pl.BlockSpec((1, L, d_ssd), lambda d, b, c: (b, cidx(d, c), 0)),
                  pl.BlockSpec((1, L, gn), lambda d, b, c: (b, cidx(d, c), xb)),
                  pl.BlockSpec((1, L, gn), lambda d, b, c: (b, cidx(d, c), xb + 1)),
                  pl.BlockSpec((1, L, LANES), lambda d, b, c: (b, cidx(d, c), d)),
                  pl.BlockSpec((1, n_heads, L), lambda d, b, c: (layer * 2 + d, 0, 0)),
                  pl.BlockSpec((1, n_heads, L), lambda d, b, c: (layer * 2 + d, 0, 0))],
        out_specs=pl.BlockSpec((1, 1, L, d_ssd), lambda d, b, c: (d, b, cidx(d, c), 0)),
        out_shape=jax.ShapeDtypeStruct((2, b_sz, t_len, d_ssd), F32),
        scratch_shapes=[pltpu.VMEM((SSD_GROUPS, SSD_STATE, d_ssd // SSD_GROUPS), F32)],
        compiler_params=_params(3),
        name="ssd_scan",
    )(xbc, xbc, xbc, dt_raw, dt_bias, a_log)


def _ssd_gate_body(y_ref, xs_ref, z_ref, dskip_ref, g_ref, o_ref, *, n_groups):
    y = y_ref[0] + y_ref[1] + dskip_ref[...] * xs_ref[...]
    z = z_ref[...]
    y = y * (z * jax.nn.sigmoid(z))
    gw = y.shape[1] // n_groups
    for g in range(n_groups):
        yg = y[:, g * gw:(g + 1) * gw]
        ms = jnp.mean(yg * yg, axis=-1, keepdims=True)
        o_ref[:, g * gw:(g + 1) * gw] = (
            yg * lax.rsqrt(ms + EPS) * g_ref[:, g * gw:(g + 1) * gw]).astype(o_ref.dtype)


def ssd_gate(y2, xbc2, proj2, d_skip_cols, gate_g, d_ssd, tm=256):
    m = xbc2.shape[0]
    return pl.pallas_call(
        functools.partial(_ssd_gate_body, n_groups=SSD_GROUPS),
        grid=(m // tm,),
        in_specs=[pl.BlockSpec((2, tm, d_ssd), lambda i: (0, i, 0)),
                  pl.BlockSpec((tm, d_ssd), lambda i: (i, 0)),
                  pl.BlockSpec((tm, d_ssd), lambda i: (i, 0)),
                  pl.BlockSpec((1, d_ssd), lambda i: (0, 0)),
                  pl.BlockSpec((1, d_ssd), lambda i: (0, 0))],
        out_specs=pl.BlockSpec((tm, d_ssd), lambda i: (i, 0)),
        out_shape=jax.ShapeDtypeStruct((m, d_ssd), BF16),
        compiler_params=_params(1),
        name="ssd_gate_norm",
    )(y2, xbc2, proj2, d_skip_cols, gate_g)


def _softmax_rows(s):
    m = jnp.max(s, axis=-1, keepdims=True)
    e = jnp.exp(s - m)
    return e / jnp.sum(e, axis=-1, keepdims=True)


def _mem_attn_body(q_ref, kv_ref, o_ref, *, n_heads, dh):
    scale = dh ** -0.5
    for h in range(n_heads):
        q = q_ref[0, :, h * dh:(h + 1) * dh]
        k = kv_ref[0, :, h * dh:(h + 1) * dh]
        v = kv_ref[0, :, (n_heads + h) * dh:(n_heads + h + 1) * dh]
        p = _softmax_rows(_nt_dot(q, k) * scale).astype(BF16)
        o_ref[0, :, h * dh:(h + 1) * dh] = jnp.dot(p, v, preferred_element_type=F32).astype(o_ref.dtype)


def mem_attention(q3, q_col_block, kv3, tm=512):
    b_sz, t_len, _ = q3.shape
    d_xa = XA_HEADS * XA_DH
    mem_len = kv3.shape[1]
    return pl.pallas_call(
        functools.partial(_mem_attn_body, n_heads=XA_HEADS, dh=XA_DH),
        grid=(b_sz, t_len // tm),
        in_specs=[pl.BlockSpec((1, tm, d_xa), lambda b, i: (b, i, q_col_block)),
                  pl.BlockSpec((1, mem_len, 2 * d_xa), lambda b, i: (b, 0, 0))],
        out_specs=pl.BlockSpec((1, tm, d_xa), lambda b, i: (b, i, 0)),
        out_shape=jax.ShapeDtypeStruct((b_sz, t_len, d_xa), BF16),
        compiler_params=_params(2),
        name="mem_attention",
    )(q3, kv3)


NA_QROWS = 4
NA_KROWS = 12
NEG_MASK = -1e30


def _na_bias_body(rpb_ref, o_ref, *, n_dr, width, kc):
    c_idx = lax.broadcasted_iota(I32, (width, LANES), 0)
    lane = lax.broadcasted_iota(I32, (width, LANES), 1)
    first = lane < width
    k_idx = jnp.where(first, lane, lane - width)
    w_start = jnp.clip(c_idx - kc // 2, 0, width - kc)
    valid = (k_idx >= w_start) & (k_idx < w_start + kc)
    base = LANES - (kc - 1)
    neg = jnp.full((width, LANES), NEG_MASK, F32)
    lo_half, hi_half = [], []
    for d in range(n_dr):
        x = jnp.broadcast_to(rpb_ref[0, d:d + 1, :], (width, LANES))
        lo_half.append(jnp.where(valid & first, pltpu.roll(x, base, 1, stride=1, stride_axis=0), neg))
        hi_half.append(jnp.where(valid & jnp.logical_not(first),
                                 pltpu.roll(x, (base + width) % LANES, 1, stride=1, stride_axis=0), neg))
    for d in range(n_dr - 1):
        o_ref[0, d] = jnp.where(first, lo_half[d], hi_half[d + 1])
    for d in range(n_dr):
        o_ref[0, n_dr - 1 + d] = lo_half[d]
        o_ref[0, 2 * n_dr - 1 + d] = hi_half[d]
    o_ref[0, 3 * n_dr - 1] = neg


def na_bias_table(rpb_padded):
    n_heads, n_dr, _ = rpb_padded.shape
    n_ent = 3 * n_dr
    return pl.pallas_call(
        functools.partial(_na_bias_body, n_dr=n_dr, width=GRID_W, kc=NA_KC),
        grid=(n_heads,),
        in_specs=[pl.BlockSpec((1, n_dr, LANES), lambda h: (h, 0, 0))],
        out_specs=pl.BlockSpec((1, n_ent, GRID_W, 2 * GRID_W), lambda h: (h, 0, 0, 0)),
        out_shape=jax.ShapeDtypeStruct((n_heads, n_ent, GRID_W, 2 * GRID_W), F32),
        compiler_params=_params(1),
        name="na_bias_table",
    )(rpb_padded)


def _na_table_entry(r, ka, rows, kr, n_dr):
    rs = min(max(r - kr // 2, 0), rows - kr)
    in_a = rs <= ka < rs + kr
    in_b = rs <= ka + 1 < rs + kr
    d_a = ka - r + (NA_KR - 1)
    if in_a and in_b:
        return d_a
    if in_a:
        return n_dr - 1 + d_a
    if in_b:
        return 2 * n_dr - 1 + d_a + 1
    return 3 * n_dr - 1


def _na_body(q_ref, k_ref, v_ref, tb_ref, o_ref, *, rows, width, kr, dh):
    scale = dh ** -0.5
    n_dr = 2 * NA_KR - 1
    nq = NA_QROWS * width
    nk = NA_KROWS * width
    for blk in range(rows // NA_QROWS):
        r0 = blk * NA_QROWS
        k_row0 = min(max(r0 - kr // 2, 0), rows - NA_KROWS)
        q = q_ref[0, r0 * width:r0 * width + nq, :]
        kw = k_ref[0, k_row0 * width:k_row0 * width + nk, :]
        vw = v_ref[0, k_row0 * width:k_row0 * width + nk, :]
        s = _nt_dot(q, kw) * scale
        s = jnp.concatenate([
            jnp.concatenate([
                s[i * width:(i + 1) * width, j * LANES:(j + 1) * LANES]
                + tb_ref[0, _na_table_entry(r0 + i, k_row0 + 2 * j, rows, kr, n_dr)]
                for j in range(nk // LANES)], axis=1)
            for i in range(NA_QROWS)], axis=0)
        m = jnp.max(s, axis=-1, keepdims=True)
        e = jnp.exp(s - m)
        den = jnp.sum(e, axis=-1, keepdims=True)
        o = jnp.dot(e.astype(BF16), vw, preferred_element_type=F32) / den
        o_ref[0, r0 * width:r0 * width + nq, :] = o.astype(o_ref.dtype)


def na_attention(proj3, table, n_heads):
    b_sz, t_len, _ = proj3.shape
    rows = t_len // GRID_W
    kr = min(NA_KR, rows)
    assert kr == NA_KR and rows % NA_QROWS == 0 and rows >= NA_KROWS and (rows - NA_KROWS) % 2 == 0
    assert NA_KROWS >= kr + NA_QROWS - 1 and 2 * GRID_W == LANES
    n_dr2 = table.shape[1]
    return pl.pallas_call(
        functools.partial(_na_body, rows=rows, width=GRID_W, kr=kr, dh=NA_DH),
        grid=(b_sz, n_heads),
        in_specs=[pl.BlockSpec((1, t_len, NA_DH), lambda b, h: (b, 0, h)),
                  pl.BlockSpec((1, t_len, NA_DH), lambda b, h: (b, 0, n_heads + h)),
                  pl.BlockSpec((1, t_len, NA_DH), lambda b, h: (b, 0, 2 * n_heads + h)),
                  pl.BlockSpec((1, n_dr2, GRID_W, 2 * GRID_W), lambda b, h: (h, 0, 0, 0))],
        out_specs=pl.BlockSpec((1, t_len, NA_DH), lambda b, h: (b, 0, h)),
        out_shape=jax.ShapeDtypeStruct((b_sz, t_len, n_heads * NA_DH), BF16),
        compiler_params=_params(2),
        name="na_attention",
    )(proj3, proj3, proj3, table)


def _router_body(x_ref, g_ref, w_ref, h_ref, aff_ref):
    x = x_ref[...]
    ms = jnp.mean(x * x, axis=-1, keepdims=True)
    hn = x * lax.rsqrt(ms + EPS) * g_ref[...]
    hn_hi = hn.astype(BF16)
    h_ref[...] = hn_hi
    hn_lo = (hn - hn_hi.astype(F32)).astype(BF16)
    w = w_ref[0]
    w_hi = w.astype(BF16)
    w_lo = (w - w_hi.astype(F32)).astype(BF16)
    logits = (jnp.dot(hn_hi, w_hi, preferred_element_type=F32)
              + (jnp.dot(hn_hi, w_lo, preferred_element_type=F32)
                 + jnp.dot(hn_lo, w_hi, preferred_element_type=F32)))
    aff_ref[...] = _softmax_rows(logits)


def moe_router(x2d, g, w_router, layer, tm=256):
    m, d = x2d.shape
    n_exp = w_router.shape[-1]
    return pl.pallas_call(
        _router_body,
        grid=(m // tm,),
        in_specs=[pl.BlockSpec((tm, d), lambda i: (i, 0)),
                  pl.BlockSpec((1, d), lambda i: (0, 0)),
                  pl.BlockSpec((1, d, n_exp), lambda i: (layer, 0, 0))],
        out_specs=[pl.BlockSpec((tm, d), lambda i: (i, 0)),
                   pl.BlockSpec((tm, n_exp), lambda i: (i, 0))],
        out_shape=[jax.ShapeDtypeStruct((m, d), BF16),
                   jax.ShapeDtypeStruct((m, n_exp), F32)],
        compiler_params=_params(1),
        name="moe_router",
    )(x2d, g.reshape(1, d), w_router)


def _select_body(aff_ref, enc_ref, encrow_ref, *, cap, row_blk):
    a = aff_ref[0]
    t_len, n_exp = a.shape
    zero = jnp.zeros((1, n_exp), I32)

    def count(pred):
        return jnp.sum(pred.astype(I32), axis=0, keepdims=True)

    def value_bit(i, prefix):
        cand = prefix | jnp.left_shift(jnp.int32(1), 30 - i)
        return jnp.where(count(a >= lax.bitcast_convert_type(cand, F32)) >= cap, cand, prefix)

    thresh = lax.bitcast_convert_type(lax.fori_loop(0, 31, value_bit, zero), F32)
    above = a > thresh
    tied = a == thresh
    need = cap - count(above)
    idx = lax.broadcasted_iota(I32, (t_len, n_exp), 0)
    n_idx_bits = (t_len - 1).bit_length()

    def index_bit(i, last):
        cand = last | jnp.left_shift(jnp.int32(1), n_idx_bits - 1 - i)
        return jnp.where(count(tied & (idx < cand)) < need, cand, last)

    last = lax.fori_loop(0, n_idx_bits, index_bit, zero)
    mask = above | (tied & (idx <= last))
    mask_bf = mask.astype(BF16)

    col = lax.broadcasted_iota(I32, (row_blk, t_len), 1)
    for i in range(t_len // row_blk):
        rowi = lax.broadcasted_iota(I32, (row_blk, t_len), 0) + i * row_blk
        before = jnp.dot((rowi > col).astype(BF16), mask_bf, preferred_element_type=F32)
        enc_ref[0, i * row_blk:(i + 1) * row_blk, :] = jnp.where(
            mask[i * row_blk:(i + 1) * row_blk], before + 1.0, 0.0).astype(I32)

    eye = (lax.broadcasted_iota(I32, (n_exp, n_exp), 0)
           == lax.broadcasted_iota(I32, (n_exp, n_exp), 1)).astype(BF16)
    enc_bf = enc_ref[0].astype(F32).astype(BF16)
    encrow_ref[0] = _nt_dot(eye, enc_bf).astype(I32)


def moe_select(aff3, cap):
    b_sz, t_len, n_exp = aff3.shape
    assert cap <= 256, "slot codes must stay exactly representable in bf16"
    return pl.pallas_call(
        functools.partial(_select_body, cap=cap, row_blk=256),
        grid=(b_sz,),
        in_specs=[pl.BlockSpec((1, t_len, n_exp), lambda b: (b, 0, 0))],
        out_specs=[pl.BlockSpec((1, t_len, n_exp), lambda b: (b, 0, 0)),
                   pl.BlockSpec((1, n_exp, t_len), lambda b: (b, 0, 0))],
        out_shape=[jax.ShapeDtypeStruct((b_sz, t_len, n_exp), I32),
                   jax.ShapeDtypeStruct((b_sz, n_exp, t_len), I32)],
        compiler_params=_params(1),
        name="moe_select",
    )(aff3)


def _gather_body(h_ref, encrow_ref, xs_ref, *, cap):
    e = pl.program_id(1)
    t_len = h_ref.shape[1]
    code = encrow_ref[0, pl.ds(e, 1), :]
    slot = lax.broadcasted_iota(I32, (cap, t_len), 0) + 1
    onehot = (slot == code).astype(BF16)
    xs_ref[0] = jnp.dot(onehot, h_ref[0], preferred_element_type=F32).astype(BF16)


def moe_gather(h3, encrow, cap):
    b_sz, t_len, d = h3.shape
    n_exp = encrow.shape[1]
    return pl.pallas_call(
        functools.partial(_gather_body, cap=cap),
        grid=(b_sz, n_exp),
        in_specs=[pl.BlockSpec((1, t_len, d), lambda b, e: (b, 0, 0)),
                  pl.BlockSpec((1, n_exp, t_len), lambda b, e: (b, 0, 0))],
        out_specs=pl.BlockSpec((1, cap, d), lambda b, e: (e, b, 0)),
        out_shape=jax.ShapeDtypeStruct((n_exp, b_sz * cap, d), BF16),
        compiler_params=_params(2),
        name="moe_gather",
    )(h3, encrow)


def _ffn_up_body(xs_ref, w1_ref, w3_ref, hid_ref):
    xs = xs_ref[0]
    a = jnp.dot(xs, w1_ref[0, 0].astype(BF16), preferred_element_type=F32)
    b = jnp.dot(xs, w3_ref[0, 0].astype(BF16), preferred_element_type=F32)
    hid_ref[0] = (a * jax.nn.sigmoid(a) * b).astype(hid_ref.dtype)


def _ffn_down_body(hid_ref, w2_ref, y_ref):
    y_ref[0] = jnp.dot(hid_ref[0], w2_ref[0, 0].astype(BF16),
                       preferred_element_type=F32).astype(y_ref.dtype)


def moe_ffn(xs, w1, w3, w2, layer, tf=512, tn=2048):
    n_exp, rows, d = xs.shape
    d_exp = w1.shape[-1]
    hid = pl.pallas_call(
        _ffn_up_body,
        grid=(n_exp, d_exp // tf),
        in_specs=[pl.BlockSpec((1, rows, d), lambda e, f: (e, 0, 0)),
                  pl.BlockSpec((1, 1, d, tf), lambda e, f: (layer, e, 0, f)),
                  pl.BlockSpec((1, 1, d, tf), lambda e, f: (layer, e, 0, f))],
        out_specs=pl.BlockSpec((1, rows, tf), lambda e, f: (e, 0, f)),
        out_shape=jax.ShapeDtypeStruct((n_exp, rows, d_exp), BF16),
        compiler_params=_params(2),
        name="moe_ffn_up",
    )(xs, w1, w3)
    return pl.pallas_call(
        _ffn_down_body,
        grid=(n_exp, d // tn),
        in_specs=[pl.BlockSpec((1, rows, d_exp), lambda e, n: (e, 0, 0)),
                  pl.BlockSpec((1, 1, d_exp, tn), lambda e, n: (layer, e, 0, n))],
        out_specs=pl.BlockSpec((1, rows, tn), lambda e, n: (e, 0, n)),
        out_shape=jax.ShapeDtypeStruct((n_exp, rows, d), BF16),
        compiler_params=_params(2),
        name="moe_ffn_down",
    )(hid, w2)


def _scatter_body(x_ref, y_ref, enc_ref, aff_ref, o_ref, *, cap):
    o_ref[0] = x_ref[0]
    enc = enc_ref[0]
    aff = aff_ref[0]
    t_len, n_exp = enc.shape
    slot = lax.broadcasted_iota(I32, (t_len, cap), 1) + 1
    for e in range(n_exp):
        onehot = (enc[:, e:e + 1] == slot).astype(BF16)
        o_ref[0] += aff[:, e:e + 1] * jnp.dot(onehot, y_ref[e], preferred_element_type=F32)


def moe_scatter_add(x3, y, enc, aff3, cap, tn=512):
    b_sz, t_len, d = x3.shape
    n_exp = enc.shape[-1]
    return pl.pallas_call(
        functools.partial(_scatter_body, cap=cap),
        grid=(b_sz, d // tn),
        in_specs=[pl.BlockSpec((1, t_len, tn), lambda b, n: (b, 0, n)),
                  pl.BlockSpec((n_exp, cap, tn), lambda b, n: (0, b, n)),
                  pl.BlockSpec((1, t_len, n_exp), lambda b, n: (b, 0, 0)),
                  pl.BlockSpec((1, t_len, n_exp), lambda b, n: (b, 0, 0))],
        out_specs=pl.BlockSpec((1, t_len, tn), lambda b, n: (b, 0, n)),
        out_shape=jax.ShapeDtypeStruct((b_sz, t_len, d), F32),
        compiler_params=_params(2),
        name="moe_scatter_add",
    )(x3, y, enc, aff3)


def ec_moe_layer(x3, norm_g, w_router, w1, w3, w2, layer):
    b_sz, t_len, d = x3.shape
    n_exp = w_router.shape[-1]
    cap = EC_FACTOR * t_len // n_exp
    h2, aff2 = moe_router(x3.reshape(b_sz * t_len, d), norm_g, w_router, layer)
    aff3 = aff2.reshape(b_sz, t_len, n_exp)
    enc, encrow = moe_select(aff3, cap)
    xs = moe_gather(h2.reshape(b_sz, t_len, d), encrow, cap)
    y = moe_ffn(xs, w1, w3, w2, layer)
    return moe_scatter_add(x3, y, enc, aff3, cap)


def _layer_cols(w, layer, col0, n_cols):
    return lax.slice(w, (layer, 0, col0), (layer + 1, w.shape[1], col0 + n_cols))[0]


def ssd_mixer_layer(x2, h2, kv3, b_sz, t_len, j, ssd_w_in, conv_w, conv_b, dt_bias, a_log, d_skip,
                    gate_g, w_out):
    m, d = x2.shape
    n_heads = d_skip.shape[-1]
    d_ssd = n_heads * SSD_HEADDIM
    gn = SSD_GROUPS * SSD_STATE
    conv_dim = d_ssd + 2 * gn
    d_xa = XA_HEADS * XA_DH
    n_main = d_ssd + conv_dim
    tn = 1024
    proj = matmul_ws([h2], [(ssd_w_in, (1, d, tn), lambda n: (j, 0, n))], n_main,
                     tn=tn, tm=512, out_dtype=F32, name="ssd_in_proj")
    w_dt = _layer_cols(ssd_w_in, j, n_main, 2 * n_heads)
    w_dt = jnp.pad(w_dt.reshape(d, 2, n_heads), ((0, 0), (0, 0), (0, LANES - n_heads))).reshape(d, 2 * LANES)
    dt_raw = matmul_ws([h2], [(w_dt, (d, 2 * LANES), lambda n: (0, 0))], 2 * LANES,
                       tn=2 * LANES, tm=512, out_dtype=F32, name="ssd_dt_proj")
    w_q = _layer_cols(ssd_w_in, j, n_main + 2 * n_heads, d_xa)
    xq = matmul_ws([h2], [(w_q, (d, d_xa), lambda n: (0, 0))], d_xa,
                   tn=d_xa, tm=512, out_dtype=BF16, name="ssd_xq_proj")

    proj3 = proj.reshape(b_sz, t_len, n_main)
    xbc = conv_silu(proj3, conv_w, conv_b, j, d_ssd, conv_dim)
    per_head = (dt_bias.shape[0] * 2, n_heads, SSD_CHUNK)
    y2 = ssd_scan(xbc, dt_raw.reshape(b_sz, t_len, 2 * LANES),
                  jnp.broadcast_to(dt_bias.reshape(-1, n_heads, 1), per_head),
                  jnp.broadcast_to(a_log.reshape(-1, n_heads, 1), per_head), j, n_heads=n_heads)
    y = ssd_gate(y2.reshape(2, m, d_ssd), xbc.reshape(m, conv_dim), proj,
                 jnp.repeat(d_skip[j], SSD_HEADDIM).reshape(1, d_ssd), gate_g[j].reshape(1, d_ssd), d_ssd)
    o_x = mem_attention(xq.reshape(b_sz, t_len, d_xa), 0, kv3).reshape(m, d_xa)
    return matmul_ws([y, o_x],
                     [(w_out, (1, d_ssd, 1024), lambda n: (j, 0, n)),
                      (w_out, (1, d_xa, 1024), lambda n: (j, d_ssd // d_xa, n))],
                     d, tn=1024, tm=512, out_dtype=F32, res=x2, name="ssd_out_proj")


def na_mixer_layer(x2, h2, kv3, b_sz, t_len, j, na_w_in, na_rpb, w_out):
    m, d = x2.shape
    n_heads = na_rpb.shape[1]
    d_na = n_heads * NA_DH
    d_xa = XA_HEADS * XA_DH
    n_in = 3 * d_na + d_xa
    tn = 1024
    proj = matmul_ws([h2], [(na_w_in, (1, d, tn), lambda n: (j, 0, n))], n_in,
                     tn=tn, tm=512, out_dtype=BF16, name="na_in_proj")
    proj3 = proj.reshape(b_sz, t_len, n_in)
    rpb = na_rpb[j]
    table = na_bias_table(jnp.pad(rpb, ((0, 0), (0, 0), (0, LANES - rpb.shape[-1]))))
    o_na = na_attention(proj3, table, n_heads).reshape(m, d_na)
    o_x = mem_attention(proj3, 3 * d_na // d_xa, kv3).reshape(m, d_xa)
    return matmul_ws([o_na, o_x],
                     [(w_out, (1, d_na, 1024), lambda n: (j, 0, n)),
                      (w_out, (1, d_xa, 1024), lambda n: (j, d_na // d_xa, n))],
                     d, tn=1024, tm=512, out_dtype=F32, res=x2, name="na_out_proj")


def kernel(x, mem, norm_mix_g, norm_ffn_g, norm_final_g, mem_norm_g, ssd_w_in, ssd_conv_w, ssd_conv_b,
           ssd_dt_bias, ssd_a_log, ssd_d, ssd_gate_norm_g, ssd_w_out, na_w_in, na_rpb, na_w_out, xa_w_kv,
           moe_w_router, moe_w1, moe_w3, moe_w2):
    b_sz, t_len, d = x.shape
    mem_len = mem.shape[1]
    depth = norm_mix_g.shape[0]
    m = b_sz * t_len
    d_kv = xa_w_kv.shape[-1]
    mem_n = rmsnorm_rows(mem.reshape(b_sz * mem_len, d), mem_norm_g, BF16)
    x2 = x.reshape(m, d)
    for i in range(depth):
        j = i // N_MIXERS
        kv3 = matmul_ws([mem_n], [(xa_w_kv, (1, d, d_kv), lambda n, i=i: (i, 0, 0))], d_kv,
                        tn=d_kv, tm=512, out_dtype=BF16, name="xa_kv_proj").reshape(b_sz, mem_len, d_kv)
        h2 = rmsnorm_rows(x2, norm_mix_g[i], BF16)
        if i % N_MIXERS == 0:
            x2 = ssd_mixer_layer(x2, h2, kv3, b_sz, t_len, j, ssd_w_in, ssd_conv_w, ssd_conv_b,
                                 ssd_dt_bias, ssd_a_log, ssd_d, ssd_gate_norm_g, ssd_w_out)
        else:
            x2 = na_mixer_layer(x2, h2, kv3, b_sz, t_len, j, na_w_in, na_rpb, na_w_out)
        x2 = ec_moe_layer(x2.reshape(b_sz, t_len, d), norm_ffn_g[i], moe_w_router, moe_w1, moe_w3,
                          moe_w2, i).reshape(m, d)
    return rmsnorm_rows(x2, norm_final_g, F32).reshape(b_sz, t_len, d)
```

```python
import functools

import jax
import jax.numpy as jnp
from jax import lax
from jax.experimental import pallas as pl
from jax.experimental.pallas import tpu as pltpu

F32 = jnp.float32
BF16 = jnp.bfloat16
I32 = jnp.int32
HIGHEST = lax.Precision.HIGHEST

EPS = 1e-6
LANES = 128
VMEM_LIMIT_BYTES = 56 * 1024 * 1024

XA_HEADS = 4
XA_DH = 128
SSD_HEADDIM = 64
SSD_GROUPS = 4
SSD_STATE = 128
SSD_CONV = 5
SSD_CHUNK = 128
NA_DH = 128
NA_KR = 8
NA_KC = 16
GRID_W = 64
EC_FACTOR = 2
N_MIXERS = 2


def _params(n_grid_dims):
    return pltpu.CompilerParams(
        dimension_semantics=("arbitrary",) * n_grid_dims,
        vmem_limit_bytes=VMEM_LIMIT_BYTES)


def _nt_dot(a, b):
    return lax.dot_general(a, b, (((1,), (1,)), ((), ())), preferred_element_type=F32)


def _rmsnorm_body(x_ref, g_ref, o_ref):
    x = x_ref[...]
    ms = jnp.mean(x * x, axis=-1, keepdims=True)
    o_ref[...] = (x * lax.rsqrt(ms + EPS) * g_ref[...]).astype(o_ref.dtype)


def rmsnorm_rows(x2d, g, out_dtype, tm=256):
    m, d = x2d.shape
    return pl.pallas_call(
        _rmsnorm_body,
        grid=(m // tm,),
        in_specs=[pl.BlockSpec((tm, d), lambda i: (i, 0)),
                  pl.BlockSpec((1, d), lambda i: (0, 0))],
        out_specs=pl.BlockSpec((tm, d), lambda i: (i, 0)),
        out_shape=jax.ShapeDtypeStruct((m, d), out_dtype),
        compiler_params=_params(1),
        name="rmsnorm",
    )(x2d, g.reshape(1, d))


def _matmul_body(*refs, n_in, has_res):
    a_refs = refs[:n_in]
    w_refs = refs[n_in:2 * n_in]
    res_ref = refs[2 * n_in] if has_res else None
    o_ref = refs[2 * n_in + has_res]
    wbf_refs = refs[2 * n_in + has_res + 1:]

    @pl.when(pl.program_id(1) == 0)
    def _():
        for w_ref, wbf_ref in zip(w_refs, wbf_refs):
            lead = (0,) * (len(w_ref.shape) - 2)
            wbf_ref[...] = w_ref[lead + (slice(None), slice(None))].astype(BF16)

    acc = None
    for a_ref, wbf_ref in zip(a_refs, wbf_refs):
        t = jnp.dot(a_ref[...], wbf_ref[...], preferred_element_type=F32)
        acc = t if acc is None else acc + t
    if has_res:
        acc = acc + res_ref[...]
    o_ref[...] = acc.astype(o_ref.dtype)


def matmul_ws(a_list, w_list, n_cols, *, tn, tm, out_dtype, res=None, name="matmul"):
    m = a_list[0].shape[0]
    n_in = len(a_list)
    in_specs = [pl.BlockSpec((tm, a.shape[1]), lambda n, i: (i, 0)) for a in a_list]
    scratch = []
    for _, blk, imap in w_list:
        in_specs.append(pl.BlockSpec(blk, lambda n, i, imap=imap: imap(n)))
        scratch.append(pltpu.VMEM(blk[-2:], BF16))
    args = list(a_list) + [w for w, _, _ in w_list]
    if res is not None:
        in_specs.append(pl.BlockSpec((tm, tn), lambda n, i: (i, n)))
        args.append(res)
    return pl.pallas_call(
        functools.partial(_matmul_body, n_in=n_in, has_res=res is not None),
        grid=(n_cols // tn, m // tm),
        in_specs=in_specs,
        out_specs=pl.BlockSpec((tm, tn), lambda n, i: (i, n)),
        out_shape=jax.ShapeDtypeStruct((m, n_cols), out_dtype),
        scratch_shapes=scratch,
        compiler_params=_params(2),
        name=name,
    )(*args)


def _conv_silu_body(u_ref, w_ref, b_ref, o_ref, pad_ref, *, t_len, k_conv, rows):
    halo = 8
    ch = u_ref.shape[-1]
    pad_ref[0:halo, :] = jnp.zeros((halo, ch), F32)
    pad_ref[t_len + halo:t_len + 2 * halo, :] = jnp.zeros((halo, ch), F32)
    pad_ref[halo:t_len + halo, :] = u_ref[0]
    w = w_ref[0]
    b = b_ref[0]
    first = halo - k_conv // 2

    def body(i, carry):
        r0 = pl.multiple_of(i * rows, rows)
        win = pad_ref[pl.ds(r0, rows + 2 * halo), :]
        acc = b + w[0:1, :] * win[first:first + rows]
        for k in range(1, k_conv):
            acc = acc + w[k:k + 1, :] * win[first + k:first + k + rows]
        o_ref[0, pl.ds(r0, rows), :] = acc * jax.nn.sigmoid(acc)
        return carry

    lax.fori_loop(0, t_len // rows, body, 0)


def conv_silu(proj3, conv_w, conv_b, layer, col0, n_ch, tc=512, rows=128):
    b_sz, t_len, _ = proj3.shape
    k_conv = conv_w.shape[1]
    c0 = col0 // tc
    return pl.pallas_call(
        functools.partial(_conv_silu_body, t_len=t_len, k_conv=k_conv, rows=rows),
        grid=(b_sz, n_ch // tc),
        in_specs=[pl.BlockSpec((1, t_len, tc), lambda b, j: (b, 0, c0 + j)),
                  pl.BlockSpec((1, k_conv, tc), lambda b, j: (layer, 0, j)),
                  pl.BlockSpec((1, 1, tc), lambda b, j: (layer, 0, j))],
        out_specs=pl.BlockSpec((1, t_len, tc), lambda b, j: (b, 0, j)),
        out_shape=jax.ShapeDtypeStruct((b_sz, t_len, n_ch), F32),
        scratch_shapes=[pltpu.VMEM((t_len + 16, tc), F32)],
        compiler_params=_params(2),
        name="ssd_conv_silu",
    )(proj3, conv_w, conv_b.reshape(conv_b.shape[0], 1, conv_b.shape[1]))


def _softplus(x):
    return jnp.maximum(x, 0.0) + jnp.log1p(jnp.exp(-jnp.abs(x)))


def _ssd_body(xs_ref, b_ref, c_ref, dt_ref, bias_ref, alog_ref, y_ref, state_ref,
              *, chunk, n_heads, head_dim, n_groups):
    d = pl.program_id(0)
    step = pl.program_id(2)
    L = chunk
    hpg = n_heads // n_groups

    @pl.when(step == 0)
    def _():
        state_ref[...] = jnp.zeros_like(state_ref)

    dt_t = _softplus(dt_ref[0].T[0:n_heads, :] + bias_ref[0])
    da_t = dt_t * (-jnp.exp(alog_ref[0]))
    row = lax.broadcasted_iota(I32, (L, L), 0)
    col = lax.broadcasted_iota(I32, (L, L), 1)
    sign = 1 - 2 * d
    fwd = d == 0
    keep = (row - col) * sign >= 0
    keep_t = (col - row) * sign >= 0
    x_t = jnp.dot(da_t, keep_t.astype(F32), precision=HIGHEST, preferred_element_type=F32)
    tot_t = jnp.where(fwd, x_t[:, L - 1:L], x_t[:, 0:1])
    w_t = dt_t * jnp.exp(tot_t - x_t)
    x_c = jnp.concatenate([x_t, jnp.zeros((LANES - n_heads, L), F32)], axis=0).T
    e_tot = jnp.exp(jnp.where(fwd, x_c[L - 1:L, :], x_c[0:1, :]))

    lo = lax.broadcasted_iota(I32, (L, LANES), 1) < head_dim
    lo_row = lo[0:1, :]

    for g in range(n_groups):
        b_gt = b_ref[0, :, g * SSD_STATE:(g + 1) * SSD_STATE].T
        c_g = c_ref[0, :, g * SSD_STATE:(g + 1) * SSD_STATE].astype(BF16)
        cb = jnp.dot(c_g, b_gt.astype(BF16), preferred_element_type=F32)
        y_off = jnp.dot(c_g, state_ref[g].astype(BF16), preferred_element_type=F32)
        for pp in range(hpg // 2):
            h0 = g * hpg + 2 * pp
            h1 = h0 + 1
            c0 = h0 * head_dim
            lc = 2 * pp * head_dim
            xs_pair = xs_ref[0, :, c0:c0 + LANES]
            rhs = jnp.concatenate([jnp.where(lo, xs_pair, 0.0).astype(BF16),
                                   jnp.where(lo, 0.0, xs_pair).astype(BF16)], axis=0)
            xc0 = jnp.broadcast_to(x_c[:, h0:h0 + 1], (L, L))
            xc1 = jnp.broadcast_to(x_c[:, h1:h1 + 1], (L, L))
            m0 = cb * jnp.exp(jnp.where(keep, xc0 - x_t[h0:h0 + 1, :], -jnp.inf)) * dt_t[h0:h0 + 1, :]
            m1 = cb * jnp.exp(jnp.where(keep, xc1 - x_t[h1:h1 + 1, :], -jnp.inf)) * dt_t[h1:h1 + 1, :]
            y_diag = jnp.dot(jnp.concatenate([m0.astype(BF16), m1.astype(BF16)], axis=1), rhs,
                             preferred_element_type=F32)
            e_out = jnp.where(lo, jnp.exp(xc0), jnp.exp(xc1))
            y_ref[0, 0, :, c0:c0 + LANES] = y_diag + y_off[:, lc:lc + LANES] * e_out
            lhs_b = jnp.concatenate([(b_gt * w_t[h0:h0 + 1, :]).astype(BF16),
                                     (b_gt * w_t[h1:h1 + 1, :]).astype(BF16)], axis=1)
            dec = jnp.where(lo_row, e_tot[:, h0:h0 + 1], e_tot[:, h1:h1 + 1])
            state_ref[g, :, lc:lc + LANES] = (state_ref[g, :, lc:lc + LANES] * dec
                                              + jnp.dot(lhs_b, rhs, preferred_element_type=F32))


def ssd_scan(xbc, dt_raw, dt_bias, a_log, layer, *, n_heads):
    b_sz, t_len, _ = xbc.shape
    d_ssd = n_heads * SSD_HEADDIM
    gn = SSD_GROUPS * SSD_STATE
    L = SSD_CHUNK
    assert L == LANES and 2 * SSD_HEADDIM == LANES and (n_heads // SSD_GROUPS) % 2 == 0
    nc = t_len // L
    xb = d_ssd // gn

    def cidx(d, c):
        return c + d * (nc - 1 - 2 * c)

    return pl.pallas_call(
        functools.partial(_ssd_body, chunk=L, n_heads=n_heads, head_dim=SSD_HEADDIM,
                          n_groups=SSD_GROUPS),
        grid=(2, b_sz, nc),
        in_specs=[pl.BlockSpec((1, L, d_ssd), lambda d, b, c: (b, cidx(d, c), 0)),
                  pl.BlockSpec((1, L, gn), lambda d, b, c: (b, cidx(d, c), xb)),
                  pl.BlockSpec((1, L, gn), lambda d, b, c: (b, cidx(d, c), xb + 1)),
                  pl.BlockSpec((1, L, LANES), lambda d, b, c: (b, cidx(d, c), d)),
                  pl.BlockSpec((1, n_heads, L), lambda d, b, c: (layer * 2 + d, 0, 0)),
                  pl.BlockSpec((1, n_heads, L), lambda d, b, c: (layer * 2 + d, 0, 0))],
        out_specs=pl.BlockSpec((1, 1, L, d_ssd), lambda d, b, c: (d, b, cidx(d, c), 0)),
        out_shape=jax.ShapeDtypeStruct((2, b_sz, t_len, d_ssd), F32),
        scratch_shapes=[pltpu.VMEM((SSD_GROUPS, SSD_STATE, d_ssd // SSD_GROUPS), F32)],
        compiler_params=_params(3),
        name="ssd_scan",
    )(xbc, xbc, xbc, dt_raw, dt_bias, a_log)


def _ssd_gate_body(y_ref, xs_ref, z_ref, dskip_ref, g_ref, o_ref, *, n_groups):
    y = y_ref[0] + y_ref[1] + dskip_ref[...] * xs_ref[...]
    z = z_ref[...]
    y = y * (z * jax.nn.sigmoid(z))
    gw = y.shape[1] // n_groups
    for g in range(n_groups):
        yg = y[:, g * gw:(g + 1) * gw]
        ms = jnp.mean(yg * yg, axis=-1, keepdims=True)
        o_ref[:, g * gw:(g + 1) * gw] = (
            yg * lax.rsqrt(ms + EPS) * g_ref[:, g * gw:(g + 1) * gw]).astype(o_ref.dtype)


def ssd_gate(y2, xbc2, proj2, d_skip_cols, gate_g, d_ssd, tm=256):
    m = xbc2.shape[0]
    return pl.pallas_call(
        functools.partial(_ssd_gate_body, n_groups=SSD_GROUPS),
        grid=(m // tm,),
        in_specs=[pl.BlockSpec((2, tm, d_ssd), lambda i: (0, i, 0)),
                  pl.BlockSpec((tm, d_ssd), lambda i: (i, 0)),
                  pl.BlockSpec((tm, d_ssd), lambda i: (i, 0)),
                  pl.BlockSpec((1, d_ssd), lambda i: (0, 0)),
                  pl.BlockSpec((1, d_ssd), lambda i: (0, 0))],
        out_specs=pl.BlockSpec((tm, d_ssd), lambda i: (i, 0)),
        out_shape=jax.ShapeDtypeStruct((m, d_ssd), BF16),
        compiler_params=_params(1),
        name="ssd_gate_norm",
    )(y2, xbc2, proj2, d_skip_cols, gate_g)


def _softmax_rows(s):
    m = jnp.max(s, axis=-1, keepdims=True)
    e = jnp.exp(s - m)
    return e / jnp.sum(e, axis=-1, keepdims=True)


def _mem_attn_body(q_ref, kv_ref, o_ref, *, n_heads, dh):
    scale = dh ** -0.5
    for h in range(n_heads):
        q = q_ref[0, :, h * dh:(h + 1) * dh]
        k = kv_ref[0, :, h * dh:(h + 1) * dh]
        v = kv_ref[0, :, (n_heads + h) * dh:(n_heads + h + 1) * dh]
        p = _softmax_rows(_nt_dot(q, k) * scale).astype(BF16)
        o_ref[0, :, h * dh:(h + 1) * dh] = jnp.dot(p, v, preferred_element_type=F32).astype(o_ref.dtype)


def mem_attention(q3, q_col_block, kv3, tm=512):
    b_sz, t_len, _ = q3.shape
    d_xa = XA_HEADS * XA_DH
    mem_len = kv3.shape[1]
    return pl.pallas_call(
        functools.partial(_mem_attn_body, n_heads=XA_HEADS, dh=XA_DH),
        grid=(b_sz, t_len // tm),
        in_specs=[pl.BlockSpec((1, tm, d_xa), lambda b, i: (b, i, q_col_block)),
                  pl.BlockSpec((1, mem_len, 2 * d_xa), lambda b, i: (b, 0, 0))],
        out_specs=pl.BlockSpec((1, tm, d_xa), lambda b, i: (b, i, 0)),
        out_shape=jax.ShapeDtypeStruct((b_sz, t_len, d_xa), BF16),
        compiler_params=_params(2),
        name="mem_attention",
    )(q3, kv3)


NA_QROWS = 4
NA_KROWS = 12
NEG_MASK = -1e30


def _na_bias_body(rpb_ref, o_ref, *, n_dr, width, kc):
    c_idx = lax.broadcasted_iota(I32, (width, LANES), 0)
    lane = lax.broadcasted_iota(I32, (width, LANES), 1)
    first = lane < width
    k_idx = jnp.where(first, lane, lane - width)
    w_start = jnp.clip(c_idx - kc // 2, 0, width - kc)
    valid = (k_idx >= w_start) & (k_idx < w_start + kc)
    base = LANES - (kc - 1)
    neg = jnp.full((width, LANES), NEG_MASK, F32)
    lo_half, hi_half = [], []
    for d in range(n_dr):
        x = jnp.broadcast_to(rpb_ref[0, d:d + 1, :], (width, LANES))
        lo_half.append(jnp.where(valid & first, pltpu.roll(x, base, 1, stride=1, stride_axis=0), neg))
        hi_half.append(jnp.where(valid & jnp.logical_not(first),
                                 pltpu.roll(x, (base + width) % LANES, 1, stride=1, stride_axis=0), neg))
    for d in range(n_dr - 1):
        o_ref[0, d] = jnp.where(first, lo_half[d], hi_half[d + 1])
    for d in range(n_dr):
        o_ref[0, n_dr - 1 + d] = lo_half[d]
        o_ref[0, 2 * n_dr - 1 + d] = hi_half[d]
    o_ref[0, 3 * n_dr - 1] = neg


def na_bias_table(rpb_padded):
    n_heads, n_dr, _ = rpb_padded.shape
    n_ent = 3 * n_dr
    return pl.pallas_call(
        functools.partial(_na_bias_body, n_dr=n_dr, width=GRID_W, kc=NA_KC),
        grid=(n_heads,),
        in_specs=[pl.BlockSpec((1, n_dr, LANES), lambda h: (h, 0, 0))],
        out_specs=pl.BlockSpec((1, n_ent, GRID_W, 2 * GRID_W), lambda h: (h, 0, 0, 0)),
        out_shape=jax.ShapeDtypeStruct((n_heads, n_ent, GRID_W, 2 * GRID_W), F32),
        compiler_params=_params(1),
        name="na_bias_table",
    )(rpb_padded)


def _na_table_entry(r, ka, rows, kr, n_dr):
    rs = min(max(r - kr // 2, 0), rows - kr)
    in_a = rs <= ka < rs + kr
    in_b = rs <= ka + 1 < rs + kr
    d_a = ka - r + (NA_KR - 1)
    if in_a and in_b:
        return d_a
    if in_a:
        return n_dr - 1 + d_a
    if in_b:
        return 2 * n_dr - 1 + d_a + 1
    return 3 * n_dr - 1


def _na_body(q_ref, k_ref, v_ref, tb_ref, o_ref, *, rows, width, kr, dh):
    scale = dh ** -0.5
    n_dr = 2 * NA_KR - 1
    nq = NA_QROWS * width
    nk = NA_KROWS * width
    for blk in range(rows // NA_QROWS):
        r0 = blk * NA_QROWS
        k_row0 = min(max(r0 - kr // 2, 0), rows - NA_KROWS)
        q = q_ref[0, r0 * width:r0 * width + nq, :]
        kw = k_ref[0, k_row0 * width:k_row0 * width + nk, :]
        vw = v_ref[0, k_row0 * width:k_row0 * width + nk, :]
        s = _nt_dot(q, kw) * scale
        s = jnp.concatenate([
            jnp.concatenate([
                s[i * width:(i + 1) * width, j * LANES:(j + 1) * LANES]
                + tb_ref[0, _na_table_entry(r0 + i, k_row0 + 2 * j, rows, kr, n_dr)]
                for j in range(nk // LANES)], axis=1)
            for i in range(NA_QROWS)], axis=0)
        m = jnp.max(s, axis=-1, keepdims=True)
        e = jnp.exp(s - m)
        den = jnp.sum(e, axis=-1, keepdims=True)
        o = jnp.dot(e.astype(BF16), vw, preferred_element_type=F32) / den
        o_ref[0, r0 * width:r0 * width + nq, :] = o.astype(o_ref.dtype)


def na_attention(proj3, table, n_heads):
    b_sz, t_len, _ = proj3.shape
    rows = t_len // GRID_W
    kr = min(NA_KR, rows)
    assert kr == NA_KR and rows % NA_QROWS == 0 and rows >= NA_KROWS and (rows - NA_KROWS) % 2 == 0
    assert NA_KROWS >= kr + NA_QROWS - 1 and 2 * GRID_W == LANES
    n_dr2 = table.shape[1]
    return pl.pallas_call(
        functools.partial(_na_body, rows=rows, width=GRID_W, kr=kr, dh=NA_DH),
        grid=(b_sz, n_heads),
        in_specs=[pl.BlockSpec((1, t_len, NA_DH), lambda b, h: (b, 0, h)),
                  pl.BlockSpec((1, t_len, NA_DH), lambda b, h: (b, 0, n_heads + h)),
                  pl.BlockSpec((1, t_len, NA_DH), lambda b, h: (b, 0, 2 * n_heads + h)),
                  pl.BlockSpec((1, n_dr2, GRID_W, 2 * GRID_W), lambda b, h: (h, 0, 0, 0))],
        out_specs=pl.BlockSpec((1, t_len, NA_DH), lambda b, h: (b, 0, h)),
        out_shape=jax.ShapeDtypeStruct((b_sz, t_len, n_heads * NA_DH), BF16),
        compiler_params=_params(2),
        name="na_attention",
    )(proj3, proj3, proj3, table)


def _norm_and_route(x, g_ref, w_ref, h_ref, aff_ref):
    ms = jnp.mean(x * x, axis=-1, keepdims=True)
    hn = x * lax.rsqrt(ms + EPS) * g_ref[...]
    hn_hi = hn.astype(BF16)
    h_ref[...] = hn_hi
    hn_lo = (hn - hn_hi.astype(F32)).astype(BF16)
    w = w_ref[0]
    w_hi = w.astype(BF16)
    w_lo = (w - w_hi.astype(F32)).astype(BF16)
    logits = (jnp.dot(hn_hi, w_hi, preferred_element_type=F32)
              + (jnp.dot(hn_hi, w_lo, preferred_element_type=F32)
                 + jnp.dot(hn_lo, w_hi, preferred_element_type=F32)))
    aff_ref[...] = _softmax_rows(logits)


def _out_proj_router_body(a1_ref, a2_ref, w1_ref, w2_ref, res_ref, g_ref, wr_ref,
                          x_ref, h_ref, aff_ref, wbf1_ref, wbf2_ref):
    @pl.when(pl.program_id(0) == 0)
    def _():
        wbf1_ref[...] = w1_ref[0].astype(BF16)
        wbf2_ref[...] = w2_ref[0].astype(BF16)

    acc = (jnp.dot(a1_ref[...], wbf1_ref[...], preferred_element_type=F32)
           + jnp.dot(a2_ref[...], wbf2_ref[...], preferred_element_type=F32))
    x = acc + res_ref[...]
    x_ref[...] = x
    _norm_and_route(x, g_ref, wr_ref, h_ref, aff_ref)


def out_proj_router(a1, a2, w_out, layer_w, res, norm_g, w_router, layer, tm=256):
    m, k1 = a1.shape
    k2 = a2.shape[1]
    d = w_out.shape[-1]
    n_exp = w_router.shape[-1]
    assert k1 % k2 == 0
    once = pl.Buffered(1)
    return pl.pallas_call(
        _out_proj_router_body,
        grid=(m // tm,),
        in_specs=[pl.BlockSpec((tm, k1), lambda i: (i, 0)),
                  pl.BlockSpec((tm, k2), lambda i: (i, 0)),
                  pl.BlockSpec((1, k1, d), lambda i: (layer_w, 0, 0), pipeline_mode=once),
                  pl.BlockSpec((1, k2, d), lambda i: (layer_w, k1 // k2, 0), pipeline_mode=once),
                  pl.BlockSpec((tm, d), lambda i: (i, 0)),
                  pl.BlockSpec((1, d), lambda i: (0, 0)),
                  pl.BlockSpec((1, d, n_exp), lambda i: (layer, 0, 0))],
        out_specs=[pl.BlockSpec((tm, d), lambda i: (i, 0)),
                   pl.BlockSpec((tm, d), lambda i: (i, 0)),
                   pl.BlockSpec((tm, n_exp), lambda i: (i, 0))],
        out_shape=[jax.ShapeDtypeStruct((m, d), F32),
                   jax.ShapeDtypeStruct((m, d), BF16),
                   jax.ShapeDtypeStruct((m, n_exp), F32)],
        scratch_shapes=[pltpu.VMEM((k1, d), BF16), pltpu.VMEM((k2, d), BF16)],
        compiler_params=_params(1),
        name="out_proj_router",
    )(a1, a2, w_out, w_out, res, norm_g.reshape(1, d), w_router)


TOKEN_BLK = 128
SLOT_WIN = 64
SLOT_ALIGN_LOG2 = 4


def _slot_window(c0, c1, cap):
    st = jnp.minimum(lax.shift_left(lax.shift_right_logical(c0, SLOT_ALIGN_LOG2), SLOT_ALIGN_LOG2),
                     cap - SLOT_WIN)
    return st, c1 <= st + SLOT_WIN


def _select_body(aff_ref, enc_ref, encrow_ref, affrow_ref, cnt_ref, *, cap, row_blk):
    a = aff_ref[0]
    t_len, n_exp = a.shape
    zero = jnp.zeros((1, n_exp), I32)

    def count(pred):
        return jnp.sum(pred.astype(I32), axis=0, keepdims=True)

    def value_bit(i, prefix):
        cand = prefix | jnp.left_shift(jnp.int32(1), 30 - i)
        return jnp.where(count(a >= lax.bitcast_convert_type(cand, F32)) >= cap, cand, prefix)

    thresh = lax.bitcast_convert_type(lax.fori_loop(0, 31, value_bit, zero), F32)
    above = a > thresh
    tied = a == thresh
    need = cap - count(above)
    idx = lax.broadcasted_iota(I32, (t_len, n_exp), 0)
    n_idx_bits = (t_len - 1).bit_length()

    def index_bit(i, last):
        cand = last | jnp.left_shift(jnp.int32(1), n_idx_bits - 1 - i)
        return jnp.where(count(tied & (idx < cand)) < need, cand, last)

    last = lax.fori_loop(0, n_idx_bits, index_bit, zero)
    mask = above | (tied & (idx <= last))
    mask_bf = mask.astype(BF16)

    col = lax.broadcasted_iota(I32, (row_blk, t_len), 1)
    for i in range(t_len // row_blk):
        rowi = lax.broadcasted_iota(I32, (row_blk, t_len), 0) + i * row_blk
        before = jnp.dot((rowi > col).astype(BF16), mask_bf, preferred_element_type=F32)
        enc_ref[0, i * row_blk:(i + 1) * row_blk, :] = jnp.where(
            mask[i * row_blk:(i + 1) * row_blk], before + 1.0, 0.0).astype(I32)

    eye = (lax.broadcasted_iota(I32, (n_exp, n_exp), 0)
           == lax.broadcasted_iota(I32, (n_exp, n_exp), 1)).astype(BF16)
    enc_bf = enc_ref[0].astype(F32).astype(BF16)
    encrow_ref[0] = _nt_dot(eye, enc_bf).astype(I32)

    a_hi = a.astype(BF16)
    rest = a - a_hi.astype(F32)
    a_mid = rest.astype(BF16)
    a_lo = (rest - a_mid.astype(F32)).astype(BF16)
    affrow_ref[0] = _nt_dot(eye, a_hi) + (_nt_dot(eye, a_mid) + _nt_dot(eye, a_lo))

    n_rows = cnt_ref.shape[1]
    bound = lax.broadcasted_iota(I32, (n_rows, t_len), 0) * TOKEN_BLK
    tok = lax.broadcasted_iota(I32, (n_rows, t_len), 1)
    cnt_ref[0] = jnp.dot((tok < bound).astype(BF16), mask_bf, preferred_element_type=F32).astype(I32)


def moe_select(aff3, cap):
    b_sz, t_len, n_exp = aff3.shape
    assert cap <= 256, "slot codes must stay exactly representable in bf16"
    n_rows = -(-(t_len // TOKEN_BLK + 1) // 8) * 8
    return pl.pallas_call(
        functools.partial(_select_body, cap=cap, row_blk=256),
        grid=(b_sz,),
        in_specs=[pl.BlockSpec((1, t_len, n_exp), lambda b: (b, 0, 0))],
        out_specs=[pl.BlockSpec((1, t_len, n_exp), lambda b: (b, 0, 0)),
                   pl.BlockSpec((1, n_exp, t_len), lambda b: (b, 0, 0)),
                   pl.BlockSpec((1, n_exp, t_len), lambda b: (b, 0, 0)),
                   pl.BlockSpec((1, n_rows, n_exp), lambda b: (b, 0, 0))],
        out_shape=[jax.ShapeDtypeStruct((b_sz, t_len, n_exp), I32),
                   jax.ShapeDtypeStruct((b_sz, n_exp, t_len), I32),
                   jax.ShapeDtypeStruct((b_sz, n_exp, t_len), F32),
                   jax.ShapeDtypeStruct((b_sz, n_rows, n_exp), I32)],
        compiler_params=_params(1),
        name="moe_select",
    )(aff3)


def _gather_body(cnt_ref, h_ref, encrow_ref, affrow_ref, xs_ref, g_ref, acc_ref, gacc_ref, *, cap, kb):
    b = pl.program_id(0)
    e = pl.program_id(1)
    t_len = h_ref.shape[1]
    per = kb // TOKEN_BLK
    code = encrow_ref[0, pl.ds(e, 1), :]
    gate = affrow_ref[0, pl.ds(e, 1), :]
    starts, fits = [], None
    for j in range(t_len // kb):
        st, ok = _slot_window(cnt_ref[b, j * per, e], cnt_ref[b, (j + 1) * per, e], cap)
        starts.append(st)
        fits = ok if fits is None else jnp.logical_and(fits, ok)

    @pl.when(fits)
    def _():
        acc_ref[...] = jnp.zeros_like(acc_ref)
        gacc_ref[...] = jnp.zeros_like(gacc_ref)
        sub = lax.broadcasted_iota(I32, (SLOT_WIN, kb), 0) + 1
        for j, st in enumerate(starts):
            st = pl.multiple_of(st, 1 << SLOT_ALIGN_LOG2)
            hit = (sub + st) == code[:, j * kb:(j + 1) * kb]
            acc_ref[pl.ds(st, SLOT_WIN), :] += jnp.dot(
                hit.astype(BF16), h_ref[0, j * kb:(j + 1) * kb, :], preferred_element_type=F32)
            g = jnp.sum(jnp.where(hit, gate[:, j * kb:(j + 1) * kb], 0.0), axis=1, keepdims=True)
            gacc_ref[pl.ds(st, SLOT_WIN), :] += jnp.broadcast_to(g, (SLOT_WIN, LANES))
        xs_ref[0] = acc_ref[...].astype(BF16)
        g_ref[0] = gacc_ref[...]

    @pl.when(jnp.logical_not(fits))
    def _():
        hit = (lax.broadcasted_iota(I32, (cap, t_len), 0) + 1) == code
        xs_ref[0] = jnp.dot(hit.astype(BF16), h_ref[0], preferred_element_type=F32).astype(BF16)
        g = jnp.sum(jnp.where(hit, gate, 0.0), axis=1, keepdims=True)
        g_ref[0] = jnp.broadcast_to(g, (cap, LANES))


def moe_gather(h3, encrow, affrow, cnt, cap, kb=256):
    b_sz, t_len, d = h3.shape
    n_exp = encrow.shape[1]
    assert cap >= SLOT_WIN and t_len % kb == 0 and kb % TOKEN_BLK == 0
    return pl.pallas_call(
        functools.partial(_gather_body, cap=cap, kb=kb),
        grid_spec=pltpu.PrefetchScalarGridSpec(
            num_scalar_prefetch=1,
            grid=(b_sz, n_exp),
            in_specs=[pl.BlockSpec((1, t_len, d), lambda b, e, cnt: (b, 0, 0)),
                      pl.BlockSpec((1, n_exp, t_len), lambda b, e, cnt: (b, 0, 0)),
                      pl.BlockSpec((1, n_exp, t_len), lambda b, e, cnt: (b, 0, 0))],
            out_specs=[pl.BlockSpec((1, cap, d), lambda b, e, cnt: (e, b, 0)),
                       pl.BlockSpec((1, cap, LANES), lambda b, e, cnt: (e, b, 0))],
            scratch_shapes=[pltpu.VMEM((cap, d), F32), pltpu.VMEM((cap, LANES), F32)]),
        out_shape=[jax.ShapeDtypeStruct((n_exp, b_sz * cap, d), BF16),
                   jax.ShapeDtypeStruct((n_exp, b_sz * cap, LANES), F32)],
        compiler_params=_params(2),
        name="moe_gather",
    )(cnt, h3, encrow, affrow)


def _ffn_up_body(xs_ref, w1_ref, w3_ref, hid_ref):
    xs = xs_ref[0]
    a = jnp.dot(xs, w1_ref[0, 0].astype(BF16), preferred_element_type=F32)
    b = jnp.dot(xs, w3_ref[0, 0].astype(BF16), preferred_element_type=F32)
    hid_ref[0] = (a * jax.nn.sigmoid(a) * b).astype(hid_ref.dtype)


def _ffn_down_body(hid_ref, w2_ref, g_ref, y_ref):
    y = jnp.dot(hid_ref[0], w2_ref[0, 0].astype(BF16), preferred_element_type=F32)
    gate = jnp.concatenate([g_ref[0]] * (y.shape[1] // LANES), axis=1)
    y_ref[0] = (y * gate).astype(y_ref.dtype)


def moe_ffn(xs, gate, w1, w3, w2, layer, tf=512, tn=2048):
    n_exp, rows, d = xs.shape
    d_exp = w1.shape[-1]
    hid = pl.pallas_call(
        _ffn_up_body,
        grid=(n_exp, d_exp // tf),
        in_specs=[pl.BlockSpec((1, rows, d), lambda e, f: (e, 0, 0)),
                  pl.BlockSpec((1, 1, d, tf), lambda e, f: (layer, e, 0, f)),
                  pl.BlockSpec((1, 1, d, tf), lambda e, f: (layer, e, 0, f))],
        out_specs=pl.BlockSpec((1, rows, tf), lambda e, f: (e, 0, f)),
        out_shape=jax.ShapeDtypeStruct((n_exp, rows, d_exp), BF16),
        compiler_params=_params(2),
        name="moe_ffn_up",
    )(xs, w1, w3)
    return pl.pallas_call(
        _ffn_down_body,
        grid=(n_exp, d // tn),
        in_specs=[pl.BlockSpec((1, rows, d_exp), lambda e, n: (e, 0, 0)),
                  pl.BlockSpec((1, 1, d_exp, tn), lambda e, n: (layer, e, 0, n)),
                  pl.BlockSpec((1, rows, LANES), lambda e, n: (e, 0, 0))],
        out_specs=pl.BlockSpec((1, rows, tn), lambda e, n: (e, 0, n)),
        out_shape=jax.ShapeDtypeStruct((n_exp, rows, d), BF16),
        compiler_params=_params(2),
        name="moe_ffn_down",
    )(hid, w2, gate)


SCATTER_GROUP = 4


def _scatter_body(cnt_ref, x_ref, y_ref, enc_ref, gn_ref, o_ref, hn_ref, *, cap):
    b = pl.program_id(0)
    j = pl.program_id(1)
    enc = enc_ref[0]
    n_tok, n_exp = enc.shape
    starts, fits = [], None
    for e in range(n_exp):
        st, ok = _slot_window(cnt_ref[b, j, e], cnt_ref[b, j + 1, e], cap)
        starts.append(pl.multiple_of(st, 1 << SLOT_ALIGN_LOG2))
        fits = ok if fits is None else jnp.logical_and(fits, ok)
    o_ref[0] = x_ref[0]

    @pl.when(fits)
    def _():
        width = SCATTER_GROUP * SLOT_WIN
        lane = lax.broadcasted_iota(I32, (n_tok, width), 1)
        lane_row = lane[0:1, :]
        within = jnp.bitwise_and(lane_row, SLOT_WIN - 1) + 1
        for q in range(n_exp // SCATTER_GROUP):
            es = range(q * SCATTER_GROUP, (q + 1) * SCATTER_GROUP)
            code = enc[:, es[-1]:es[-1] + 1]
            first = starts[es[-1]]
            for k in reversed(range(SCATTER_GROUP - 1)):
                code = jnp.where(lane < (k + 1) * SLOT_WIN, enc[:, es[k]:es[k] + 1], code)
                first = jnp.where(lane_row < (k + 1) * SLOT_WIN, starts[es[k]], first)
            onehot = (code == first + within).astype(BF16)
            rows = jnp.concatenate([y_ref[e, pl.ds(starts[e], SLOT_WIN), :] for e in es], axis=0)
            o_ref[0] += jnp.dot(onehot, rows, preferred_element_type=F32)

    @pl.when(jnp.logical_not(fits))
    def _():
        slot = lax.broadcasted_iota(I32, (n_tok, cap), 1) + 1
        for e in range(n_exp):
            onehot = (enc[:, e:e + 1] == slot).astype(BF16)
            o_ref[0] += jnp.dot(onehot, y_ref[e], preferred_element_type=F32)

    x_new = o_ref[0]
    ms = jnp.mean(x_new * x_new, axis=-1, keepdims=True)
    hn_ref[0] = (x_new * lax.rsqrt(ms + EPS) * gn_ref[...]).astype(hn_ref.dtype)


def moe_scatter_add(x3, y, enc, cnt, norm_g, norm_dtype, cap):
    b_sz, t_len, d = x3.shape
    n_exp = enc.shape[-1]
    assert cap >= SLOT_WIN and n_exp % SCATTER_GROUP == 0 and SLOT_WIN & (SLOT_WIN - 1) == 0
    return pl.pallas_call(
        functools.partial(_scatter_body, cap=cap),
        grid_spec=pltpu.PrefetchScalarGridSpec(
            num_scalar_prefetch=1,
            grid=(b_sz, t_len // TOKEN_BLK),
            in_specs=[pl.BlockSpec((1, TOKEN_BLK, d), lambda b, j, cnt: (b, j, 0)),
                      pl.BlockSpec((n_exp, cap, d), lambda b, j, cnt: (0, b, 0)),
                      pl.BlockSpec((1, TOKEN_BLK, n_exp), lambda b, j, cnt: (b, j, 0)),
                      pl.BlockSpec((1, d), lambda b, j, cnt: (0, 0))],
            out_specs=[pl.BlockSpec((1, TOKEN_BLK, d), lambda b, j, cnt: (b, j, 0)),
                       pl.BlockSpec((1, TOKEN_BLK, d), lambda b, j, cnt: (b, j, 0))]),
        out_shape=[jax.ShapeDtypeStruct((b_sz, t_len, d), F32),
                   jax.ShapeDtypeStruct((b_sz, t_len, d), norm_dtype)],
        compiler_params=_params(2),
        name="moe_scatter_add",
    )(cnt, x3, y, enc, norm_g.reshape(1, d))


def ec_moe_layer(x3, h2, aff2, w1, w3, w2, layer, next_norm_g, next_norm_dtype):
    b_sz, t_len, d = x3.shape
    n_exp = aff2.shape[-1]
    cap = EC_FACTOR * t_len // n_exp
    enc, encrow, affrow, cnt = moe_select(aff2.reshape(b_sz, t_len, n_exp), cap)
    xs, gate = moe_gather(h2.reshape(b_sz, t_len, d), encrow, affrow, cnt, cap)
    y = moe_ffn(xs, gate, w1, w3, w2, layer)
    return moe_scatter_add(x3, y, enc, cnt, next_norm_g, next_norm_dtype, cap)


def _layer_cols(w, layer, col0, n_cols):
    return lax.slice(w, (layer, 0, col0), (layer + 1, w.shape[1], col0 + n_cols))[0]


def ssd_mixer_layer(h2, kv3, b_sz, t_len, j, ssd_w_in, conv_w, conv_b, dt_bias, a_log, d_skip, gate_g):
    m, d = h2.shape
    n_heads = d_skip.shape[-1]
    d_ssd = n_heads * SSD_HEADDIM
    gn = SSD_GROUPS * SSD_STATE
    conv_dim = d_ssd + 2 * gn
    d_xa = XA_HEADS * XA_DH
    n_main = d_ssd + conv_dim
    tn = 1024
    proj = matmul_ws([h2], [(ssd_w_in, (1, d, tn), lambda n: (j, 0, n))], n_main,
                     tn=tn, tm=512, out_dtype=F32, name="ssd_in_proj")
    w_dt = _layer_cols(ssd_w_in, j, n_main, 2 * n_heads)
    w_dt = jnp.pad(w_dt.reshape(d, 2, n_heads), ((0, 0), (0, 0), (0, LANES - n_heads))).reshape(d, 2 * LANES)
    dt_raw = matmul_ws([h2], [(w_dt, (d, 2 * LANES), lambda n: (0, 0))], 2 * LANES,
                       tn=2 * LANES, tm=512, out_dtype=F32, name="ssd_dt_proj")
    w_q = _layer_cols(ssd_w_in, j, n_main + 2 * n_heads, d_xa)
    xq = matmul_ws([h2], [(w_q, (d, d_xa), lambda n: (0, 0))], d_xa,
                   tn=d_xa, tm=512, out_dtype=BF16, name="ssd_xq_proj")

    proj3 = proj.reshape(b_sz, t_len, n_main)
    xbc = conv_silu(proj3, conv_w, conv_b, j, d_ssd, conv_dim)
    per_head = (dt_bias.shape[0] * 2, n_heads, SSD_CHUNK)
    y2 = ssd_scan(xbc, dt_raw.reshape(b_sz, t_len, 2 * LANES),
                  jnp.broadcast_to(dt_bias.reshape(-1, n_heads, 1), per_head),
                  jnp.broadcast_to(a_log.reshape(-1, n_heads, 1), per_head), j, n_heads=n_heads)
    y = ssd_gate(y2.reshape(2, m, d_ssd), xbc.reshape(m, conv_dim), proj,
                 jnp.repeat(d_skip[j], SSD_HEADDIM).reshape(1, d_ssd), gate_g[j].reshape(1, d_ssd), d_ssd)
    o_x = mem_attention(xq.reshape(b_sz, t_len, d_xa), 0, kv3).reshape(m, d_xa)
    return y, o_x


def na_mixer_layer(h2, kv3, b_sz, t_len, j, na_w_in, na_rpb):
    m, d = h2.shape
    n_heads = na_rpb.shape[1]
    d_na = n_heads * NA_DH
    d_xa = XA_HEADS * XA_DH
    n_in = 3 * d_na + d_xa
    tn = 1024
    proj = matmul_ws([h2], [(na_w_in, (1, d, tn), lambda n: (j, 0, n))], n_in,
                     tn=tn, tm=512, out_dtype=BF16, name="na_in_proj")
    proj3 = proj.reshape(b_sz, t_len, n_in)
    rpb = na_rpb[j]
    table = na_bias_table(jnp.pad(rpb, ((0, 0), (0, 0), (0, LANES - rpb.shape[-1]))))
    o_na = na_attention(proj3, table, n_heads).reshape(m, d_na)
    o_x = mem_attention(proj3, 3 * d_na // d_xa, kv3).reshape(m, d_xa)
    return o_na, o_x


def kernel(x, mem, norm_mix_g, norm_ffn_g, norm_final_g, mem_norm_g, ssd_w_in, ssd_conv_w, ssd_conv_b,
           ssd_dt_bias, ssd_a_log, ssd_d, ssd_gate_norm_g, ssd_w_out, na_w_in, na_rpb, na_w_out, xa_w_kv,
           moe_w_router, moe_w1, moe_w3, moe_w2):
    b_sz, t_len, d = x.shape
    mem_len = mem.shape[1]
    depth = norm_mix_g.shape[0]
    m = b_sz * t_len
    d_kv = xa_w_kv.shape[-1]
    mem_n = rmsnorm_rows(mem.reshape(b_sz * mem_len, d), mem_norm_g, BF16)
    x2 = x.reshape(m, d)
    h2 = rmsnorm_rows(x2, norm_mix_g[0], BF16)
    for i in range(depth):
        j = i // N_MIXERS
        kv3 = matmul_ws([mem_n], [(xa_w_kv, (1, d, d_kv), lambda n, i=i: (i, 0, 0))], d_kv,
                        tn=d_kv, tm=512, out_dtype=BF16, name="xa_kv_proj").reshape(b_sz, mem_len, d_kv)
        if i % N_MIXERS == 0:
            heads, o_x = ssd_mixer_layer(h2, kv3, b_sz, t_len, j, ssd_w_in, ssd_conv_w, ssd_conv_b,
                                         ssd_dt_bias, ssd_a_log, ssd_d, ssd_gate_norm_g)
            w_out = ssd_w_out
        else:
            heads, o_x = na_mixer_layer(h2, kv3, b_sz, t_len, j, na_w_in, na_rpb)
            w_out = na_w_out
        x2, h_ffn, aff2 = out_proj_router(heads, o_x, w_out, j, x2, norm_ffn_g[i], moe_w_router, i)
        last = i == depth - 1
        x3, normed = ec_moe_layer(x2.reshape(b_sz, t_len, d), h_ffn, aff2, moe_w1, moe_w3, moe_w2, i,
                                  norm_final_g if last else norm_mix_g[i + 1], F32 if last else BF16)
        x2, h2 = x3.reshape(m, d), normed.reshape(m, d)
    return normed
```

```python
import functools

import jax
import jax.numpy as jnp
from jax import lax
from jax.experimental import pallas as pl
from jax.experimental.pallas import tpu as pltpu

F32 = jnp.float32
BF16 = jnp.bfloat16
I32 = jnp.int32
HIGHEST = lax.Precision.HIGHEST

EPS = 1e-6
LANES = 128
VMEM_LIMIT_BYTES = 56 * 1024 * 1024

XA_HEADS = 4
XA_DH = 128
SSD_HEADDIM = 64
SSD_GROUPS = 4
SSD_STATE = 128
SSD_CONV = 5
SSD_CHUNK = 128
NA_DH = 128
NA_KR = 8
NA_KC = 16
GRID_W = 64
EC_FACTOR = 2
N_MIXERS = 2


def _params(n_grid_dims):
    return pltpu.CompilerParams(
        dimension_semantics=("arbitrary",) * n_grid_dims,
        vmem_limit_bytes=VMEM_LIMIT_BYTES)


def _nt_dot(a, b):
    return lax.dot_general(a, b, (((1,), (1,)), ((), ())), preferred_element_type=F32)


def _rmsnorm_body(x_ref, g_ref, o_ref):
    x = x_ref[...]
    ms = jnp.mean(x * x, axis=-1, keepdims=True)
    o_ref[...] = (x * lax.rsqrt(ms + EPS) * g_ref[...]).astype(o_ref.dtype)


def rmsnorm_rows(x2d, g, out_dtype, tm=256):
    m, d = x2d.shape
    return pl.pallas_call(
        _rmsnorm_body,
        grid=(m // tm,),
        in_specs=[pl.BlockSpec((tm, d), lambda i: (i, 0)),
                  pl.BlockSpec((1, d), lambda i: (0, 0))],
        out_specs=pl.BlockSpec((tm, d), lambda i: (i, 0)),
        out_shape=jax.ShapeDtypeStruct((m, d), out_dtype),
        compiler_params=_params(1),
        name="rmsnorm",
    )(x2d, g.reshape(1, d))


def _matmul_body(*refs, n_in, has_res):
    a_refs = refs[:n_in]
    w_refs = refs[n_in:2 * n_in]
    res_ref = refs[2 * n_in] if has_res else None
    o_ref = refs[2 * n_in + has_res]
    wbf_refs = refs[2 * n_in + has_res + 1:]

    @pl.when(pl.program_id(1) == 0)
    def _():
        for w_ref, wbf_ref in zip(w_refs, wbf_refs):
            lead = (0,) * (len(w_ref.shape) - 2)
            wbf_ref[...] = w_ref[lead + (slice(None), slice(None))].astype(BF16)

    acc = None
    for a_ref, wbf_ref in zip(a_refs, wbf_refs):
        t = jnp.dot(a_ref[...], wbf_ref[...], preferred_element_type=F32)
        acc = t if acc is None else acc + t
    if has_res:
        acc = acc + res_ref[...]
    o_ref[...] = acc.astype(o_ref.dtype)


def matmul_ws(a_list, w_list, n_cols, *, tn, tm, out_dtype, res=None, name="matmul"):
    m = a_list[0].shape[0]
    n_in = len(a_list)
    in_specs = [pl.BlockSpec((tm, a.shape[1]), lambda n, i: (i, 0)) for a in a_list]
    scratch = []
    for _, blk, imap in w_list:
        in_specs.append(pl.BlockSpec(blk, lambda n, i, imap=imap: imap(n)))
        scratch.append(pltpu.VMEM(blk[-2:], BF16))
    args = list(a_list) + [w for w, _, _ in w_list]
    if res is not None:
        in_specs.append(pl.BlockSpec((tm, tn), lambda n, i: (i, n)))
        args.append(res)
    return pl.pallas_call(
        functools.partial(_matmul_body, n_in=n_in, has_res=res is not None),
        grid=(n_cols // tn, m // tm),
        in_specs=in_specs,
        out_specs=pl.BlockSpec((tm, tn), lambda n, i: (i, n)),
        out_shape=jax.ShapeDtypeStruct((m, n_cols), out_dtype),
        scratch_shapes=scratch,
        compiler_params=_params(2),
        name=name,
    )(*args)


def _conv_silu_body(u_ref, w_ref, b_ref, o_ref, pad_ref, *, t_len, k_conv, rows):
    halo = 8
    ch = u_ref.shape[-1]
    pad_ref[0:halo, :] = jnp.zeros((halo, ch), F32)
    pad_ref[t_len + halo:t_len + 2 * halo, :] = jnp.zeros((halo, ch), F32)
    pad_ref[halo:t_len + halo, :] = u_ref[0]
    w = w_ref[0]
    b = b_ref[0]
    first = halo - k_conv // 2

    def body(i, carry):
        r0 = pl.multiple_of(i * rows, rows)
        win = pad_ref[pl.ds(r0, rows + 2 * halo), :]
        acc = b + w[0:1, :] * win[first:first + rows]
        for k in range(1, k_conv):
            acc = acc + w[k:k + 1, :] * win[first + k:first + k + rows]
        o_ref[0, pl.ds(r0, rows), :] = acc * jax.nn.sigmoid(acc)
        return carry

    lax.fori_loop(0, t_len // rows, body, 0)


def conv_silu(proj3, conv_w, conv_b, layer, col0, n_ch, tc=512, rows=128):
    b_sz, t_len, _ = proj3.shape
    k_conv = conv_w.shape[1]
    c0 = col0 // tc
    return pl.pallas_call(
        functools.partial(_conv_silu_body, t_len=t_len, k_conv=k_conv, rows=rows),
        grid=(b_sz, n_ch // tc),
        in_specs=[pl.BlockSpec((1, t_len, tc), lambda b, j: (b, 0, c0 + j)),
                  pl.BlockSpec((1, k_conv, tc), lambda b, j: (layer, 0, j)),
                  pl.BlockSpec((1, 1, tc), lambda b, j: (layer, 0, j))],
        out_specs=pl.BlockSpec((1, t_len, tc), lambda b, j: (b, 0, j)),
        out_shape=jax.ShapeDtypeStruct((b_sz, t_len, n_ch), F32),
        scratch_shapes=[pltpu.VMEM((t_len + 16, tc), F32)],
        compiler_params=_params(2),
        name="ssd_conv_silu",
    )(proj3, conv_w, conv_b.reshape(conv_b.shape[0], 1, conv_b.shape[1]))


def _softplus(x):
    return jnp.maximum(x, 0.0) + jnp.log1p(jnp.exp(-jnp.abs(x)))


def _ssd_body(*refs, chunk, n_heads, head_dim, n_groups, backward):
    if backward:
        (xs_ref, b_ref, c_ref, dt_ref, bias_ref, alog_ref, yf_ref, z_ref, dskip_ref, gnorm_ref,
         y_ref, state_ref) = refs
    else:
        xs_ref, b_ref, c_ref, dt_ref, bias_ref, alog_ref, y_ref, state_ref = refs
    L = chunk
    hpg = n_heads // n_groups
    gw = hpg * head_dim

    @pl.when(pl.program_id(1) == 0)
    def _():
        state_ref[...] = jnp.zeros_like(state_ref)

    dt_t = _softplus(dt_ref[0].T[0:n_heads, :] + bias_ref[0])
    da_t = dt_t * (-jnp.exp(alog_ref[0]))
    row = lax.broadcasted_iota(I32, (L, L), 0)
    col = lax.broadcasted_iota(I32, (L, L), 1)
    keep = row <= col if backward else row >= col
    keep_t = col <= row if backward else col >= row
    x_t = jnp.dot(da_t, keep_t.astype(F32), precision=HIGHEST, preferred_element_type=F32)
    edge = 0 if backward else L - 1
    w_t = dt_t * jnp.exp(x_t[:, edge:edge + 1] - x_t)
    x_c = jnp.concatenate([x_t, jnp.zeros((LANES - n_heads, L), F32)], axis=0).T
    e_tot = jnp.exp(x_c[edge:edge + 1, :])

    lo = lax.broadcasted_iota(I32, (L, LANES), 1) < head_dim
    lo_row = lo[0:1, :]

    for g in range(n_groups):
        b_gt = b_ref[0, :, g * SSD_STATE:(g + 1) * SSD_STATE].T
        c_g = c_ref[0, :, g * SSD_STATE:(g + 1) * SSD_STATE].astype(BF16)
        cb = jnp.dot(c_g, b_gt.astype(BF16), preferred_element_type=F32)
        y_off = jnp.dot(c_g, state_ref[g].astype(BF16), preferred_element_type=F32)
        y_pairs = []
        for pp in range(hpg // 2):
            h0 = g * hpg + 2 * pp
            h1 = h0 + 1
            c0 = h0 * head_dim
            lc = 2 * pp * head_dim
            xs_pair = xs_ref[0, :, c0:c0 + LANES]
            rhs = jnp.concatenate([jnp.where(lo, xs_pair, 0.0).astype(BF16),
                                   jnp.where(lo, 0.0, xs_pair).astype(BF16)], axis=0)
            xc0 = jnp.broadcast_to(x_c[:, h0:h0 + 1], (L, L))
            xc1 = jnp.broadcast_to(x_c[:, h1:h1 + 1], (L, L))
            m0 = cb * jnp.exp(jnp.where(keep, xc0 - x_t[h0:h0 + 1, :], -jnp.inf)) * dt_t[h0:h0 + 1, :]
            m1 = cb * jnp.exp(jnp.where(keep, xc1 - x_t[h1:h1 + 1, :], -jnp.inf)) * dt_t[h1:h1 + 1, :]
            y_diag = jnp.dot(jnp.concatenate([m0.astype(BF16), m1.astype(BF16)], axis=1), rhs,
                             preferred_element_type=F32)
            e_out = jnp.where(lo, jnp.exp(xc0), jnp.exp(xc1))
            y_pair = y_diag + y_off[:, lc:lc + LANES] * e_out
            if backward:
                y_pairs.append(y_pair)
            else:
                y_ref[0, :, c0:c0 + LANES] = y_pair
            lhs_b = jnp.concatenate([(b_gt * w_t[h0:h0 + 1, :]).astype(BF16),
                                     (b_gt * w_t[h1:h1 + 1, :]).astype(BF16)], axis=1)
            dec = jnp.where(lo_row, e_tot[:, h0:h0 + 1], e_tot[:, h1:h1 + 1])
            state_ref[g, :, lc:lc + LANES] = (state_ref[g, :, lc:lc + LANES] * dec
                                              + jnp.dot(lhs_b, rhs, preferred_element_type=F32))
        if backward:
            cols = slice(g * gw, (g + 1) * gw)
            y = (yf_ref[0, :, cols] + jnp.concatenate(y_pairs, axis=1)
                 + dskip_ref[:, cols] * xs_ref[0, :, cols])
            z = z_ref[0, :, cols]
            y = y * (z * jax.nn.sigmoid(z))
            ms = jnp.mean(y * y, axis=-1, keepdims=True)
            y_ref[0, :, cols] = (y * lax.rsqrt(ms + EPS) * gnorm_ref[:, cols]).astype(y_ref.dtype)


def ssd_scan(xbc, dt_raw, dt_bias, a_log, layer, *, n_heads, dt_col0=0, epilogue=None):
    b_sz, t_len, _ = xbc.shape
    d_ssd = n_heads * SSD_HEADDIM
    gn = SSD_GROUPS * SSD_STATE
    L = SSD_CHUNK
    assert L == LANES and 2 * SSD_HEADDIM == LANES and (n_heads // SSD_GROUPS) % 2 == 0
    nc = t_len // L
    xb = d_ssd // gn
    backward = epilogue is not None
    direction = int(backward)

    def chunk_of(c):
        return nc - 1 - c if backward else c

    in_specs = [pl.BlockSpec((1, L, d_ssd), lambda b, c: (b, chunk_of(c), 0)),
                pl.BlockSpec((1, L, gn), lambda b, c: (b, chunk_of(c), xb)),
                pl.BlockSpec((1, L, gn), lambda b, c: (b, chunk_of(c), xb + 1)),
                pl.BlockSpec((1, L, LANES), lambda b, c: (b, chunk_of(c), dt_col0 + direction)),
                pl.BlockSpec((1, n_heads, L), lambda b, c: (layer * 2 + direction, 0, 0)),
                pl.BlockSpec((1, n_heads, L), lambda b, c: (layer * 2 + direction, 0, 0))]
    args = [xbc, xbc, xbc, dt_raw, dt_bias, a_log]
    if backward:
        in_specs += [pl.BlockSpec((1, L, d_ssd), lambda b, c: (b, chunk_of(c), 0)),
                     pl.BlockSpec((1, L, d_ssd), lambda b, c: (b, chunk_of(c), 0)),
                     pl.BlockSpec((1, d_ssd), lambda b, c: (0, 0)),
                     pl.BlockSpec((1, d_ssd), lambda b, c: (0, 0))]
        args += list(epilogue)
    return pl.pallas_call(
        functools.partial(_ssd_body, chunk=L, n_heads=n_heads, head_dim=SSD_HEADDIM,
                          n_groups=SSD_GROUPS, backward=backward),
        grid=(b_sz, nc),
        in_specs=in_specs,
        out_specs=pl.BlockSpec((1, L, d_ssd), lambda b, c: (b, chunk_of(c), 0)),
        out_shape=jax.ShapeDtypeStruct((b_sz, t_len, d_ssd), BF16 if backward else F32),
        scratch_shapes=[pltpu.VMEM((SSD_GROUPS, SSD_STATE, d_ssd // SSD_GROUPS), F32)],
        compiler_params=_params(2),
        name="ssd_scan_bwd_gate" if backward else "ssd_scan_fwd",
    )(*args)


def _softmax_rows(s):
    m = jnp.max(s, axis=-1, keepdims=True)
    e = jnp.exp(s - m)
    return e / jnp.sum(e, axis=-1, keepdims=True)


def _mem_attn_body(q_ref, kv_ref, o_ref, *, n_heads, dh):
    scale = dh ** -0.5
    for h in range(n_heads):
        q = q_ref[0, :, h * dh:(h + 1) * dh].astype(BF16)
        k = kv_ref[0, :, h * dh:(h + 1) * dh]
        v = kv_ref[0, :, (n_heads + h) * dh:(n_heads + h + 1) * dh]
        p = _softmax_rows(_nt_dot(q, k) * scale).astype(BF16)
        o_ref[0, :, h * dh:(h + 1) * dh] = jnp.dot(p, v, preferred_element_type=F32).astype(o_ref.dtype)


def mem_attention(q3, q_col_block, kv3, tm=512):
    b_sz, t_len, _ = q3.shape
    d_xa = XA_HEADS * XA_DH
    mem_len = kv3.shape[1]
    return pl.pallas_call(
        functools.partial(_mem_attn_body, n_heads=XA_HEADS, dh=XA_DH),
        grid=(b_sz, t_len // tm),
        in_specs=[pl.BlockSpec((1, tm, d_xa), lambda b, i: (b, i, q_col_block)),
                  pl.BlockSpec((1, mem_len, 2 * d_xa), lambda b, i: (b, 0, 0))],
        out_specs=pl.BlockSpec((1, tm, d_xa), lambda b, i: (b, i, 0)),
        out_shape=jax.ShapeDtypeStruct((b_sz, t_len, d_xa), BF16),
        compiler_params=_params(2),
        name="mem_attention",
    )(q3, kv3)


NA_QROWS = 4
NA_KROWS = 12
NEG_MASK = -1e30


def _na_bias_body(rpb_ref, o_ref, *, n_dr, width, kc):
    c_idx = lax.broadcasted_iota(I32, (width, LANES), 0)
    lane = lax.broadcasted_iota(I32, (width, LANES), 1)
    first = lane < width
    k_idx = jnp.where(first, lane, lane - width)
    w_start = jnp.clip(c_idx - kc // 2, 0, width - kc)
    valid = (k_idx >= w_start) & (k_idx < w_start + kc)
    base = LANES - (kc - 1)
    neg = jnp.full((width, LANES), NEG_MASK, F32)
    lo_half, hi_half = [], []
    for d in range(n_dr):
        x = jnp.broadcast_to(rpb_ref[0, d:d + 1, :], (width, LANES))
        lo_half.append(jnp.where(valid & first, pltpu.roll(x, base, 1, stride=1, stride_axis=0), neg))
        hi_half.append(jnp.where(valid & jnp.logical_not(first),
                                 pltpu.roll(x, (base + width) % LANES, 1, stride=1, stride_axis=0), neg))
    for d in range(n_dr - 1):
        o_ref[0, d] = jnp.where(first, lo_half[d], hi_half[d + 1])
    for d in range(n_dr):
        o_ref[0, n_dr - 1 + d] = lo_half[d]
        o_ref[0, 2 * n_dr - 1 + d] = hi_half[d]
    o_ref[0, 3 * n_dr - 1] = neg


def na_bias_table(rpb_padded):
    n_heads, n_dr, _ = rpb_padded.shape
    n_ent = 3 * n_dr
    return pl.pallas_call(
        functools.partial(_na_bias_body, n_dr=n_dr, width=GRID_W, kc=NA_KC),
        grid=(n_heads,),
        in_specs=[pl.BlockSpec((1, n_dr, LANES), lambda h: (h, 0, 0))],
        out_specs=pl.BlockSpec((1, n_ent, GRID_W, 2 * GRID_W), lambda h: (h, 0, 0, 0)),
        out_shape=jax.ShapeDtypeStruct((n_heads, n_ent, GRID_W, 2 * GRID_W), F32),
        compiler_params=_params(1),
        name="na_bias_table",
    )(rpb_padded)


def _na_table_entry(r, ka, rows, kr, n_dr):
    rs = min(max(r - kr // 2, 0), rows - kr)
    in_a = rs <= ka < rs + kr
    in_b = rs <= ka + 1 < rs + kr
    d_a = ka - r + (NA_KR - 1)
    if in_a and in_b:
        return d_a
    if in_a:
        return n_dr - 1 + d_a
    if in_b:
        return 2 * n_dr - 1 + d_a + 1
    return 3 * n_dr - 1


def _na_body(q_ref, k_ref, v_ref, tb_ref, o_ref, *, rows, width, kr, dh):
    scale = dh ** -0.5
    n_dr = 2 * NA_KR - 1
    nq = NA_QROWS * width
    nk = NA_KROWS * width
    for blk in range(rows // NA_QROWS):
        r0 = blk * NA_QROWS
        k_row0 = min(max(r0 - kr // 2, 0), rows - NA_KROWS)
        q = q_ref[0, r0 * width:r0 * width + nq, :]
        kw = k_ref[0, k_row0 * width:k_row0 * width + nk, :]
        vw = v_ref[0, k_row0 * width:k_row0 * width + nk, :]
        s = _nt_dot(q, kw) * scale
        s = jnp.concatenate([
            jnp.concatenate([
                s[i * width:(i + 1) * width, j * LANES:(j + 1) * LANES]
                + tb_ref[0, _na_table_entry(r0 + i, k_row0 + 2 * j, rows, kr, n_dr)]
                for j in range(nk // LANES)], axis=1)
            for i in range(NA_QROWS)], axis=0)
        m = jnp.max(s, axis=-1, keepdims=True)
        e = jnp.exp(s - m)
        den = jnp.sum(e, axis=-1, keepdims=True)
        o = jnp.dot(e.astype(BF16), vw, preferred_element_type=F32) / den
        o_ref[0, r0 * width:r0 * width + nq, :] = o.astype(o_ref.dtype)


def na_attention(proj3, table, n_heads):
    b_sz, t_len, _ = proj3.shape
    rows = t_len // GRID_W
    kr = min(NA_KR, rows)
    assert kr == NA_KR and rows % NA_QROWS == 0 and rows >= NA_KROWS and (rows - NA_KROWS) % 2 == 0
    assert NA_KROWS >= kr + NA_QROWS - 1 and 2 * GRID_W == LANES
    n_dr2 = table.shape[1]
    return pl.pallas_call(
        functools.partial(_na_body, rows=rows, width=GRID_W, kr=kr, dh=NA_DH),
        grid=(b_sz, n_heads),
        in_specs=[pl.BlockSpec((1, t_len, NA_DH), lambda b, h: (b, 0, h)),
                  pl.BlockSpec((1, t_len, NA_DH), lambda b, h: (b, 0, n_heads + h)),
                  pl.BlockSpec((1, t_len, NA_DH), lambda b, h: (b, 0, 2 * n_heads + h)),
                  pl.BlockSpec((1, n_dr2, GRID_W, 2 * GRID_W), lambda b, h: (h, 0, 0, 0))],
        out_specs=pl.BlockSpec((1, t_len, NA_DH), lambda b, h: (b, 0, h)),
        out_shape=jax.ShapeDtypeStruct((b_sz, t_len, n_heads * NA_DH), BF16),
        compiler_params=_params(2),
        name="na_attention",
    )(proj3, proj3, proj3, table)


def _norm_and_route(x, g_ref, w_ref, h_ref, aff_ref):
    ms = jnp.mean(x * x, axis=-1, keepdims=True)
    hn = x * lax.rsqrt(ms + EPS) * g_ref[...]
    hn_hi = hn.astype(BF16)
    h_ref[...] = hn_hi
    hn_lo = (hn - hn_hi.astype(F32)).astype(BF16)
    w = w_ref[0]
    w_hi = w.astype(BF16)
    w_lo = (w - w_hi.astype(F32)).astype(BF16)
    n_exp = w.shape[1]
    both = jnp.dot(hn_hi, jnp.concatenate([w_hi, w_lo], axis=1), preferred_element_type=F32)
    logits = both[:, :n_exp] + (both[:, n_exp:] + jnp.dot(hn_lo, w_hi, preferred_element_type=F32))
    aff_ref[...] = _softmax_rows(logits)


def _out_proj_router_body(a1_ref, a2_ref, w1_ref, w2_ref, res_ref, g_ref, wr_ref,
                          x_ref, h_ref, aff_ref, wbf1_ref, wbf2_ref):
    @pl.when(pl.program_id(0) == 0)
    def _():
        wbf1_ref[...] = w1_ref[0].astype(BF16)
        wbf2_ref[...] = w2_ref[0].astype(BF16)

    acc = (jnp.dot(a1_ref[...], wbf1_ref[...], preferred_element_type=F32)
           + jnp.dot(a2_ref[...], wbf2_ref[...], preferred_element_type=F32))
    x = acc + res_ref[...]
    x_ref[...] = x
    _norm_and_route(x, g_ref, wr_ref, h_ref, aff_ref)


def out_proj_router(a1, a2, w_out, layer_w, res, norm_g, w_router, layer, tm=256):
    m, k1 = a1.shape
    k2 = a2.shape[1]
    d = w_out.shape[-1]
    n_exp = w_router.shape[-1]
    assert k1 % k2 == 0
    once = pl.Buffered(1)
    return pl.pallas_call(
        _out_proj_router_body,
        grid=(m // tm,),
        in_specs=[pl.BlockSpec((tm, k1), lambda i: (i, 0)),
                  pl.BlockSpec((tm, k2), lambda i: (i, 0)),
                  pl.BlockSpec((1, k1, d), lambda i: (layer_w, 0, 0), pipeline_mode=once),
                  pl.BlockSpec((1, k2, d), lambda i: (layer_w, k1 // k2, 0), pipeline_mode=once),
                  pl.BlockSpec((tm, d), lambda i: (i, 0)),
                  pl.BlockSpec((1, d), lambda i: (0, 0)),
                  pl.BlockSpec((1, d, n_exp), lambda i: (layer, 0, 0))],
        out_specs=[pl.BlockSpec((tm, d), lambda i: (i, 0)),
                   pl.BlockSpec((tm, d), lambda i: (i, 0)),
                   pl.BlockSpec((tm, n_exp), lambda i: (i, 0))],
        out_shape=[jax.ShapeDtypeStruct((m, d), F32),
                   jax.ShapeDtypeStruct((m, d), BF16),
                   jax.ShapeDtypeStruct((m, n_exp), F32)],
        scratch_shapes=[pltpu.VMEM((k1, d), BF16), pltpu.VMEM((k2, d), BF16)],
        compiler_params=_params(1),
        name="out_proj_router",
    )(a1, a2, w_out, w_out, res, norm_g.reshape(1, d), w_router)


TOKEN_BLK = 128
SLOT_WIN = 64
SLOT_ALIGN_LOG2 = 4


def _slot_window(c0, c1, cap):
    st = jnp.minimum(lax.shift_left(lax.shift_right_logical(c0, SLOT_ALIGN_LOG2), SLOT_ALIGN_LOG2),
                     cap - SLOT_WIN)
    return st, c1 <= st + SLOT_WIN


def _select_body(aff_ref, enc_ref, encrow_ref, affrow_ref, cnt_ref, *, cap):
    a = aff_ref[0]
    t_len, n_exp = a.shape
    zero = jnp.zeros((1, n_exp), I32)

    def count(pred):
        return jnp.sum(pred.astype(I32), axis=0, keepdims=True)

    def value_bit(i, prefix):
        cand = prefix | jnp.left_shift(jnp.int32(1), 30 - i)
        return jnp.where(count(a >= lax.bitcast_convert_type(cand, F32)) >= cap, cand, prefix)

    thresh = lax.bitcast_convert_type(lax.fori_loop(0, 31, value_bit, zero), F32)
    above = a > thresh
    tied = a == thresh
    need = cap - count(above)
    idx = lax.broadcasted_iota(I32, (t_len, n_exp), 0)
    n_idx_bits = (t_len - 1).bit_length()

    def index_bit(i, last):
        cand = last | jnp.left_shift(jnp.int32(1), n_idx_bits - 1 - i)
        return jnp.where(count(tied & (idx < cand)) < need, cand, last)

    last = lax.fori_loop(0, n_idx_bits, index_bit, zero)
    mask = above | (tied & (idx <= last))
    mask_bf = mask.astype(BF16)

    n_rows = cnt_ref.shape[1]
    bound = lax.broadcasted_iota(I32, (n_rows, t_len), 0) * TOKEN_BLK
    tok = lax.broadcasted_iota(I32, (n_rows, t_len), 1)
    cnt = jnp.dot((tok < bound).astype(BF16), mask_bf, preferred_element_type=F32)
    cnt_ref[0] = cnt.astype(I32)
    earlier = (lax.broadcasted_iota(I32, (TOKEN_BLK, TOKEN_BLK), 0)
               > lax.broadcasted_iota(I32, (TOKEN_BLK, TOKEN_BLK), 1)).astype(BF16)
    for i in range(t_len // TOKEN_BLK):
        rows = slice(i * TOKEN_BLK, (i + 1) * TOKEN_BLK)
        before = jnp.dot(earlier, mask_bf[rows], preferred_element_type=F32) + cnt[i:i + 1, :]
        enc_ref[0, rows, :] = jnp.where(mask[rows], before + 1.0, 0.0).astype(I32)

    eye = (lax.broadcasted_iota(I32, (n_exp, n_exp), 0)
           == lax.broadcasted_iota(I32, (n_exp, n_exp), 1)).astype(BF16)
    enc_bf = enc_ref[0].astype(F32).astype(BF16)
    encrow_ref[0] = _nt_dot(eye, enc_bf).astype(I32)

    a_hi = a.astype(BF16)
    rest = a - a_hi.astype(F32)
    a_mid = rest.astype(BF16)
    a_lo = (rest - a_mid.astype(F32)).astype(BF16)
    affrow_ref[0] = _nt_dot(eye, a_hi) + (_nt_dot(eye, a_mid) + _nt_dot(eye, a_lo))


def moe_select(aff3, cap):
    b_sz, t_len, n_exp = aff3.shape
    assert cap <= 256, "slot codes must stay exactly representable in bf16"
    n_rows = -(-(t_len // TOKEN_BLK + 1) // 8) * 8
    return pl.pallas_call(
        functools.partial(_select_body, cap=cap),
        grid=(b_sz,),
        in_specs=[pl.BlockSpec((1, t_len, n_exp), lambda b: (b, 0, 0))],
        out_specs=[pl.BlockSpec((1, t_len, n_exp), lambda b: (b, 0, 0)),
                   pl.BlockSpec((1, n_exp, t_len), lambda b: (b, 0, 0)),
                   pl.BlockSpec((1, n_exp, t_len), lambda b: (b, 0, 0)),
                   pl.BlockSpec((1, n_rows, n_exp), lambda b: (b, 0, 0))],
        out_shape=[jax.ShapeDtypeStruct((b_sz, t_len, n_exp), I32),
                   jax.ShapeDtypeStruct((b_sz, n_exp, t_len), I32),
                   jax.ShapeDtypeStruct((b_sz, n_exp, t_len), F32),
                   jax.ShapeDtypeStruct((b_sz, n_rows, n_exp), I32)],
        compiler_params=_params(1),
        name="moe_select",
    )(aff3)


def _gather_body(cnt_ref, h_ref, encrow_ref, affrow_ref, xs_ref, g_ref, acc_ref, gacc_ref, *, cap, kb):
    b = pl.program_id(0)
    e = pl.program_id(1)
    t_len = h_ref.shape[1]
    per = kb // TOKEN_BLK
    code = encrow_ref[0, pl.ds(e, 1), :]
    gate = affrow_ref[0, pl.ds(e, 1), :]
    starts, fits = [], None
    for j in range(t_len // kb):
        st, ok = _slot_window(cnt_ref[b, j * per, e], cnt_ref[b, (j + 1) * per, e], cap)
        starts.append(st)
        fits = ok if fits is None else jnp.logical_and(fits, ok)

    @pl.when(fits)
    def _():
        acc_ref[...] = jnp.zeros_like(acc_ref)
        gacc_ref[...] = jnp.zeros_like(gacc_ref)
        sub = lax.broadcasted_iota(I32, (SLOT_WIN, kb), 0) + 1
        for j, st in enumerate(starts):
            st = pl.multiple_of(st, 1 << SLOT_ALIGN_LOG2)
            hit = (sub + st) == code[:, j * kb:(j + 1) * kb]
            acc_ref[pl.ds(st, SLOT_WIN), :] += jnp.dot(
                hit.astype(BF16), h_ref[0, j * kb:(j + 1) * kb, :], preferred_element_type=F32)
            g = jnp.sum(jnp.where(hit, gate[:, j * kb:(j + 1) * kb], 0.0), axis=1, keepdims=True)
            gacc_ref[pl.ds(st, SLOT_WIN), :] += jnp.broadcast_to(g, (SLOT_WIN, LANES))
        xs_ref[0] = acc_ref[...].astype(BF16)
        g_ref[0] = gacc_ref[...]

    @pl.when(jnp.logical_not(fits))
    def _():
        hit = (lax.broadcasted_iota(I32, (cap, t_len), 0) + 1) == code
        xs_ref[0] = jnp.dot(hit.astype(BF16), h_ref[0], preferred_element_type=F32).astype(BF16)
        g = jnp.sum(jnp.where(hit, gate, 0.0), axis=1, keepdims=True)
        g_ref[0] = jnp.broadcast_to(g, (cap, LANES))


def moe_gather(h3, encrow, affrow, cnt, cap, kb=256):
    b_sz, t_len, d = h3.shape
    n_exp = encrow.shape[1]
    assert cap >= SLOT_WIN and t_len % kb == 0 and kb % TOKEN_BLK == 0
    return pl.pallas_call(
        functools.partial(_gather_body, cap=cap, kb=kb),
        grid_spec=pltpu.PrefetchScalarGridSpec(
            num_scalar_prefetch=1,
            grid=(b_sz, n_exp),
            in_specs=[pl.BlockSpec((1, t_len, d), lambda b, e, cnt: (b, 0, 0)),
                      pl.BlockSpec((1, n_exp, t_len), lambda b, e, cnt: (b, 0, 0)),
                      pl.BlockSpec((1, n_exp, t_len), lambda b, e, cnt: (b, 0, 0))],
            out_specs=[pl.BlockSpec((1, cap, d), lambda b, e, cnt: (e, b, 0)),
                       pl.BlockSpec((1, cap, LANES), lambda b, e, cnt: (e, b, 0))],
            scratch_shapes=[pltpu.VMEM((cap, d), F32), pltpu.VMEM((cap, LANES), F32)]),
        out_shape=[jax.ShapeDtypeStruct((n_exp, b_sz * cap, d), BF16),
                   jax.ShapeDtypeStruct((n_exp, b_sz * cap, LANES), F32)],
        compiler_params=_params(2),
        name="moe_gather",
    )(cnt, h3, encrow, affrow)


def _ffn_up_body(xs_ref, w1_ref, w3_ref, hid_ref):
    xs = xs_ref[0]
    a = jnp.dot(xs, w1_ref[0, 0].astype(BF16), preferred_element_type=F32)
    b = jnp.dot(xs, w3_ref[0, 0].astype(BF16), preferred_element_type=F32)
    hid_ref[0] = (a * jax.nn.sigmoid(a) * b).astype(hid_ref.dtype)


def _ffn_down_body(hid_ref, w2_ref, g_ref, y_ref):
    y = jnp.dot(hid_ref[0], w2_ref[0, 0].astype(BF16), preferred_element_type=F32)
    gate = jnp.concatenate([g_ref[0]] * (y.shape[1] // LANES), axis=1)
    y_ref[0] = (y * gate).astype(y_ref.dtype)


def moe_ffn(xs, gate, w1, w3, w2, layer, tf=512, tn=2048):
    n_exp, rows, d = xs.shape
    d_exp = w1.shape[-1]
    hid = pl.pallas_call(
        _ffn_up_body,
        grid=(n_exp, d_exp // tf),
        in_specs=[pl.BlockSpec((1, rows, d), lambda e, f: (e, 0, 0)),
                  pl.BlockSpec((1, 1, d, tf), lambda e, f: (layer, e, 0, f)),
                  pl.BlockSpec((1, 1, d, tf), lambda e, f: (layer, e, 0, f))],
        out_specs=pl.BlockSpec((1, rows, tf), lambda e, f: (e, 0, f)),
        out_shape=jax.ShapeDtypeStruct((n_exp, rows, d_exp), BF16),
        compiler_params=_params(2),
        name="moe_ffn_up",
    )(xs, w1, w3)
    return pl.pallas_call(
        _ffn_down_body,
        grid=(n_exp, d // tn),
        in_specs=[pl.BlockSpec((1, rows, d_exp), lambda e, n: (e, 0, 0)),
                  pl.BlockSpec((1, 1, d_exp, tn), lambda e, n: (layer, e, 0, n)),
                  pl.BlockSpec((1, rows, LANES), lambda e, n: (e, 0, 0))],
        out_specs=pl.BlockSpec((1, rows, tn), lambda e, n: (e, 0, n)),
        out_shape=jax.ShapeDtypeStruct((n_exp, rows, d), BF16),
        compiler_params=_params(2),
        name="moe_ffn_down",
    )(hid, w2, gate)


SCATTER_GROUP = 4


def _scatter_body(cnt_ref, x_ref, y_ref, enc_ref, gn_ref, o_ref, hn_ref, *, cap):
    b = pl.program_id(0)
    j = pl.program_id(1)
    enc = enc_ref[0]
    n_tok, n_exp = enc.shape
    starts, fits = [], None
    for e in range(n_exp):
        st, ok = _slot_window(cnt_ref[b, j, e], cnt_ref[b, j + 1, e], cap)
        starts.append(pl.multiple_of(st, 1 << SLOT_ALIGN_LOG2))
        fits = ok if fits is None else jnp.logical_and(fits, ok)
    o_ref[0] = x_ref[0]

    @pl.when(fits)
    def _():
        width = SCATTER_GROUP * SLOT_WIN
        lane = lax.broadcasted_iota(I32, (n_tok, width), 1)
        lane_row = lane[0:1, :]
        within = jnp.bitwise_and(lane_row, SLOT_WIN - 1) + 1
        onehots, windows = [], []
        for q in range(n_exp // SCATTER_GROUP):
            es = range(q * SCATTER_GROUP, (q + 1) * SCATTER_GROUP)
            code = enc[:, es[-1]:es[-1] + 1]
            first = starts[es[-1]]
            for k in reversed(range(SCATTER_GROUP - 1)):
                code = jnp.where(lane < (k + 1) * SLOT_WIN, enc[:, es[k]:es[k] + 1], code)
                first = jnp.where(lane_row < (k + 1) * SLOT_WIN, starts[es[k]], first)
            onehots.append((code == first + within).astype(BF16))
            windows += [y_ref[e, pl.ds(starts[e], SLOT_WIN), :] for e in es]
        o_ref[0] += jnp.dot(jnp.concatenate(onehots, axis=1), jnp.concatenate(windows, axis=0),
                            preferred_element_type=F32)

    @pl.when(jnp.logical_not(fits))
    def _():
        slot = lax.broadcasted_iota(I32, (n_tok, cap), 1) + 1
        for e in range(n_exp):
            onehot = (enc[:, e:e + 1] == slot).astype(BF16)
            o_ref[0] += jnp.dot(onehot, y_ref[e], preferred_element_type=F32)

    x_new = o_ref[0]
    ms = jnp.mean(x_new * x_new, axis=-1, keepdims=True)
    hn_ref[0] = (x_new * lax.rsqrt(ms + EPS) * gn_ref[...]).astype(hn_ref.dtype)


def moe_scatter_add(x3, y, enc, cnt, norm_g, norm_dtype, cap):
    b_sz, t_len, d = x3.shape
    n_exp = enc.shape[-1]
    assert cap >= SLOT_WIN and n_exp % SCATTER_GROUP == 0 and SLOT_WIN & (SLOT_WIN - 1) == 0
    return pl.pallas_call(
        functools.partial(_scatter_body, cap=cap),
        grid_spec=pltpu.PrefetchScalarGridSpec(
            num_scalar_prefetch=1,
            grid=(b_sz, t_len // TOKEN_BLK),
            in_specs=[pl.BlockSpec((1, TOKEN_BLK, d), lambda b, j, cnt: (b, j, 0)),
                      pl.BlockSpec((n_exp, cap, d), lambda b, j, cnt: (0, b, 0)),
                      pl.BlockSpec((1, TOKEN_BLK, n_exp), lambda b, j, cnt: (b, j, 0)),
                      pl.BlockSpec((1, d), lambda b, j, cnt: (0, 0))],
            out_specs=[pl.BlockSpec((1, TOKEN_BLK, d), lambda b, j, cnt: (b, j, 0)),
                       pl.BlockSpec((1, TOKEN_BLK, d), lambda b, j, cnt: (b, j, 0))]),
        out_shape=[jax.ShapeDtypeStruct((b_sz, t_len, d), F32),
                   jax.ShapeDtypeStruct((b_sz, t_len, d), norm_dtype)],
        compiler_params=_params(2),
        name="moe_scatter_add",
    )(cnt, x3, y, enc, norm_g.reshape(1, d))


def ec_moe_layer(x3, h2, aff2, w1, w3, w2, layer, next_norm_g, next_norm_dtype):
    b_sz, t_len, d = x3.shape
    n_exp = aff2.shape[-1]
    cap = EC_FACTOR * t_len // n_exp
    enc, encrow, affrow, cnt = moe_select(aff2.reshape(b_sz, t_len, n_exp), cap)
    xs, gate = moe_gather(h2.reshape(b_sz, t_len, d), encrow, affrow, cnt, cap)
    y = moe_ffn(xs, gate, w1, w3, w2, layer)
    return moe_scatter_add(x3, y, enc, cnt, next_norm_g, next_norm_dtype, cap)


def _layer_cols(w, layer, col0, n_cols):
    return lax.slice(w, (layer, 0, col0), (layer + 1, w.shape[1], col0 + n_cols))[0]


def ssd_mixer_layer(h2, kv3, b_sz, t_len, j, ssd_w_in, conv_w, conv_b, dt_bias, a_log, d_skip, gate_g):
    m, d = h2.shape
    n_heads = d_skip.shape[-1]
    d_ssd = n_heads * SSD_HEADDIM
    gn = SSD_GROUPS * SSD_STATE
    conv_dim = d_ssd + 2 * gn
    d_xa = XA_HEADS * XA_DH
    n_main = d_ssd + conv_dim
    tn = 1024
    proj = matmul_ws([h2], [(ssd_w_in, (1, d, tn), lambda n: (j, 0, n))], n_main,
                     tn=tn, tm=512, out_dtype=F32, name="ssd_in_proj")
    w_dt = _layer_cols(ssd_w_in, j, n_main, 2 * n_heads)
    w_dt = jnp.pad(w_dt.reshape(d, 2, n_heads), ((0, 0), (0, 0), (0, LANES - n_heads))).reshape(d, 2 * LANES)
    w_tail = jnp.concatenate([_layer_cols(ssd_w_in, j, n_main + 2 * n_heads, d_xa), w_dt], axis=1)
    n_tail = d_xa + 2 * LANES
    tail = matmul_ws([h2], [(w_tail, (d, n_tail), lambda n: (0, 0))], n_tail,
                     tn=n_tail, tm=512, out_dtype=F32, name="ssd_tail_proj").reshape(b_sz, t_len, n_tail)

    proj3 = proj.reshape(b_sz, t_len, n_main)
    xbc = conv_silu(proj3, conv_w, conv_b, j, d_ssd, conv_dim)
    per_head = (dt_bias.shape[0] * 2, n_heads, SSD_CHUNK)
    bias_t = jnp.broadcast_to(dt_bias.reshape(-1, n_heads, 1), per_head)
    alog_t = jnp.broadcast_to(a_log.reshape(-1, n_heads, 1), per_head)
    dt_col0 = d_xa // LANES
    y_fwd = ssd_scan(xbc, tail, bias_t, alog_t, j, n_heads=n_heads, dt_col0=dt_col0)
    y = ssd_scan(xbc, tail, bias_t, alog_t, j, n_heads=n_heads, dt_col0=dt_col0,
                 epilogue=(y_fwd, proj3, jnp.repeat(d_skip[j], SSD_HEADDIM).reshape(1, d_ssd),
                           gate_g[j].reshape(1, d_ssd))).reshape(m, d_ssd)
    o_x = mem_attention(tail, 0, kv3).reshape(m, d_xa)
    return y, o_x


def na_mixer_layer(h2, kv3, b_sz, t_len, j, na_w_in, na_rpb):
    m, d = h2.shape
    n_heads = na_rpb.shape[1]
    d_na = n_heads * NA_DH
    d_xa = XA_HEADS * XA_DH
    n_in = 3 * d_na + d_xa
    tn = 1024
    proj = matmul_ws([h2], [(na_w_in, (1, d, tn), lambda n: (j, 0, n))], n_in,
                     tn=tn, tm=512, out_dtype=BF16, name="na_in_proj")
    proj3 = proj.reshape(b_sz, t_len, n_in)
    rpb = na_rpb[j]
    table = na_bias_table(jnp.pad(rpb, ((0, 0), (0, 0), (0, LANES - rpb.shape[-1]))))
    o_na = na_attention(proj3, table, n_heads).reshape(m, d_na)
    o_x = mem_attention(proj3, 3 * d_na // d_xa, kv3).reshape(m, d_xa)
    return o_na, o_x


def kernel(x, mem, norm_mix_g, norm_ffn_g, norm_final_g, mem_norm_g, ssd_w_in, ssd_conv_w, ssd_conv_b,
           ssd_dt_bias, ssd_a_log, ssd_d, ssd_gate_norm_g, ssd_w_out, na_w_in, na_rpb, na_w_out, xa_w_kv,
           moe_w_router, moe_w1, moe_w3, moe_w2):
    b_sz, t_len, d = x.shape
    mem_len = mem.shape[1]
    depth = norm_mix_g.shape[0]
    m = b_sz * t_len
    d_kv = xa_w_kv.shape[-1]
    mem_n = rmsnorm_rows(mem.reshape(b_sz * mem_len, d), mem_norm_g, BF16)
    x2 = x.reshape(m, d)
    h2 = rmsnorm_rows(x2, norm_mix_g[0], BF16)
    for i in range(depth):
        j = i // N_MIXERS
        kv3 = matmul_ws([mem_n], [(xa_w_kv, (1, d, d_kv), lambda n, i=i: (i, 0, 0))], d_kv,
                        tn=d_kv, tm=512, out_dtype=BF16, name="xa_kv_proj").reshape(b_sz, mem_len, d_kv)
        if i % N_MIXERS == 0:
            heads, o_x = ssd_mixer_layer(h2, kv3, b_sz, t_len, j, ssd_w_in, ssd_conv_w, ssd_conv_b,
                                         ssd_dt_bias, ssd_a_log, ssd_d, ssd_gate_norm_g)
            w_out = ssd_w_out
        else:
            heads, o_x = na_mixer_layer(h2, kv3, b_sz, t_len, j, na_w_in, na_rpb)
            w_out = na_w_out
        x2, h_ffn, aff2 = out_proj_router(heads, o_x, w_out, j, x2, norm_ffn_g[i], moe_w_router, i)
        last = i == depth - 1
        x3, normed = ec_moe_layer(x2.reshape(b_sz, t_len, d), h_ffn, aff2, moe_w1, moe_w3, moe_w2, i,
                                  norm_final_g if last else norm_mix_g[i + 1], F32 if last else BF16)
        x2, h2 = x3.reshape(m, d), normed.reshape(m, d)
    return normed
```

```python
import functools

import jax
import jax.numpy as jnp
from jax import lax
from jax.experimental import pallas as pl
from jax.experimental.pallas import tpu as pltpu

F32 = jnp.float32
BF16 = jnp.bfloat16
I32 = jnp.int32
HIGHEST = lax.Precision.HIGHEST

EPS = 1e-6
LANES = 128
VMEM_LIMIT_BYTES = 56 * 1024 * 1024

XA_HEADS = 4
XA_DH = 128
SSD_HEADDIM = 64
SSD_GROUPS = 4
SSD_STATE = 128
SSD_CONV = 5
SSD_CHUNK = 128
NA_DH = 128
NA_KR = 8
NA_KC = 16
GRID_W = 64
EC_FACTOR = 2
N_MIXERS = 2


def _params(n_grid_dims):
    return pltpu.CompilerParams(
        dimension_semantics=("arbitrary",) * n_grid_dims,
        vmem_limit_bytes=VMEM_LIMIT_BYTES)


def _nt_dot(a, b):
    return lax.dot_general(a, b, (((1,), (1,)), ((), ())), preferred_element_type=F32)


def _rmsnorm_body(x_ref, g_ref, o_ref):
    x = x_ref[...]
    ms = jnp.mean(x * x, axis=-1, keepdims=True)
    o_ref[...] = (x * lax.rsqrt(ms + EPS) * g_ref[...]).astype(o_ref.dtype)


def rmsnorm_rows(x2d, g, out_dtype, tm=256):
    m, d = x2d.shape
    return pl.pallas_call(
        _rmsnorm_body,
        grid=(m // tm,),
        in_specs=[pl.BlockSpec((tm, d), lambda i: (i, 0)),
                  pl.BlockSpec((1, d), lambda i: (0, 0))],
        out_specs=pl.BlockSpec((tm, d), lambda i: (i, 0)),
        out_shape=jax.ShapeDtypeStruct((m, d), out_dtype),
        compiler_params=_params(1),
        name="rmsnorm",
    )(x2d, g.reshape(1, d))


def _matmul_body(*refs, n_in, has_res):
    a_refs = refs[:n_in]
    w_refs = refs[n_in:2 * n_in]
    res_ref = refs[2 * n_in] if has_res else None
    o_ref = refs[2 * n_in + has_res]
    wbf_refs = refs[2 * n_in + has_res + 1:]

    @pl.when(pl.program_id(1) == 0)
    def _():
        for w_ref, wbf_ref in zip(w_refs, wbf_refs):
            lead = (0,) * (len(w_ref.shape) - 2)
            wbf_ref[...] = w_ref[lead + (slice(None), slice(None))].astype(BF16)

    acc = None
    for a_ref, wbf_ref in zip(a_refs, wbf_refs):
        t = jnp.dot(a_ref[...], wbf_ref[...], preferred_element_type=F32)
        acc = t if acc is None else acc + t
    if has_res:
        acc = acc + res_ref[...]
    o_ref[...] = acc.astype(o_ref.dtype)


def matmul_ws(a_list, w_list, n_cols, *, tn, tm, out_dtype, res=None, name="matmul"):
    m = a_list[0].shape[0]
    n_in = len(a_list)
    in_specs = [pl.BlockSpec((tm, a.shape[1]), lambda n, i: (i, 0)) for a in a_list]
    scratch = []
    for _, blk, imap in w_list:
        in_specs.append(pl.BlockSpec(blk, lambda n, i, imap=imap: imap(n)))
        scratch.append(pltpu.VMEM(blk[-2:], BF16))
    args = list(a_list) + [w for w, _, _ in w_list]
    if res is not None:
        in_specs.append(pl.BlockSpec((tm, tn), lambda n, i: (i, n)))
        args.append(res)
    return pl.pallas_call(
        functools.partial(_matmul_body, n_in=n_in, has_res=res is not None),
        grid=(n_cols // tn, m // tm),
        in_specs=in_specs,
        out_specs=pl.BlockSpec((tm, tn), lambda n, i: (i, n)),
        out_shape=jax.ShapeDtypeStruct((m, n_cols), out_dtype),
        scratch_shapes=scratch,
        compiler_params=_params(2),
        name=name,
    )(*args)


def _conv_silu_body(u_ref, w_ref, b_ref, o_ref, pad_ref, *, t_len, k_conv, rows):
    halo = 8
    ch = u_ref.shape[-1]
    pad_ref[0:halo, :] = jnp.zeros((halo, ch), F32)
    pad_ref[t_len + halo:t_len + 2 * halo, :] = jnp.zeros((halo, ch), F32)
    pad_ref[halo:t_len + halo, :] = u_ref[0]
    w = w_ref[0]
    b = b_ref[0]
    first = halo - k_conv // 2

    def body(i, carry):
        r0 = pl.multiple_of(i * rows, rows)
        win = pad_ref[pl.ds(r0, rows + 2 * halo), :]
        acc = b + w[0:1, :] * win[first:first + rows]
        for k in range(1, k_conv):
            acc = acc + w[k:k + 1, :] * win[first + k:first + k + rows]
        o_ref[0, pl.ds(r0, rows), :] = acc * jax.nn.sigmoid(acc)
        return carry

    lax.fori_loop(0, t_len // rows, body, 0)


def conv_silu(proj3, conv_w, conv_b, layer, col0, n_ch, tc=512, rows=128):
    b_sz, t_len, _ = proj3.shape
    k_conv = conv_w.shape[1]
    c0 = col0 // tc
    return pl.pallas_call(
        functools.partial(_conv_silu_body, t_len=t_len, k_conv=k_conv, rows=rows),
        grid=(b_sz, n_ch // tc),
        in_specs=[pl.BlockSpec((1, t_len, tc), lambda b, j: (b, 0, c0 + j)),
                  pl.BlockSpec((1, k_conv, tc), lambda b, j: (layer, 0, j)),
                  pl.BlockSpec((1, 1, tc), lambda b, j: (layer, 0, j))],
        out_specs=pl.BlockSpec((1, t_len, tc), lambda b, j: (b, 0, j)),
        out_shape=jax.ShapeDtypeStruct((b_sz, t_len, n_ch), F32),
        scratch_shapes=[pltpu.VMEM((t_len + 16, tc), F32)],
        compiler_params=_params(2),
        name="ssd_conv_silu",
    )(proj3, conv_w, conv_b.reshape(conv_b.shape[0], 1, conv_b.shape[1]))


def _softplus(x):
    return jnp.maximum(x, 0.0) + jnp.log1p(jnp.exp(-jnp.abs(x)))


def _ssd_body(*refs, chunk, n_heads, head_dim, n_groups, backward):
    if backward:
        (xs_ref, b_ref, c_ref, dt_ref, bias_ref, alog_ref, yf_ref, z_ref, dskip_ref, gnorm_ref,
         y_ref, state_ref) = refs
    else:
        xs_ref, b_ref, c_ref, dt_ref, bias_ref, alog_ref, y_ref, state_ref = refs
    L = chunk
    hpg = n_heads // n_groups
    gw = hpg * head_dim

    @pl.when(pl.program_id(1) == 0)
    def _():
        state_ref[...] = jnp.zeros_like(state_ref)

    dt_t = _softplus(dt_ref[0].T[0:n_heads, :] + bias_ref[0])
    da_t = dt_t * (-jnp.exp(alog_ref[0]))
    row = lax.broadcasted_iota(I32, (L, L), 0)
    col = lax.broadcasted_iota(I32, (L, L), 1)
    keep = row <= col if backward else row >= col
    keep_t = col <= row if backward else col >= row
    x_t = jnp.dot(da_t, keep_t.astype(F32), precision=HIGHEST, preferred_element_type=F32)
    edge = 0 if backward else L - 1
    w_t = dt_t * jnp.exp(x_t[:, edge:edge + 1] - x_t)
    x_c = jnp.concatenate([x_t, jnp.zeros((LANES - n_heads, L), F32)], axis=0).T
    e_tot = jnp.exp(x_c[edge:edge + 1, :])

    lo = lax.broadcasted_iota(I32, (L, LANES), 1) < head_dim
    lo_row = lo[0:1, :]

    for g in range(n_groups):
        b_gt = b_ref[0, :, g * SSD_STATE:(g + 1) * SSD_STATE].T
        c_g = c_ref[0, :, g * SSD_STATE:(g + 1) * SSD_STATE].astype(BF16)
        cb = jnp.dot(c_g, b_gt.astype(BF16), preferred_element_type=F32)
        y_off = jnp.dot(c_g, state_ref[g].astype(BF16), preferred_element_type=F32)
        y_pairs = []
        for pp in range(hpg // 2):
            h0 = g * hpg + 2 * pp
            h1 = h0 + 1
            c0 = h0 * head_dim
            lc = 2 * pp * head_dim
            xs_pair = xs_ref[0, :, c0:c0 + LANES]
            rhs = jnp.concatenate([jnp.where(lo, xs_pair, 0.0).astype(BF16),
                                   jnp.where(lo, 0.0, xs_pair).astype(BF16)], axis=0)
            xc0 = jnp.broadcast_to(x_c[:, h0:h0 + 1], (L, L))
            xc1 = jnp.broadcast_to(x_c[:, h1:h1 + 1], (L, L))
            m0 = cb * jnp.exp(jnp.where(keep, xc0 - x_t[h0:h0 + 1, :], -jnp.inf)) * dt_t[h0:h0 + 1, :]
            m1 = cb * jnp.exp(jnp.where(keep, xc1 - x_t[h1:h1 + 1, :], -jnp.inf)) * dt_t[h1:h1 + 1, :]
            y_diag = jnp.dot(jnp.concatenate([m0.astype(BF16), m1.astype(BF16)], axis=1), rhs,
                             preferred_element_type=F32)
            e_out = jnp.where(lo, jnp.exp(xc0), jnp.exp(xc1))
            y_pair = y_diag + y_off[:, lc:lc + LANES] * e_out
            if backward:
                y_pairs.append(y_pair)
            else:
                y_ref[0, :, c0:c0 + LANES] = y_pair
            lhs_b = jnp.concatenate([(b_gt * w_t[h0:h0 + 1, :]).astype(BF16),
                                     (b_gt * w_t[h1:h1 + 1, :]).astype(BF16)], axis=1)
            dec = jnp.where(lo_row, e_tot[:, h0:h0 + 1], e_tot[:, h1:h1 + 1])
            state_ref[g, :, lc:lc + LANES] = (state_ref[g, :, lc:lc + LANES] * dec
                                              + jnp.dot(lhs_b, rhs, preferred_element_type=F32))
        if backward:
            cols = slice(g * gw, (g + 1) * gw)
            y = (yf_ref[0, :, cols] + jnp.concatenate(y_pairs, axis=1)
                 + dskip_ref[:, cols] * xs_ref[0, :, cols])
            z = z_ref[0, :, cols]
            y = y * (z * jax.nn.sigmoid(z))
            ms = jnp.mean(y * y, axis=-1, keepdims=True)
            y_ref[0, :, cols] = (y * lax.rsqrt(ms + EPS) * gnorm_ref[:, cols]).astype(y_ref.dtype)


def ssd_scan(xbc, dt_raw, dt_bias, a_log, layer, *, n_heads, dt_col0=0, epilogue=None):
    b_sz, t_len, _ = xbc.shape
    d_ssd = n_heads * SSD_HEADDIM
    gn = SSD_GROUPS * SSD_STATE
    L = SSD_CHUNK
    assert L == LANES and 2 * SSD_HEADDIM == LANES and (n_heads // SSD_GROUPS) % 2 == 0
    nc = t_len // L
    xb = d_ssd // gn
    backward = epilogue is not None
    direction = int(backward)

    def chunk_of(c):
        return nc - 1 - c if backward else c

    in_specs = [pl.BlockSpec((1, L, d_ssd), lambda b, c: (b, chunk_of(c), 0)),
                pl.BlockSpec((1, L, gn), lambda b, c: (b, chunk_of(c), xb)),
                pl.BlockSpec((1, L, gn), lambda b, c: (b, chunk_of(c), xb + 1)),
                pl.BlockSpec((1, L, LANES), lambda b, c: (b, chunk_of(c), dt_col0 + direction)),
                pl.BlockSpec((1, n_heads, L), lambda b, c: (layer * 2 + direction, 0, 0)),
                pl.BlockSpec((1, n_heads, L), lambda b, c: (layer * 2 + direction, 0, 0))]
    args = [xbc, xbc, xbc, dt_raw, dt_bias, a_log]
    if backward:
        in_specs += [pl.BlockSpec((1, L, d_ssd), lambda b, c: (b, chunk_of(c), 0)),
                     pl.BlockSpec((1, L, d_ssd), lambda b, c: (b, chunk_of(c), 0)),
                     pl.BlockSpec((1, d_ssd), lambda b, c: (0, 0)),
                     pl.BlockSpec((1, d_ssd), lambda b, c: (0, 0))]
        args += list(epilogue)
    return pl.pallas_call(
        functools.partial(_ssd_body, chunk=L, n_heads=n_heads, head_dim=SSD_HEADDIM,
                          n_groups=SSD_GROUPS, backward=backward),
        grid=(b_sz, nc),
        in_specs=in_specs,
        out_specs=pl.BlockSpec((1, L, d_ssd), lambda b, c: (b, chunk_of(c), 0)),
        out_shape=jax.ShapeDtypeStruct((b_sz, t_len, d_ssd), BF16 if backward else F32),
        scratch_shapes=[pltpu.VMEM((SSD_GROUPS, SSD_STATE, d_ssd // SSD_GROUPS), F32)],
        compiler_params=_params(2),
        name="ssd_scan_bwd_gate" if backward else "ssd_scan_fwd",
    )(*args)


def _softmax_rows(s):
    m = jnp.max(s, axis=-1, keepdims=True)
    e = jnp.exp(s - m)
    return e / jnp.sum(e, axis=-1, keepdims=True)


def _mem_attn_body(q_ref, kv_ref, o_ref, *, n_heads, dh):
    scale = dh ** -0.5
    for h in range(n_heads):
        q = q_ref[0, :, h * dh:(h + 1) * dh].astype(BF16)
        k = kv_ref[0, :, h * dh:(h + 1) * dh]
        v = kv_ref[0, :, (n_heads + h) * dh:(n_heads + h + 1) * dh]
        p = _softmax_rows(_nt_dot(q, k) * scale).astype(BF16)
        o_ref[0, :, h * dh:(h + 1) * dh] = jnp.dot(p, v, preferred_element_type=F32).astype(o_ref.dtype)


def mem_attention(q3, q_col_block, kv3, tm=512):
    b_sz, t_len, _ = q3.shape
    d_xa = XA_HEADS * XA_DH
    mem_len = kv3.shape[1]
    return pl.pallas_call(
        functools.partial(_mem_attn_body, n_heads=XA_HEADS, dh=XA_DH),
        grid=(b_sz, t_len // tm),
        in_specs=[pl.BlockSpec((1, tm, d_xa), lambda b, i: (b, i, q_col_block)),
                  pl.BlockSpec((1, mem_len, 2 * d_xa), lambda b, i: (b, 0, 0))],
        out_specs=pl.BlockSpec((1, tm, d_xa), lambda b, i: (b, i, 0)),
        out_shape=jax.ShapeDtypeStruct((b_sz, t_len, d_xa), BF16),
        compiler_params=_params(2),
        name="mem_attention",
    )(q3, kv3)


NA_QROWS = 4
NA_KROWS = 12
NEG_MASK = -1e30


def _na_bias_body(rpb_ref, o_ref, *, n_dr, width, kc):
    c_idx = lax.broadcasted_iota(I32, (width, LANES), 0)
    lane = lax.broadcasted_iota(I32, (width, LANES), 1)
    first = lane < width
    k_idx = jnp.where(first, lane, lane - width)
    w_start = jnp.clip(c_idx - kc // 2, 0, width - kc)
    valid = (k_idx >= w_start) & (k_idx < w_start + kc)
    base = LANES - (kc - 1)
    neg = jnp.full((width, LANES), NEG_MASK, F32)
    lo_half, hi_half = [], []
    for d in range(n_dr):
        x = jnp.broadcast_to(rpb_ref[0, d:d + 1, :], (width, LANES))
        lo_half.append(jnp.where(valid & first, pltpu.roll(x, base, 1, stride=1, stride_axis=0), neg))
        hi_half.append(jnp.where(valid & jnp.logical_not(first),
                                 pltpu.roll(x, (base + width) % LANES, 1, stride=1, stride_axis=0), neg))
    for d in range(n_dr - 1):
        o_ref[0, d] = jnp.where(first, lo_half[d], hi_half[d + 1])
    for d in range(n_dr):
        o_ref[0, n_dr - 1 + d] = lo_half[d]
        o_ref[0, 2 * n_dr - 1 + d] = hi_half[d]
    o_ref[0, 3 * n_dr - 1] = neg


def na_bias_table(rpb_padded):
    n_heads, n_dr, _ = rpb_padded.shape
    n_ent = 3 * n_dr
    return pl.pallas_call(
        functools.partial(_na_bias_body, n_dr=n_dr, width=GRID_W, kc=NA_KC),
        grid=(n_heads,),
        in_specs=[pl.BlockSpec((1, n_dr, LANES), lambda h: (h, 0, 0))],
        out_specs=pl.BlockSpec((1, n_ent, GRID_W, 2 * GRID_W), lambda h: (h, 0, 0, 0)),
        out_shape=jax.ShapeDtypeStruct((n_heads, n_ent, GRID_W, 2 * GRID_W), F32),
        compiler_params=_params(1),
        name="na_bias_table",
    )(rpb_padded)


def _na_table_entry(r, ka, rows, kr, n_dr):
    rs = min(max(r - kr // 2, 0), rows - kr)
    in_a = rs <= ka < rs + kr
    in_b = rs <= ka + 1 < rs + kr
    d_a = ka - r + (NA_KR - 1)
    if in_a and in_b:
        return d_a
    if in_a:
        return n_dr - 1 + d_a
    if in_b:
        return 2 * n_dr - 1 + d_a + 1
    return 3 * n_dr - 1


def _na_body(q_ref, k_ref, v_ref, tb_ref, o_ref, *, rows, width, kr, dh):
    scale = dh ** -0.5
    n_dr = 2 * NA_KR - 1
    nq = NA_QROWS * width
    nk = NA_KROWS * width
    for blk in range(rows // NA_QROWS):
        r0 = blk * NA_QROWS
        k_row0 = min(max(r0 - kr // 2, 0), rows - NA_KROWS)
        q = q_ref[0, r0 * width:r0 * width + nq, :]
        kw = k_ref[0, k_row0 * width:k_row0 * width + nk, :]
        vw = v_ref[0, k_row0 * width:k_row0 * width + nk, :]
        s = _nt_dot(q, kw) * scale
        s = jnp.concatenate([
            jnp.concatenate([
                s[i * width:(i + 1) * width, j * LANES:(j + 1) * LANES]
                + tb_ref[0, _na_table_entry(r0 + i, k_row0 + 2 * j, rows, kr, n_dr)]
                for j in range(nk // LANES)], axis=1)
            for i in range(NA_QROWS)], axis=0)
        m = jnp.max(s, axis=-1, keepdims=True)
        e = jnp.exp(s - m)
        den = jnp.sum(e, axis=-1, keepdims=True)
        o = jnp.dot(e.astype(BF16), vw, preferred_element_type=F32) / den
        o_ref[0, r0 * width:r0 * width + nq, :] = o.astype(o_ref.dtype)


def na_attention(proj3, table, n_heads):
    b_sz, t_len, _ = proj3.shape
    rows = t_len // GRID_W
    kr = min(NA_KR, rows)
    assert kr == NA_KR and rows % NA_QROWS == 0 and rows >= NA_KROWS and (rows - NA_KROWS) % 2 == 0
    assert NA_KROWS >= kr + NA_QROWS - 1 and 2 * GRID_W == LANES
    n_dr2 = table.shape[1]
    return pl.pallas_call(
        functools.partial(_na_body, rows=rows, width=GRID_W, kr=kr, dh=NA_DH),
        grid=(b_sz, n_heads),
        in_specs=[pl.BlockSpec((1, t_len, NA_DH), lambda b, h: (b, 0, h)),
                  pl.BlockSpec((1, t_len, NA_DH), lambda b, h: (b, 0, n_heads + h)),
                  pl.BlockSpec((1, t_len, NA_DH), lambda b, h: (b, 0, 2 * n_heads + h)),
                  pl.BlockSpec((1, n_dr2, GRID_W, 2 * GRID_W), lambda b, h: (h, 0, 0, 0))],
        out_specs=pl.BlockSpec((1, t_len, NA_DH), lambda b, h: (b, 0, h)),
        out_shape=jax.ShapeDtypeStruct((b_sz, t_len, n_heads * NA_DH), BF16),
        compiler_params=_params(2),
        name="na_attention",
    )(proj3, proj3, proj3, table)


def _norm_and_route(x, g_ref, w_ref, h_ref, aff_ref):
    ms = jnp.mean(x * x, axis=-1, keepdims=True)
    hn = x * lax.rsqrt(ms + EPS) * g_ref[...]
    hn_hi = hn.astype(BF16)
    h_ref[...] = hn_hi
    hn_lo = (hn - hn_hi.astype(F32)).astype(BF16)
    w = w_ref[0]
    w_hi = w.astype(BF16)
    w_lo = (w - w_hi.astype(F32)).astype(BF16)
    n_exp = w.shape[1]
    both = jnp.dot(hn_hi, jnp.concatenate([w_hi, w_lo], axis=1), preferred_element_type=F32)
    logits = both[:, :n_exp] + (both[:, n_exp:] + jnp.dot(hn_lo, w_hi, preferred_element_type=F32))
    aff_ref[...] = _softmax_rows(logits)


def _out_proj_router_body(a1_ref, a2_ref, w1_ref, w2_ref, res_ref, g_ref, wr_ref,
                          x_ref, h_ref, aff_ref, wbf1_ref, wbf2_ref, *, sub_rows):
    @pl.when(pl.program_id(0) == 0)
    def _():
        wbf1_ref[...] = w1_ref[0].astype(BF16)
        wbf2_ref[...] = w2_ref[0].astype(BF16)

    for r in range(0, a1_ref.shape[0], sub_rows):
        rows = slice(r, r + sub_rows)
        acc = (jnp.dot(a1_ref[rows, :], wbf1_ref[...], preferred_element_type=F32)
               + jnp.dot(a2_ref[rows, :], wbf2_ref[...], preferred_element_type=F32))
        x = acc + res_ref[rows, :]
        x_ref[rows, :] = x
        _norm_and_route(x, g_ref, wr_ref, h_ref.at[rows, :], aff_ref.at[rows, :])


def out_proj_router(a1, a2, w_out, layer_w, res, norm_g, w_router, layer, tm=512, sub_rows=256):
    m, k1 = a1.shape
    k2 = a2.shape[1]
    d = w_out.shape[-1]
    n_exp = w_router.shape[-1]
    assert k1 % k2 == 0
    once = pl.Buffered(1)
    return pl.pallas_call(
        functools.partial(_out_proj_router_body, sub_rows=sub_rows),
        grid=(m // tm,),
        in_specs=[pl.BlockSpec((tm, k1), lambda i: (i, 0)),
                  pl.BlockSpec((tm, k2), lambda i: (i, 0)),
                  pl.BlockSpec((1, k1, d), lambda i: (layer_w, 0, 0), pipeline_mode=once),
                  pl.BlockSpec((1, k2, d), lambda i: (layer_w, k1 // k2, 0), pipeline_mode=once),
                  pl.BlockSpec((tm, d), lambda i: (i, 0)),
                  pl.BlockSpec((1, d), lambda i: (0, 0)),
                  pl.BlockSpec((1, d, n_exp), lambda i: (layer, 0, 0))],
        out_specs=[pl.BlockSpec((tm, d), lambda i: (i, 0)),
                   pl.BlockSpec((tm, d), lambda i: (i, 0)),
                   pl.BlockSpec((tm, n_exp), lambda i: (i, 0))],
        out_shape=[jax.ShapeDtypeStruct((m, d), F32),
                   jax.ShapeDtypeStruct((m, d), BF16),
                   jax.ShapeDtypeStruct((m, n_exp), F32)],
        scratch_shapes=[pltpu.VMEM((k1, d), BF16), pltpu.VMEM((k2, d), BF16)],
        compiler_params=_params(1),
        name="out_proj_router",
    )(a1, a2, w_out, w_out, res, norm_g.reshape(1, d), w_router)


TOKEN_BLK = 128
SLOT_WIN = 64
SLOT_ALIGN_LOG2 = 4


def _slot_window(c0, c1, cap):
    st = jnp.minimum(lax.shift_left(lax.shift_right_logical(c0, SLOT_ALIGN_LOG2), SLOT_ALIGN_LOG2),
                     cap - SLOT_WIN)
    return st, c1 <= st + SLOT_WIN


def _select_body(aff_ref, enc_ref, encrow_ref, affrow_ref, cnt_ref, *, cap):
    a = aff_ref[0]
    t_len, n_exp = a.shape
    zero = jnp.zeros((1, n_exp), I32)

    def count(pred):
        return jnp.sum(pred.astype(I32), axis=0, keepdims=True)

    def value_bit(i, prefix):
        cand = prefix | jnp.left_shift(jnp.int32(1), 30 - i)
        return jnp.where(count(a >= lax.bitcast_convert_type(cand, F32)) >= cap, cand, prefix)

    thresh = lax.bitcast_convert_type(lax.fori_loop(0, 31, value_bit, zero), F32)
    above = a > thresh
    tied = a == thresh
    need = cap - count(above)
    idx = lax.broadcasted_iota(I32, (t_len, n_exp), 0)
    n_idx_bits = (t_len - 1).bit_length()

    def index_bit(i, last):
        cand = last | jnp.left_shift(jnp.int32(1), n_idx_bits - 1 - i)
        return jnp.where(count(tied & (idx < cand)) < need, cand, last)

    last = lax.fori_loop(0, n_idx_bits, index_bit, zero)
    mask = above | (tied & (idx <= last))
    mask_bf = mask.astype(BF16)

    n_rows = cnt_ref.shape[1]
    bound = lax.broadcasted_iota(I32, (n_rows, t_len), 0) * TOKEN_BLK
    tok = lax.broadcasted_iota(I32, (n_rows, t_len), 1)
    cnt = jnp.dot((tok < bound).astype(BF16), mask_bf, preferred_element_type=F32)
    cnt_ref[0] = cnt.astype(I32)
    earlier = (lax.broadcasted_iota(I32, (TOKEN_BLK, TOKEN_BLK), 0)
               > lax.broadcasted_iota(I32, (TOKEN_BLK, TOKEN_BLK), 1)).astype(BF16)
    for i in range(t_len // TOKEN_BLK):
        rows = slice(i * TOKEN_BLK, (i + 1) * TOKEN_BLK)
        before = jnp.dot(earlier, mask_bf[rows], preferred_element_type=F32) + cnt[i:i + 1, :]
        enc_ref[0, rows, :] = jnp.where(mask[rows], before + 1.0, 0.0).astype(I32)

    eye = (lax.broadcasted_iota(I32, (n_exp, n_exp), 0)
           == lax.broadcasted_iota(I32, (n_exp, n_exp), 1)).astype(BF16)
    enc_bf = enc_ref[0].astype(F32).astype(BF16)
    encrow_ref[0] = _nt_dot(eye, enc_bf).astype(I32)

    a_hi = a.astype(BF16)
    rest = a - a_hi.astype(F32)
    a_mid = rest.astype(BF16)
    a_lo = (rest - a_mid.astype(F32)).astype(BF16)
    affrow_ref[0] = _nt_dot(eye, a_hi) + (_nt_dot(eye, a_mid) + _nt_dot(eye, a_lo))


def moe_select(aff3, cap):
    b_sz, t_len, n_exp = aff3.shape
    assert cap <= 256, "slot codes must stay exactly representable in bf16"
    n_rows = -(-(t_len // TOKEN_BLK + 1) // 8) * 8
    return pl.pallas_call(
        functools.partial(_select_body, cap=cap),
        grid=(b_sz,),
        in_specs=[pl.BlockSpec((1, t_len, n_exp), lambda b: (b, 0, 0))],
        out_specs=[pl.BlockSpec((1, t_len, n_exp), lambda b: (b, 0, 0)),
                   pl.BlockSpec((1, n_exp, t_len), lambda b: (b, 0, 0)),
                   pl.BlockSpec((1, n_exp, t_len), lambda b: (b, 0, 0)),
                   pl.BlockSpec((1, n_rows, n_exp), lambda b: (b, 0, 0))],
        out_shape=[jax.ShapeDtypeStruct((b_sz, t_len, n_exp), I32),
                   jax.ShapeDtypeStruct((b_sz, n_exp, t_len), I32),
                   jax.ShapeDtypeStruct((b_sz, n_exp, t_len), F32),
                   jax.ShapeDtypeStruct((b_sz, n_rows, n_exp), I32)],
        compiler_params=_params(1),
        name="moe_select",
    )(aff3)


GATHER_GROUP = 4


def _gather_body(cnt_ref, h_ref, encrow_ref, affrow_ref, xs_ref, g_ref, acc_ref, gacc_ref, *, cap, kb):
    b = pl.program_id(0)
    e0 = pl.program_id(1) * GATHER_GROUP
    t_len = h_ref.shape[1]
    per = kb // TOKEN_BLK
    codes = [encrow_ref[0, pl.ds(e0 + i, 1), :] for i in range(GATHER_GROUP)]
    gates = [affrow_ref[0, pl.ds(e0 + i, 1), :] for i in range(GATHER_GROUP)]
    starts, fits = [], None
    for j in range(t_len // kb):
        row = []
        for i in range(GATHER_GROUP):
            st, ok = _slot_window(cnt_ref[b, j * per, e0 + i], cnt_ref[b, (j + 1) * per, e0 + i], cap)
            row.append(pl.multiple_of(st, 1 << SLOT_ALIGN_LOG2))
            fits = ok if fits is None else jnp.logical_and(fits, ok)
        starts.append(row)

    @pl.when(fits)
    def _():
        acc_ref[...] = jnp.zeros_like(acc_ref)
        gacc_ref[...] = jnp.zeros_like(gacc_ref)
        sub = lax.broadcasted_iota(I32, (SLOT_WIN, kb), 0) + 1
        for j, row in enumerate(starts):
            tok = slice(j * kb, (j + 1) * kb)
            hits = [(sub + row[i]) == codes[i][:, tok] for i in range(GATHER_GROUP)]
            picked = jnp.dot(jnp.concatenate([hit.astype(BF16) for hit in hits], axis=0), h_ref[0, tok, :],
                             preferred_element_type=F32)
            for i in range(GATHER_GROUP):
                acc_ref[i, pl.ds(row[i], SLOT_WIN), :] += picked[i * SLOT_WIN:(i + 1) * SLOT_WIN]
                g = jnp.sum(jnp.where(hits[i], gates[i][:, tok], 0.0), axis=1, keepdims=True)
                gacc_ref[i, pl.ds(row[i], SLOT_WIN), :] += jnp.broadcast_to(g, (SLOT_WIN, LANES))
        xs_ref[...] = acc_ref[...].astype(BF16)
        g_ref[...] = gacc_ref[...]

    @pl.when(jnp.logical_not(fits))
    def _():
        slot = lax.broadcasted_iota(I32, (cap, t_len), 0) + 1
        for i in range(GATHER_GROUP):
            hit = slot == codes[i]
            xs_ref[i] = jnp.dot(hit.astype(BF16), h_ref[0], preferred_element_type=F32).astype(BF16)
            g = jnp.sum(jnp.where(hit, gates[i], 0.0), axis=1, keepdims=True)
            g_ref[i] = jnp.broadcast_to(g, (cap, LANES))


def moe_gather(h3, encrow, affrow, cnt, cap, kb=256):
    b_sz, t_len, d = h3.shape
    n_exp = encrow.shape[1]
    assert cap >= SLOT_WIN and t_len % kb == 0 and kb % TOKEN_BLK == 0 and n_exp % GATHER_GROUP == 0
    grp = GATHER_GROUP
    return pl.pallas_call(
        functools.partial(_gather_body, cap=cap, kb=kb),
        grid_spec=pltpu.PrefetchScalarGridSpec(
            num_scalar_prefetch=1,
            grid=(b_sz, n_exp // grp),
            in_specs=[pl.BlockSpec((1, t_len, d), lambda b, q, cnt: (b, 0, 0)),
                      pl.BlockSpec((1, n_exp, t_len), lambda b, q, cnt: (b, 0, 0)),
                      pl.BlockSpec((1, n_exp, t_len), lambda b, q, cnt: (b, 0, 0))],
            out_specs=[pl.BlockSpec((grp, cap, d), lambda b, q, cnt: (q, b, 0)),
                       pl.BlockSpec((grp, cap, LANES), lambda b, q, cnt: (q, b, 0))],
            scratch_shapes=[pltpu.VMEM((grp, cap, d), F32), pltpu.VMEM((grp, cap, LANES), F32)]),
        out_shape=[jax.ShapeDtypeStruct((n_exp, b_sz * cap, d), BF16),
                   jax.ShapeDtypeStruct((n_exp, b_sz * cap, LANES), F32)],
        compiler_params=_params(2),
        name="moe_gather",
    )(cnt, h3, encrow, affrow)


def _ffn_up_body(xs_ref, w1_ref, w3_ref, hid_ref):
    xs = xs_ref[0]
    a = jnp.dot(xs, w1_ref[0, 0].astype(BF16), preferred_element_type=F32)
    b = jnp.dot(xs, w3_ref[0, 0].astype(BF16), preferred_element_type=F32)
    hid_ref[0] = (a * jax.nn.sigmoid(a) * b).astype(hid_ref.dtype)


def _ffn_down_body(hid_ref, w2_ref, g_ref, y_ref):
    y = jnp.dot(hid_ref[0], w2_ref[0, 0].astype(BF16), preferred_element_type=F32)
    gate = jnp.concatenate([g_ref[0]] * (y.shape[1] // LANES), axis=1)
    y_ref[0] = (y * gate).astype(y_ref.dtype)


def moe_ffn(xs, gate, w1, w3, w2, layer, tf=512, tn=2048):
    n_exp, rows, d = xs.shape
    d_exp = w1.shape[-1]
    hid = pl.pallas_call(
        _ffn_up_body,
        grid=(n_exp, d_exp // tf),
        in_specs=[pl.BlockSpec((1, rows, d), lambda e, f: (e, 0, 0)),
                  pl.BlockSpec((1, 1, d, tf), lambda e, f: (layer, e, 0, f)),
                  pl.BlockSpec((1, 1, d, tf), lambda e, f: (layer, e, 0, f))],
        out_specs=pl.BlockSpec((1, rows, tf), lambda e, f: (e, 0, f)),
        out_shape=jax.ShapeDtypeStruct((n_exp, rows, d_exp), BF16),
        compiler_params=_params(2),
        name="moe_ffn_up",
    )(xs, w1, w3)
    return pl.pallas_call(
        _ffn_down_body,
        grid=(n_exp, d // tn),
        in_specs=[pl.BlockSpec((1, rows, d_exp), lambda e, n: (e, 0, 0)),
                  pl.BlockSpec((1, 1, d_exp, tn), lambda e, n: (layer, e, 0, n)),
                  pl.BlockSpec((1, rows, LANES), lambda e, n: (e, 0, 0))],
        out_specs=pl.BlockSpec((1, rows, tn), lambda e, n: (e, 0, n)),
        out_shape=jax.ShapeDtypeStruct((n_exp, rows, d), BF16),
        compiler_params=_params(2),
        name="moe_ffn_down",
    )(hid, w2, gate)


SCATTER_GROUP = 4


def _scatter_body(cnt_ref, x_ref, y_ref, enc_ref, gn_ref, o_ref, hn_ref, *, cap):
    b = pl.program_id(0)
    j = pl.program_id(1)
    enc = enc_ref[0]
    n_tok, n_exp = enc.shape
    starts, fits = [], None
    for e in range(n_exp):
        st, ok = _slot_window(cnt_ref[b, j, e], cnt_ref[b, j + 1, e], cap)
        starts.append(pl.multiple_of(st, 1 << SLOT_ALIGN_LOG2))
        fits = ok if fits is None else jnp.logical_and(fits, ok)
    o_ref[0] = x_ref[0]

    @pl.when(fits)
    def _():
        width = SCATTER_GROUP * SLOT_WIN
        lane = lax.broadcasted_iota(I32, (n_tok, width), 1)
        lane_row = lane[0:1, :]
        within = jnp.bitwise_and(lane_row, SLOT_WIN - 1) + 1
        onehots, windows = [], []
        for q in range(n_exp // SCATTER_GROUP):
            es = range(q * SCATTER_GROUP, (q + 1) * SCATTER_GROUP)
            code = enc[:, es[-1]:es[-1] + 1]
            first = starts[es[-1]]
            for k in reversed(range(SCATTER_GROUP - 1)):
                code = jnp.where(lane < (k + 1) * SLOT_WIN, enc[:, es[k]:es[k] + 1], code)
                first = jnp.where(lane_row < (k + 1) * SLOT_WIN, starts[es[k]], first)
            onehots.append((code == first + within).astype(BF16))
            windows += [y_ref[e, pl.ds(starts[e], SLOT_WIN), :] for e in es]
        o_ref[0] += jnp.dot(jnp.concatenate(onehots, axis=1), jnp.concatenate(windows, axis=0),
                            preferred_element_type=F32)

    @pl.when(jnp.logical_not(fits))
    def _():
        slot = lax.broadcasted_iota(I32, (n_tok, cap), 1) + 1
        for e in range(n_exp):
            onehot = (enc[:, e:e + 1] == slot).astype(BF16)
            o_ref[0] += jnp.dot(onehot, y_ref[e], preferred_element_type=F32)

    x_new = o_ref[0]
    ms = jnp.mean(x_new * x_new, axis=-1, keepdims=True)
    hn_ref[0] = (x_new * lax.rsqrt(ms + EPS) * gn_ref[...]).astype(hn_ref.dtype)


def moe_scatter_add(x3, y, enc, cnt, norm_g, norm_dtype, cap):
    b_sz, t_len, d = x3.shape
    n_exp = enc.shape[-1]
    assert cap >= SLOT_WIN and n_exp % SCATTER_GROUP == 0 and SLOT_WIN & (SLOT_WIN - 1) == 0
    return pl.pallas_call(
        functools.partial(_scatter_body, cap=cap),
        grid_spec=pltpu.PrefetchScalarGridSpec(
            num_scalar_prefetch=1,
            grid=(b_sz, t_len // TOKEN_BLK),
            in_specs=[pl.BlockSpec((1, TOKEN_BLK, d), lambda b, j, cnt: (b, j, 0)),
                      pl.BlockSpec((n_exp, cap, d), lambda b, j, cnt: (0, b, 0)),
                      pl.BlockSpec((1, TOKEN_BLK, n_exp), lambda b, j, cnt: (b, j, 0)),
                      pl.BlockSpec((1, d), lambda b, j, cnt: (0, 0))],
            out_specs=[pl.BlockSpec((1, TOKEN_BLK, d), lambda b, j, cnt: (b, j, 0)),
                       pl.BlockSpec((1, TOKEN_BLK, d), lambda b, j, cnt: (b, j, 0))]),
        out_shape=[jax.ShapeDtypeStruct((b_sz, t_len, d), F32),
                   jax.ShapeDtypeStruct((b_sz, t_len, d), norm_dtype)],
        compiler_params=_params(2),
        name="moe_scatter_add",
    )(cnt, x3, y, enc, norm_g.reshape(1, d))


def ec_moe_layer(x3, h2, aff2, w1, w3, w2, layer, next_norm_g, next_norm_dtype):
    b_sz, t_len, d = x3.shape
    n_exp = aff2.shape[-1]
    cap = EC_FACTOR * t_len // n_exp
    enc, encrow, affrow, cnt = moe_select(aff2.reshape(b_sz, t_len, n_exp), cap)
    xs, gate = moe_gather(h2.reshape(b_sz, t_len, d), encrow, affrow, cnt, cap)
    y = moe_ffn(xs, gate, w1, w3, w2, layer)
    return moe_scatter_add(x3, y, enc, cnt, next_norm_g, next_norm_dtype, cap)


def _layer_cols(w, layer, col0, n_cols):
    return lax.slice(w, (layer, 0, col0), (layer + 1, w.shape[1], col0 + n_cols))[0]


def ssd_mixer_layer(h2, kv3, b_sz, t_len, j, ssd_w_in, conv_w, conv_b, dt_bias, a_log, d_skip, gate_g):
    m, d = h2.shape
    n_heads = d_skip.shape[-1]
    d_ssd = n_heads * SSD_HEADDIM
    gn = SSD_GROUPS * SSD_STATE
    conv_dim = d_ssd + 2 * gn
    d_xa = XA_HEADS * XA_DH
    n_main = d_ssd + conv_dim
    tn = 1024
    proj = matmul_ws([h2], [(ssd_w_in, (1, d, tn), lambda n: (j, 0, n))], n_main,
                     tn=tn, tm=512, out_dtype=F32, name="ssd_in_proj")
    w_dt = _layer_cols(ssd_w_in, j, n_main, 2 * n_heads)
    w_dt = jnp.pad(w_dt.reshape(d, 2, n_heads), ((0, 0), (0, 0), (0, LANES - n_heads))).reshape(d, 2 * LANES)
    w_tail = jnp.concatenate([_layer_cols(ssd_w_in, j, n_main + 2 * n_heads, d_xa), w_dt], axis=1)
    n_tail = d_xa + 2 * LANES
    tail = matmul_ws([h2], [(w_tail, (d, n_tail), lambda n: (0, 0))], n_tail,
                     tn=n_tail, tm=512, out_dtype=F32, name="ssd_tail_proj").reshape(b_sz, t_len, n_tail)

    proj3 = proj.reshape(b_sz, t_len, n_main)
    xbc = conv_silu(proj3, conv_w, conv_b, j, d_ssd, conv_dim)
    per_head = (dt_bias.shape[0] * 2, n_heads, SSD_CHUNK)
    bias_t = jnp.broadcast_to(dt_bias.reshape(-1, n_heads, 1), per_head)
    alog_t = jnp.broadcast_to(a_log.reshape(-1, n_heads, 1), per_head)
    dt_col0 = d_xa // LANES
    y_fwd = ssd_scan(xbc, tail, bias_t, alog_t, j, n_heads=n_heads, dt_col0=dt_col0)
    y = ssd_scan(xbc, tail, bias_t, alog_t, j, n_heads=n_heads, dt_col0=dt_col0,
                 epilogue=(y_fwd, proj3, jnp.repeat(d_skip[j], SSD_HEADDIM).reshape(1, d_ssd),
                           gate_g[j].reshape(1, d_ssd))).reshape(m, d_ssd)
    o_x = mem_attention(tail, 0, kv3).reshape(m, d_xa)
    return y, o_x


def na_mixer_layer(h2, kv3, b_sz, t_len, j, na_w_in, na_rpb):
    m, d = h2.shape
    n_heads = na_rpb.shape[1]
    d_na = n_heads * NA_DH
    d_xa = XA_HEADS * XA_DH
    n_in = 3 * d_na + d_xa
    tn = 1024
    proj = matmul_ws([h2], [(na_w_in, (1, d, tn), lambda n: (j, 0, n))], n_in,
                     tn=tn, tm=512, out_dtype=BF16, name="na_in_proj")
    proj3 = proj.reshape(b_sz, t_len, n_in)
    rpb = na_rpb[j]
    table = na_bias_table(jnp.pad(rpb, ((0, 0), (0, 0), (0, LANES - rpb.shape[-1]))))
    o_na = na_attention(proj3, table, n_heads).reshape(m, d_na)
    o_x = mem_attention(proj3, 3 * d_na // d_xa, kv3).reshape(m, d_xa)
    return o_na, o_x


def kernel(x, mem, norm_mix_g, norm_ffn_g, norm_final_g, mem_norm_g, ssd_w_in, ssd_conv_w, ssd_conv_b,
           ssd_dt_bias, ssd_a_log, ssd_d, ssd_gate_norm_g, ssd_w_out, na_w_in, na_rpb, na_w_out, xa_w_kv,
           moe_w_router, moe_w1, moe_w3, moe_w2):
    b_sz, t_len, d = x.shape
    mem_len = mem.shape[1]
    depth = norm_mix_g.shape[0]
    m = b_sz * t_len
    d_kv = xa_w_kv.shape[-1]
    mem_n = rmsnorm_rows(mem.reshape(b_sz * mem_len, d), mem_norm_g, BF16)
    x2 = x.reshape(m, d)
    h2 = rmsnorm_rows(x2, norm_mix_g[0], BF16)
    for i in range(depth):
        j = i // N_MIXERS
        kv3 = matmul_ws([mem_n], [(xa_w_kv, (1, d, d_kv), lambda n, i=i: (i, 0, 0))], d_kv,
                        tn=d_kv, tm=512, out_dtype=BF16, name="xa_kv_proj").reshape(b_sz, mem_len, d_kv)
        if i % N_MIXERS == 0:
            heads, o_x = ssd_mixer_layer(h2, kv3, b_sz, t_len, j, ssd_w_in, ssd_conv_w, ssd_conv_b,
                                         ssd_dt_bias, ssd_a_log, ssd_d, ssd_gate_norm_g)
            w_out = ssd_w_out
        else:
            heads, o_x = na_mixer_layer(h2, kv3, b_sz, t_len, j, na_w_in, na_rpb)
            w_out = na_w_out
        x2, h_ffn, aff2 = out_proj_router(heads, o_x, w_out, j, x2, norm_ffn_g[i], moe_w_router, i)
        last = i == depth - 1
        x3, normed = ec_moe_layer(x2.reshape(b_sz, t_len, d), h_ffn, aff2, moe_w1, moe_w3, moe_w2, i,
                                  norm_final_g if last else norm_mix_g[i + 1], F32 if last else BF16)
        x2, h2 = x3.reshape(m, d), normed.reshape(m, d)
    return normed
```

```python
import functools

import jax
import jax.numpy as jnp
from jax import lax
from jax.experimental import pallas as pl
from jax.experimental.pallas import tpu as pltpu

F32 = jnp.float32
BF16 = jnp.bfloat16
I32 = jnp.int32
HIGHEST = lax.Precision.HIGHEST

EPS = 1e-6
LANES = 128
VMEM_LIMIT_BYTES = 56 * 1024 * 1024

XA_HEADS = 4
XA_DH = 128
SSD_HEADDIM = 64
SSD_GROUPS = 4
SSD_STATE = 128
SSD_CONV = 5
SSD_CHUNK = 128
NA_DH = 128
NA_KR = 8
NA_KC = 16
GRID_W = 64
EC_FACTOR = 2
N_MIXERS = 2


def _params(n_grid_dims):
    return pltpu.CompilerParams(
        dimension_semantics=("arbitrary",) * n_grid_dims,
        vmem_limit_bytes=VMEM_LIMIT_BYTES)


def _nt_dot(a, b):
    return lax.dot_general(a, b, (((1,), (1,)), ((), ())), preferred_element_type=F32)


def _rmsnorm_body(x_ref, g_ref, o_ref):
    x = x_ref[...]
    ms = jnp.mean(x * x, axis=-1, keepdims=True)
    o_ref[...] = (x * lax.rsqrt(ms + EPS) * g_ref[...]).astype(o_ref.dtype)


def rmsnorm_rows(x2d, g, out_dtype, tm=256):
    m, d = x2d.shape
    return pl.pallas_call(
        _rmsnorm_body,
        grid=(m // tm,),
        in_specs=[pl.BlockSpec((tm, d), lambda i: (i, 0)),
                  pl.BlockSpec((1, d), lambda i: (0, 0))],
        out_specs=pl.BlockSpec((tm, d), lambda i: (i, 0)),
        out_shape=jax.ShapeDtypeStruct((m, d), out_dtype),
        compiler_params=_params(1),
        name="rmsnorm",
    )(x2d, g.reshape(1, d))


def _matmul_body(*refs, n_in, has_res):
    a_refs = refs[:n_in]
    w_refs = refs[n_in:2 * n_in]
    res_ref = refs[2 * n_in] if has_res else None
    o_ref = refs[2 * n_in + has_res]
    wbf_refs = refs[2 * n_in + has_res + 1:]

    @pl.when(pl.program_id(1) == 0)
    def _():
        for w_ref, wbf_ref in zip(w_refs, wbf_refs):
            lead = (0,) * (len(w_ref.shape) - 2)
            wbf_ref[...] = w_ref[lead + (slice(None), slice(None))].astype(BF16)

    acc = None
    for a_ref, wbf_ref in zip(a_refs, wbf_refs):
        t = jnp.dot(a_ref[...], wbf_ref[...], preferred_element_type=F32)
        acc = t if acc is None else acc + t
    if has_res:
        acc = acc + res_ref[...]
    o_ref[...] = acc.astype(o_ref.dtype)


def matmul_ws(a_list, w_list, n_cols, *, tn, tm, out_dtype, res=None, name="matmul"):
    m = a_list[0].shape[0]
    n_in = len(a_list)
    in_specs = [pl.BlockSpec((tm, a.shape[1]), lambda n, i: (i, 0)) for a in a_list]
    scratch = []
    for _, blk, imap in w_list:
        in_specs.append(pl.BlockSpec(blk, lambda n, i, imap=imap: imap(n)))
        scratch.append(pltpu.VMEM(blk[-2:], BF16))
    args = list(a_list) + [w for w, _, _ in w_list]
    if res is not None:
        in_specs.append(pl.BlockSpec((tm, tn), lambda n, i: (i, n)))
        args.append(res)
    return pl.pallas_call(
        functools.partial(_matmul_body, n_in=n_in, has_res=res is not None),
        grid=(n_cols // tn, m // tm),
        in_specs=in_specs,
        out_specs=pl.BlockSpec((tm, tn), lambda n, i: (i, n)),
        out_shape=jax.ShapeDtypeStruct((m, n_cols), out_dtype),
        scratch_shapes=scratch,
        compiler_params=_params(2),
        name=name,
    )(*args)


def _conv_silu_body(u_ref, w_ref, b_ref, o_ref, pad_ref, *, t_len, k_conv, rows):
    halo = 8
    ch = u_ref.shape[-1]
    pad_ref[0:halo, :] = jnp.zeros((halo, ch), F32)
    pad_ref[t_len + halo:t_len + 2 * halo, :] = jnp.zeros((halo, ch), F32)
    pad_ref[halo:t_len + halo, :] = u_ref[0]
    w = w_ref[0]
    b = b_ref[0]
    first = halo - k_conv // 2

    def body(i, carry):
        r0 = pl.multiple_of(i * rows, rows)
        win = pad_ref[pl.ds(r0, rows + 2 * halo), :]
        n_win = rows + 2 * halo
        acc = b
        for k in range(k_conv):
            off = first + k
            tap = win if off % 8 == 0 else pltpu.roll(win, n_win - off % 8, 0)
            lo = off - off % 8
            acc = acc + w[k:k + 1, :] * tap[lo:lo + rows]
        o_ref[0, pl.ds(r0, rows), :] = acc * jax.nn.sigmoid(acc)
        return carry

    lax.fori_loop(0, t_len // rows, body, 0)


def conv_silu(proj3, conv_w, conv_b, layer, col0, n_ch, tc=512, rows=128):
    b_sz, t_len, _ = proj3.shape
    k_conv = conv_w.shape[1]
    c0 = col0 // tc
    return pl.pallas_call(
        functools.partial(_conv_silu_body, t_len=t_len, k_conv=k_conv, rows=rows),
        grid=(b_sz, n_ch // tc),
        in_specs=[pl.BlockSpec((1, t_len, tc), lambda b, j: (b, 0, c0 + j)),
                  pl.BlockSpec((1, k_conv, tc), lambda b, j: (layer, 0, j)),
                  pl.BlockSpec((1, 1, tc), lambda b, j: (layer, 0, j))],
        out_specs=pl.BlockSpec((1, t_len, tc), lambda b, j: (b, 0, j)),
        out_shape=jax.ShapeDtypeStruct((b_sz, t_len, n_ch), F32),
        scratch_shapes=[pltpu.VMEM((t_len + 16, tc), F32)],
        compiler_params=_params(2),
        name="ssd_conv_silu",
    )(proj3, conv_w, conv_b.reshape(conv_b.shape[0], 1, conv_b.shape[1]))


def _softplus(x):
    return jnp.maximum(x, 0.0) + jnp.log1p(jnp.exp(-jnp.abs(x)))


def _ssd_body(*refs, chunk, n_heads, head_dim, n_groups, backward):
    if backward:
        (xs_ref, b_ref, c_ref, dt_ref, bias_ref, alog_ref, yf_ref, z_ref, dskip_ref, gnorm_ref,
         y_ref, state_ref) = refs
    else:
        xs_ref, b_ref, c_ref, dt_ref, bias_ref, alog_ref, y_ref, state_ref = refs
    L = chunk
    hpg = n_heads // n_groups
    gw = hpg * head_dim

    @pl.when(pl.program_id(1) == 0)
    def _():
        state_ref[...] = jnp.zeros_like(state_ref)

    dt_t = _softplus(dt_ref[0].T[0:n_heads, :] + bias_ref[0])
    da_t = dt_t * (-jnp.exp(alog_ref[0]))
    row = lax.broadcasted_iota(I32, (L, L), 0)
    col = lax.broadcasted_iota(I32, (L, L), 1)
    keep = row <= col if backward else row >= col
    keep_t = col <= row if backward else col >= row
    x_t = jnp.dot(da_t, keep_t.astype(F32), precision=HIGHEST, preferred_element_type=F32)
    edge = 0 if backward else L - 1
    w_t = dt_t * jnp.exp(x_t[:, edge:edge + 1] - x_t)
    x_c = jnp.concatenate([x_t, jnp.zeros((LANES - n_heads, L), F32)], axis=0).T
    e_tot = jnp.exp(x_c[edge:edge + 1, :])

    lo = lax.broadcasted_iota(I32, (L, LANES), 1) < head_dim
    lo_row = lo[0:1, :]

    for g in range(n_groups):
        b_gt = b_ref[0, :, g * SSD_STATE:(g + 1) * SSD_STATE].T
        c_g = c_ref[0, :, g * SSD_STATE:(g + 1) * SSD_STATE].astype(BF16)
        cb = jnp.dot(c_g, b_gt.astype(BF16), preferred_element_type=F32)
        y_off = jnp.dot(c_g, state_ref[g].astype(BF16), preferred_element_type=F32)
        y_pairs = []
        for pp in range(hpg // 2):
            h0 = g * hpg + 2 * pp
            h1 = h0 + 1
            c0 = h0 * head_dim
            lc = 2 * pp * head_dim
            xs_pair = xs_ref[0, :, c0:c0 + LANES]
            rhs = jnp.concatenate([jnp.where(lo, xs_pair, 0.0).astype(BF16),
                                   jnp.where(lo, 0.0, xs_pair).astype(BF16)], axis=0)
            xc0 = jnp.broadcast_to(x_c[:, h0:h0 + 1], (L, L))
            xc1 = jnp.broadcast_to(x_c[:, h1:h1 + 1], (L, L))
            m0 = cb * jnp.exp(jnp.where(keep, xc0 - x_t[h0:h0 + 1, :], -jnp.inf)) * dt_t[h0:h0 + 1, :]
            m1 = cb * jnp.exp(jnp.where(keep, xc1 - x_t[h1:h1 + 1, :], -jnp.inf)) * dt_t[h1:h1 + 1, :]
            y_diag = jnp.dot(jnp.concatenate([m0.astype(BF16), m1.astype(BF16)], axis=1), rhs,
                             preferred_element_type=F32)
            e_out = jnp.where(lo, jnp.exp(xc0), jnp.exp(xc1))
            y_pair = y_diag + y_off[:, lc:lc + LANES] * e_out
            if backward:
                y_pairs.append(y_pair)
            else:
                y_ref[0, :, c0:c0 + LANES] = y_pair
            lhs_b = jnp.concatenate([(b_gt * w_t[h0:h0 + 1, :]).astype(BF16),
                                     (b_gt * w_t[h1:h1 + 1, :]).astype(BF16)], axis=1)
            dec = jnp.where(lo_row, e_tot[:, h0:h0 + 1], e_tot[:, h1:h1 + 1])
            state_ref[g, :, lc:lc + LANES] = (state_ref[g, :, lc:lc + LANES] * dec
                                              + jnp.dot(lhs_b, rhs, preferred_element_type=F32))
        if backward:
            cols = slice(g * gw, (g + 1) * gw)
            y = (yf_ref[0, :, cols] + jnp.concatenate(y_pairs, axis=1)
                 + dskip_ref[:, cols] * xs_ref[0, :, cols])
            z = z_ref[0, :, cols]
            y = y * (z * jax.nn.sigmoid(z))
            ms = jnp.mean(y * y, axis=-1, keepdims=True)
            y_ref[0, :, cols] = (y * lax.rsqrt(ms + EPS) * gnorm_ref[:, cols]).astype(y_ref.dtype)


def ssd_scan(xbc, dt_raw, dt_bias, a_log, layer, *, n_heads, dt_col0=0, epilogue=None):
    b_sz, t_len, _ = xbc.shape
    d_ssd = n_heads * SSD_HEADDIM
    gn = SSD_GROUPS * SSD_STATE
    L = SSD_CHUNK
    assert L == LANES and 2 * SSD_HEADDIM == LANES and (n_heads // SSD_GROUPS) % 2 == 0
    nc = t_len // L
    xb = d_ssd // gn
    backward = epilogue is not None
    direction = int(backward)

    def chunk_of(c):
        return nc - 1 - c if backward else c

    in_specs = [pl.BlockSpec((1, L, d_ssd), lambda b, c: (b, chunk_of(c), 0)),
                pl.BlockSpec((1, L, gn), lambda b, c: (b, chunk_of(c), xb)),
                pl.BlockSpec((1, L, gn), lambda b, c: (b, chunk_of(c), xb + 1)),
                pl.BlockSpec((1, L, LANES), lambda b, c: (b, chunk_of(c), dt_col0 + direction)),
                pl.BlockSpec((1, n_heads, L), lambda b, c: (layer * 2 + direction, 0, 0)),
                pl.BlockSpec((1, n_heads, L), lambda b, c: (layer * 2 + direction, 0, 0))]
    args = [xbc, xbc, xbc, dt_raw, dt_bias, a_log]
    if backward:
        in_specs += [pl.BlockSpec((1, L, d_ssd), lambda b, c: (b, chunk_of(c), 0)),
                     pl.BlockSpec((1, L, d_ssd), lambda b, c: (b, chunk_of(c), 0)),
                     pl.BlockSpec((1, d_ssd), lambda b, c: (0, 0)),
                     pl.BlockSpec((1, d_ssd), lambda b, c: (0, 0))]
        args += list(epilogue)
    return pl.pallas_call(
        functools.partial(_ssd_body, chunk=L, n_heads=n_heads, head_dim=SSD_HEADDIM,
                          n_groups=SSD_GROUPS, backward=backward),
        grid=(b_sz, nc),
        in_specs=in_specs,
        out_specs=pl.BlockSpec((1, L, d_ssd), lambda b, c: (b, chunk_of(c), 0)),
        out_shape=jax.ShapeDtypeStruct((b_sz, t_len, d_ssd), BF16 if backward else F32),
        scratch_shapes=[pltpu.VMEM((SSD_GROUPS, SSD_STATE, d_ssd // SSD_GROUPS), F32)],
        compiler_params=_params(2),
        name="ssd_scan_bwd_gate" if backward else "ssd_scan_fwd",
    )(*args)


def _softmax_rows(s):
    m = jnp.max(s, axis=-1, keepdims=True)
    e = jnp.exp(s - m)
    return e / jnp.sum(e, axis=-1, keepdims=True)


def _mem_attn_body(q_ref, kv_ref, o_ref, *, n_heads, dh):
    scale = dh ** -0.5
    for h in range(n_heads):
        q = q_ref[0, :, h * dh:(h + 1) * dh].astype(BF16)
        k = kv_ref[0, :, h * dh:(h + 1) * dh]
        v = kv_ref[0, :, (n_heads + h) * dh:(n_heads + h + 1) * dh]
        p = _softmax_rows(_nt_dot(q, k) * scale).astype(BF16)
        o_ref[0, :, h * dh:(h + 1) * dh] = jnp.dot(p, v, preferred_element_type=F32).astype(o_ref.dtype)


def mem_attention(q3, q_col_block, kv3, tm=512):
    b_sz, t_len, _ = q3.shape
    d_xa = XA_HEADS * XA_DH
    mem_len = kv3.shape[1]
    return pl.pallas_call(
        functools.partial(_mem_attn_body, n_heads=XA_HEADS, dh=XA_DH),
        grid=(b_sz, t_len // tm),
        in_specs=[pl.BlockSpec((1, tm, d_xa), lambda b, i: (b, i, q_col_block)),
                  pl.BlockSpec((1, mem_len, 2 * d_xa), lambda b, i: (b, 0, 0))],
        out_specs=pl.BlockSpec((1, tm, d_xa), lambda b, i: (b, i, 0)),
        out_shape=jax.ShapeDtypeStruct((b_sz, t_len, d_xa), BF16),
        compiler_params=_params(2),
        name="mem_attention",
    )(q3, kv3)


NA_QROWS = 4
NA_KROWS = 12
NEG_MASK = -1e30


def _na_bias_body(rpb_ref, o_ref, *, n_dr, width, kc):
    c_idx = lax.broadcasted_iota(I32, (width, LANES), 0)
    lane = lax.broadcasted_iota(I32, (width, LANES), 1)
    first = lane < width
    k_idx = jnp.where(first, lane, lane - width)
    w_start = jnp.clip(c_idx - kc // 2, 0, width - kc)
    valid = (k_idx >= w_start) & (k_idx < w_start + kc)
    base = LANES - (kc - 1)
    neg = jnp.full((width, LANES), NEG_MASK, F32)
    lo_half, hi_half = [], []
    for d in range(n_dr):
        x = jnp.broadcast_to(rpb_ref[0, d:d + 1, :], (width, LANES))
        lo_half.append(jnp.where(valid & first, pltpu.roll(x, base, 1, stride=1, stride_axis=0), neg))
        hi_half.append(jnp.where(valid & jnp.logical_not(first),
                                 pltpu.roll(x, (base + width) % LANES, 1, stride=1, stride_axis=0), neg))
    for d in range(n_dr - 1):
        o_ref[0, d] = jnp.where(first, lo_half[d], hi_half[d + 1])
    for d in range(n_dr):
        o_ref[0, n_dr - 1 + d] = lo_half[d]
        o_ref[0, 2 * n_dr - 1 + d] = hi_half[d]
    o_ref[0, 3 * n_dr - 1] = neg


def na_bias_table(rpb_padded):
    n_heads, n_dr, _ = rpb_padded.shape
    n_ent = 3 * n_dr
    return pl.pallas_call(
        functools.partial(_na_bias_body, n_dr=n_dr, width=GRID_W, kc=NA_KC),
        grid=(n_heads,),
        in_specs=[pl.BlockSpec((1, n_dr, LANES), lambda h: (h, 0, 0))],
        out_specs=pl.BlockSpec((1, n_ent, GRID_W, 2 * GRID_W), lambda h: (h, 0, 0, 0)),
        out_shape=jax.ShapeDtypeStruct((n_heads, n_ent, GRID_W, 2 * GRID_W), F32),
        compiler_params=_params(1),
        name="na_bias_table",
    )(rpb_padded)


def _na_table_entry(r, ka, rows, kr, n_dr):
    rs = min(max(r - kr // 2, 0), rows - kr)
    in_a = rs <= ka < rs + kr
    in_b = rs <= ka + 1 < rs + kr
    d_a = ka - r + (NA_KR - 1)
    if in_a and in_b:
        return d_a
    if in_a:
        return n_dr - 1 + d_a
    if in_b:
        return 2 * n_dr - 1 + d_a + 1
    return 3 * n_dr - 1


def _na_body(q_ref, k_ref, v_ref, tb_ref, o_ref, *, rows, width, kr, dh):
    scale = dh ** -0.5
    n_dr = 2 * NA_KR - 1
    nq = NA_QROWS * width
    nk = NA_KROWS * width
    for blk in range(rows // NA_QROWS):
        r0 = blk * NA_QROWS
        k_row0 = min(max(r0 - kr // 2, 0), rows - NA_KROWS)
        q = q_ref[0, r0 * width:r0 * width + nq, :]
        kw = k_ref[0, k_row0 * width:k_row0 * width + nk, :]
        vw = v_ref[0, k_row0 * width:k_row0 * width + nk, :]
        s = _nt_dot(q, kw) * scale
        s = jnp.concatenate([
            jnp.concatenate([
                s[i * width:(i + 1) * width, j * LANES:(j + 1) * LANES]
                + tb_ref[0, _na_table_entry(r0 + i, k_row0 + 2 * j, rows, kr, n_dr)]
                for j in range(nk // LANES)], axis=1)
            for i in range(NA_QROWS)], axis=0)
        m = jnp.max(s, axis=-1, keepdims=True)
        e = jnp.exp(s - m)
        den = jnp.sum(e, axis=-1, keepdims=True)
        o = jnp.dot(e.astype(BF16), vw, preferred_element_type=F32) / den
        o_ref[0, r0 * width:r0 * width + nq, :] = o.astype(o_ref.dtype)


def na_attention(proj3, table, n_heads):
    b_sz, t_len, _ = proj3.shape
    rows = t_len // GRID_W
    kr = min(NA_KR, rows)
    assert kr == NA_KR and rows % NA_QROWS == 0 and rows >= NA_KROWS and (rows - NA_KROWS) % 2 == 0
    assert NA_KROWS >= kr + NA_QROWS - 1 and 2 * GRID_W == LANES
    n_dr2 = table.shape[1]
    return pl.pallas_call(
        functools.partial(_na_body, rows=rows, width=GRID_W, kr=kr, dh=NA_DH),
        grid=(b_sz, n_heads),
        in_specs=[pl.BlockSpec((1, t_len, NA_DH), lambda b, h: (b, 0, h)),
                  pl.BlockSpec((1, t_len, NA_DH), lambda b, h: (b, 0, n_heads + h)),
                  pl.BlockSpec((1, t_len, NA_DH), lambda b, h: (b, 0, 2 * n_heads + h)),
                  pl.BlockSpec((1, n_dr2, GRID_W, 2 * GRID_W), lambda b, h: (h, 0, 0, 0))],
        out_specs=pl.BlockSpec((1, t_len, NA_DH), lambda b, h: (b, 0, h)),
        out_shape=jax.ShapeDtypeStruct((b_sz, t_len, n_heads * NA_DH), BF16),
        compiler_params=_params(2),
        name="na_attention",
    )(proj3, proj3, proj3, table)


def _norm_and_route(x, g_ref, w_ref, h_ref, aff_ref):
    ms = jnp.mean(x * x, axis=-1, keepdims=True)
    hn = x * lax.rsqrt(ms + EPS) * g_ref[...]
    hn_hi = hn.astype(BF16)
    h_ref[...] = hn_hi
    hn_lo = (hn - hn_hi.astype(F32)).astype(BF16)
    w = w_ref[0]
    w_hi = w.astype(BF16)
    w_lo = (w - w_hi.astype(F32)).astype(BF16)
    n_exp = w.shape[1]
    both = jnp.dot(hn_hi, jnp.concatenate([w_hi, w_lo], axis=1), preferred_element_type=F32)
    logits = both[:, :n_exp] + (both[:, n_exp:] + jnp.dot(hn_lo, w_hi, preferred_element_type=F32))
    aff_ref[...] = _softmax_rows(logits)


def _out_proj_router_body(a1_ref, a2_ref, w1_ref, w2_ref, res_ref, g_ref, wr_ref,
                          x_ref, h_ref, aff_ref, wbf1_ref, wbf2_ref, *, sub_rows):
    @pl.when(pl.program_id(0) == 0)
    def _():
        wbf1_ref[...] = w1_ref[0].astype(BF16)
        wbf2_ref[...] = w2_ref[0].astype(BF16)

    for r in range(0, a1_ref.shape[0], sub_rows):
        rows = slice(r, r + sub_rows)
        acc = (jnp.dot(a1_ref[rows, :], wbf1_ref[...], preferred_element_type=F32)
               + jnp.dot(a2_ref[rows, :], wbf2_ref[...], preferred_element_type=F32))
        x = acc + res_ref[rows, :]
        x_ref[rows, :] = x
        _norm_and_route(x, g_ref, wr_ref, h_ref.at[rows, :], aff_ref.at[rows, :])


def out_proj_router(a1, a2, w_out, layer_w, res, norm_g, w_router, layer, tm=512, sub_rows=256):
    m, k1 = a1.shape
    k2 = a2.shape[1]
    d = w_out.shape[-1]
    n_exp = w_router.shape[-1]
    assert k1 % k2 == 0
    once = pl.Buffered(1)
    return pl.pallas_call(
        functools.partial(_out_proj_router_body, sub_rows=sub_rows),
        grid=(m // tm,),
        in_specs=[pl.BlockSpec((tm, k1), lambda i: (i, 0)),
                  pl.BlockSpec((tm, k2), lambda i: (i, 0)),
                  pl.BlockSpec((1, k1, d), lambda i: (layer_w, 0, 0), pipeline_mode=once),
                  pl.BlockSpec((1, k2, d), lambda i: (layer_w, k1 // k2, 0), pipeline_mode=once),
                  pl.BlockSpec((tm, d), lambda i: (i, 0)),
                  pl.BlockSpec((1, d), lambda i: (0, 0)),
                  pl.BlockSpec((1, d, n_exp), lambda i: (layer, 0, 0))],
        out_specs=[pl.BlockSpec((tm, d), lambda i: (i, 0)),
                   pl.BlockSpec((tm, d), lambda i: (i, 0)),
                   pl.BlockSpec((tm, n_exp), lambda i: (i, 0))],
        out_shape=[jax.ShapeDtypeStruct((m, d), F32),
                   jax.ShapeDtypeStruct((m, d), BF16),
                   jax.ShapeDtypeStruct((m, n_exp), F32)],
        scratch_shapes=[pltpu.VMEM((k1, d), BF16), pltpu.VMEM((k2, d), BF16)],
        compiler_params=_params(1),
        name="out_proj_router",
    )(a1, a2, w_out, w_out, res, norm_g.reshape(1, d), w_router)


TOKEN_BLK = 128
SLOT_WIN = 64
SLOT_ALIGN_LOG2 = 4


def _slot_window(c0, c1, cap):
    st = jnp.minimum(lax.shift_left(lax.shift_right_logical(c0, SLOT_ALIGN_LOG2), SLOT_ALIGN_LOG2),
                     cap - SLOT_WIN)
    return st, c1 <= st + SLOT_WIN


def _select_body(aff_ref, enc_ref, encrow_ref, affrow_ref, cnt_ref, *, cap):
    a = aff_ref[0]
    t_len, n_exp = a.shape
    zero = jnp.zeros((1, n_exp), I32)

    def count(pred):
        return jnp.sum(pred.astype(I32), axis=0, keepdims=True)

    def value_bit(i, prefix):
        cand = prefix | jnp.left_shift(jnp.int32(1), 30 - i)
        return jnp.where(count(a >= lax.bitcast_convert_type(cand, F32)) >= cap, cand, prefix)

    thresh = lax.bitcast_convert_type(lax.fori_loop(0, 31, value_bit, zero), F32)
    above = a > thresh
    tied = a == thresh
    need = cap - count(above)
    idx = lax.broadcasted_iota(I32, (t_len, n_exp), 0)
    n_idx_bits = (t_len - 1).bit_length()

    def index_bit(i, last):
        cand = last | jnp.left_shift(jnp.int32(1), n_idx_bits - 1 - i)
        return jnp.where(count(tied & (idx < cand)) < need, cand, last)

    last = lax.fori_loop(0, n_idx_bits, index_bit, zero)
    mask = above | (tied & (idx <= last))
    mask_bf = mask.astype(BF16)

    n_rows = cnt_ref.shape[1]
    bound = lax.broadcasted_iota(I32, (n_rows, t_len), 0) * TOKEN_BLK
    tok = lax.broadcasted_iota(I32, (n_rows, t_len), 1)
    cnt = jnp.dot((tok < bound).astype(BF16), mask_bf, preferred_element_type=F32)
    cnt_ref[0] = cnt.astype(I32)
    earlier = (lax.broadcasted_iota(I32, (TOKEN_BLK, TOKEN_BLK), 0)
               > lax.broadcasted_iota(I32, (TOKEN_BLK, TOKEN_BLK), 1)).astype(BF16)
    for i in range(t_len // TOKEN_BLK):
        rows = slice(i * TOKEN_BLK, (i + 1) * TOKEN_BLK)
        before = jnp.dot(earlier, mask_bf[rows], preferred_element_type=F32) + cnt[i:i + 1, :]
        enc_ref[0, rows, :] = jnp.where(mask[rows], before + 1.0, 0.0).astype(I32)

    eye = (lax.broadcasted_iota(I32, (n_exp, n_exp), 0)
           == lax.broadcasted_iota(I32, (n_exp, n_exp), 1)).astype(BF16)
    enc_bf = enc_ref[0].astype(F32).astype(BF16)
    encrow_ref[0] = _nt_dot(eye, enc_bf).astype(I32)

    a_hi = a.astype(BF16)
    rest = a - a_hi.astype(F32)
    a_mid = rest.astype(BF16)
    a_lo = (rest - a_mid.astype(F32)).astype(BF16)
    affrow_ref[0] = _nt_dot(eye, a_hi) + (_nt_dot(eye, a_mid) + _nt_dot(eye, a_lo))


def moe_select(aff3, cap):
    b_sz, t_len, n_exp = aff3.shape
    assert cap <= 256, "slot codes must stay exactly representable in bf16"
    n_rows = -(-(t_len // TOKEN_BLK + 1) // 8) * 8
    return pl.pallas_call(
        functools.partial(_select_body, cap=cap),
        grid=(b_sz,),
        in_specs=[pl.BlockSpec((1, t_len, n_exp), lambda b: (b, 0, 0))],
        out_specs=[pl.BlockSpec((1, t_len, n_exp), lambda b: (b, 0, 0)),
                   pl.BlockSpec((1, n_exp, t_len), lambda b: (b, 0, 0)),
                   pl.BlockSpec((1, n_exp, t_len), lambda b: (b, 0, 0)),
                   pl.BlockSpec((1, n_rows, n_exp), lambda b: (b, 0, 0))],
        out_shape=[jax.ShapeDtypeStruct((b_sz, t_len, n_exp), I32),
                   jax.ShapeDtypeStruct((b_sz, n_exp, t_len), I32),
                   jax.ShapeDtypeStruct((b_sz, n_exp, t_len), F32),
                   jax.ShapeDtypeStruct((b_sz, n_rows, n_exp), I32)],
        compiler_params=_params(1),
        name="moe_select",
    )(aff3)


GATHER_GROUP = 4


def _gather_body(cnt_ref, h_ref, encrow_ref, affrow_ref, xs_ref, g_ref, acc_ref, gacc_ref, *, cap, kb):
    b = pl.program_id(0)
    e0 = pl.program_id(1) * GATHER_GROUP
    t_len = h_ref.shape[1]
    per = kb // TOKEN_BLK
    codes = [encrow_ref[0, pl.ds(e0 + i, 1), :] for i in range(GATHER_GROUP)]
    gates = [affrow_ref[0, pl.ds(e0 + i, 1), :] for i in range(GATHER_GROUP)]
    starts, fits = [], None
    for j in range(t_len // kb):
        row = []
        for i in range(GATHER_GROUP):
            st, ok = _slot_window(cnt_ref[b, j * per, e0 + i], cnt_ref[b, (j + 1) * per, e0 + i], cap)
            row.append(pl.multiple_of(st, 1 << SLOT_ALIGN_LOG2))
            fits = ok if fits is None else jnp.logical_and(fits, ok)
        starts.append(row)

    @pl.when(fits)
    def _():
        acc_ref[...] = jnp.zeros_like(acc_ref)
        gacc_ref[...] = jnp.zeros_like(gacc_ref)
        sub = lax.broadcasted_iota(I32, (SLOT_WIN, kb), 0) + 1
        for j, row in enumerate(starts):
            tok = slice(j * kb, (j + 1) * kb)
            hits = [(sub + row[i]) == codes[i][:, tok] for i in range(GATHER_GROUP)]
            picked = jnp.dot(jnp.concatenate([hit.astype(BF16) for hit in hits], axis=0), h_ref[0, tok, :],
                             preferred_element_type=F32)
            for i in range(GATHER_GROUP):
                acc_ref[i, pl.ds(row[i], SLOT_WIN), :] += picked[i * SLOT_WIN:(i + 1) * SLOT_WIN]
                g = jnp.sum(jnp.where(hits[i], gates[i][:, tok], 0.0), axis=1, keepdims=True)
                gacc_ref[i, pl.ds(row[i], SLOT_WIN), :] += jnp.broadcast_to(g, (SLOT_WIN, LANES))
        xs_ref[...] = acc_ref[...].astype(BF16)
        g_ref[...] = gacc_ref[...]

    @pl.when(jnp.logical_not(fits))
    def _():
        slot = lax.broadcasted_iota(I32, (cap, t_len), 0) + 1
        for i in range(GATHER_GROUP):
            hit = slot == codes[i]
            xs_ref[i] = jnp.dot(hit.astype(BF16), h_ref[0], preferred_element_type=F32).astype(BF16)
            g = jnp.sum(jnp.where(hit, gates[i], 0.0), axis=1, keepdims=True)
            g_ref[i] = jnp.broadcast_to(g, (cap, LANES))


def moe_gather(h3, encrow, affrow, cnt, cap, kb=256):
    b_sz, t_len, d = h3.shape
    n_exp = encrow.shape[1]
    assert cap >= SLOT_WIN and t_len % kb == 0 and kb % TOKEN_BLK == 0 and n_exp % GATHER_GROUP == 0
    grp = GATHER_GROUP
    return pl.pallas_call(
        functools.partial(_gather_body, cap=cap, kb=kb),
        grid_spec=pltpu.PrefetchScalarGridSpec(
            num_scalar_prefetch=1,
            grid=(b_sz, n_exp // grp),
            in_specs=[pl.BlockSpec((1, t_len, d), lambda b, q, cnt: (b, 0, 0)),
                      pl.BlockSpec((1, n_exp, t_len), lambda b, q, cnt: (b, 0, 0)),
                      pl.BlockSpec((1, n_exp, t_len), lambda b, q, cnt: (b, 0, 0))],
            out_specs=[pl.BlockSpec((grp, cap, d), lambda b, q, cnt: (q, b, 0)),
                       pl.BlockSpec((grp, cap, LANES), lambda b, q, cnt: (q, b, 0))],
            scratch_shapes=[pltpu.VMEM((grp, cap, d), F32), pltpu.VMEM((grp, cap, LANES), F32)]),
        out_shape=[jax.ShapeDtypeStruct((n_exp, b_sz * cap, d), BF16),
                   jax.ShapeDtypeStruct((n_exp, b_sz * cap, LANES), F32)],
        compiler_params=_params(2),
        name="moe_gather",
    )(cnt, h3, encrow, affrow)


def _ffn_up_body(xs_ref, w1_ref, w3_ref, hid_ref):
    xs = xs_ref[0]
    a = jnp.dot(xs, w1_ref[0, 0].astype(BF16), preferred_element_type=F32)
    b = jnp.dot(xs, w3_ref[0, 0].astype(BF16), preferred_element_type=F32)
    hid_ref[0] = (a * jax.nn.sigmoid(a) * b).astype(hid_ref.dtype)


def _ffn_down_body(hid_ref, w2_ref, g_ref, y_ref):
    y = jnp.dot(hid_ref[0], w2_ref[0, 0].astype(BF16), preferred_element_type=F32)
    gate = jnp.concatenate([g_ref[0]] * (y.shape[1] // LANES), axis=1)
    y_ref[0] = (y * gate).astype(y_ref.dtype)


def moe_ffn(xs, gate, w1, w3, w2, layer, tf=512, tn=2048):
    n_exp, rows, d = xs.shape
    d_exp = w1.shape[-1]
    hid = pl.pallas_call(
        _ffn_up_body,
        grid=(n_exp, d_exp // tf),
        in_specs=[pl.BlockSpec((1, rows, d), lambda e, f: (e, 0, 0)),
                  pl.BlockSpec((1, 1, d, tf), lambda e, f: (layer, e, 0, f)),
                  pl.BlockSpec((1, 1, d, tf), lambda e, f: (layer, e, 0, f))],
        out_specs=pl.BlockSpec((1, rows, tf), lambda e, f: (e, 0, f)),
        out_shape=jax.ShapeDtypeStruct((n_exp, rows, d_exp), BF16),
        compiler_params=_params(2),
        name="moe_ffn_up",
    )(xs, w1, w3)
    return pl.pallas_call(
        _ffn_down_body,
        grid=(n_exp, d // tn),
        in_specs=[pl.BlockSpec((1, rows, d_exp), lambda e, n: (e, 0, 0)),
                  pl.BlockSpec((1, 1, d_exp, tn), lambda e, n: (layer, e, 0, n)),
                  pl.BlockSpec((1, rows, LANES), lambda e, n: (e, 0, 0))],
        out_specs=pl.BlockSpec((1, rows, tn), lambda e, n: (e, 0, n)),
        out_shape=jax.ShapeDtypeStruct((n_exp, rows, d), BF16),
        compiler_params=_params(2),
        name="moe_ffn_down",
    )(hid, w2, gate)


SCATTER_GROUP = 4


SCATTER_SUB = 2


def _scatter_body(cnt_ref, x_ref, y_ref, enc_ref, gn_ref, o_ref, hn_ref, *, cap):
    b = pl.program_id(0)
    j0 = pl.program_id(1) * SCATTER_SUB
    n_exp = enc_ref.shape[-1]
    starts, fits = [], None
    for u in range(SCATTER_SUB):
        row = []
        for e in range(n_exp):
            st, ok = _slot_window(cnt_ref[b, j0 + u, e], cnt_ref[b, j0 + u + 1, e], cap)
            row.append(pl.multiple_of(st, 1 << SLOT_ALIGN_LOG2))
            fits = ok if fits is None else jnp.logical_and(fits, ok)
        starts.append(row)

    def finish(rows, x_new):
        o_ref[0, rows, :] = x_new
        ms = jnp.mean(x_new * x_new, axis=-1, keepdims=True)
        hn_ref[0, rows, :] = (x_new * lax.rsqrt(ms + EPS) * gn_ref[...]).astype(hn_ref.dtype)

    @pl.when(fits)
    def _():
        width = SCATTER_GROUP * SLOT_WIN
        lane = lax.broadcasted_iota(I32, (TOKEN_BLK, width), 1)
        lane_row = lane[0:1, :]
        within = jnp.bitwise_and(lane_row, SLOT_WIN - 1) + 1
        for u in range(SCATTER_SUB):
            rows = slice(u * TOKEN_BLK, (u + 1) * TOKEN_BLK)
            enc = enc_ref[0, rows, :]
            onehots, windows = [], []
            for q in range(n_exp // SCATTER_GROUP):
                es = range(q * SCATTER_GROUP, (q + 1) * SCATTER_GROUP)
                code = enc[:, es[-1]:es[-1] + 1]
                first = starts[u][es[-1]]
                for k in reversed(range(SCATTER_GROUP - 1)):
                    code = jnp.where(lane < (k + 1) * SLOT_WIN, enc[:, es[k]:es[k] + 1], code)
                    first = jnp.where(lane_row < (k + 1) * SLOT_WIN, starts[u][es[k]], first)
                onehots.append((code == first + within).astype(BF16))
                windows += [y_ref[e, pl.ds(starts[u][e], SLOT_WIN), :] for e in es]
            finish(rows, x_ref[0, rows, :] + jnp.dot(
                jnp.concatenate(onehots, axis=1), jnp.concatenate(windows, axis=0),
                preferred_element_type=F32))

    @pl.when(jnp.logical_not(fits))
    def _():
        slot = lax.broadcasted_iota(I32, (TOKEN_BLK, cap), 1) + 1
        for u in range(SCATTER_SUB):
            rows = slice(u * TOKEN_BLK, (u + 1) * TOKEN_BLK)
            enc = enc_ref[0, rows, :]
            acc = x_ref[0, rows, :]
            for e in range(n_exp):
                onehot = (enc[:, e:e + 1] == slot).astype(BF16)
                acc = acc + jnp.dot(onehot, y_ref[e], preferred_element_type=F32)
            finish(rows, acc)


def moe_scatter_add(x3, y, enc, cnt, norm_g, norm_dtype, cap):
    b_sz, t_len, d = x3.shape
    n_exp = enc.shape[-1]
    assert cap >= SLOT_WIN and n_exp % SCATTER_GROUP == 0 and SLOT_WIN & (SLOT_WIN - 1) == 0
    rows = SCATTER_SUB * TOKEN_BLK
    assert t_len % rows == 0
    return pl.pallas_call(
        functools.partial(_scatter_body, cap=cap),
        grid_spec=pltpu.PrefetchScalarGridSpec(
            num_scalar_prefetch=1,
            grid=(b_sz, t_len // rows),
            in_specs=[pl.BlockSpec((1, rows, d), lambda b, j, cnt: (b, j, 0)),
                      pl.BlockSpec((n_exp, cap, d), lambda b, j, cnt: (0, b, 0)),
                      pl.BlockSpec((1, rows, n_exp), lambda b, j, cnt: (b, j, 0)),
                      pl.BlockSpec((1, d), lambda b, j, cnt: (0, 0))],
            out_specs=[pl.BlockSpec((1, rows, d), lambda b, j, cnt: (b, j, 0)),
                       pl.BlockSpec((1, rows, d), lambda b, j, cnt: (b, j, 0))]),
        out_shape=[jax.ShapeDtypeStruct((b_sz, t_len, d), F32),
                   jax.ShapeDtypeStruct((b_sz, t_len, d), norm_dtype)],
        compiler_params=_params(2),
        name="moe_scatter_add",
    )(cnt, x3, y, enc, norm_g.reshape(1, d))


def ec_moe_layer(x3, h2, aff2, w1, w3, w2, layer, next_norm_g, next_norm_dtype):
    b_sz, t_len, d = x3.shape
    n_exp = aff2.shape[-1]
    cap = EC_FACTOR * t_len // n_exp
    enc, encrow, affrow, cnt = moe_select(aff2.reshape(b_sz, t_len, n_exp), cap)
    xs, gate = moe_gather(h2.reshape(b_sz, t_len, d), encrow, affrow, cnt, cap)
    y = moe_ffn(xs, gate, w1, w3, w2, layer)
    return moe_scatter_add(x3, y, enc, cnt, next_norm_g, next_norm_dtype, cap)


def _layer_cols(w, layer, col0, n_cols):
    return lax.slice(w, (layer, 0, col0), (layer + 1, w.shape[1], col0 + n_cols))[0]


def ssd_mixer_layer(h2, kv3, b_sz, t_len, j, ssd_w_in, conv_w, conv_b, dt_bias, a_log, d_skip, gate_g):
    m, d = h2.shape
    n_heads = d_skip.shape[-1]
    d_ssd = n_heads * SSD_HEADDIM
    gn = SSD_GROUPS * SSD_STATE
    conv_dim = d_ssd + 2 * gn
    d_xa = XA_HEADS * XA_DH
    n_main = d_ssd + conv_dim
    tn = 1024
    proj = matmul_ws([h2], [(ssd_w_in, (1, d, tn), lambda n: (j, 0, n))], n_main,
                     tn=tn, tm=512, out_dtype=F32, name="ssd_in_proj")
    w_dt = _layer_cols(ssd_w_in, j, n_main, 2 * n_heads)
    w_dt = jnp.pad(w_dt.reshape(d, 2, n_heads), ((0, 0), (0, 0), (0, LANES - n_heads))).reshape(d, 2 * LANES)
    w_tail = jnp.concatenate([_layer_cols(ssd_w_in, j, n_main + 2 * n_heads, d_xa), w_dt], axis=1)
    n_tail = d_xa + 2 * LANES
    tail = matmul_ws([h2], [(w_tail, (d, n_tail), lambda n: (0, 0))], n_tail,
                     tn=n_tail, tm=512, out_dtype=F32, name="ssd_tail_proj").reshape(b_sz, t_len, n_tail)

    proj3 = proj.reshape(b_sz, t_len, n_main)
    xbc = conv_silu(proj3, conv_w, conv_b, j, d_ssd, conv_dim)
    per_head = (dt_bias.shape[0] * 2, n_heads, SSD_CHUNK)
    bias_t = jnp.broadcast_to(dt_bias.reshape(-1, n_heads, 1), per_head)
    alog_t = jnp.broadcast_to(a_log.reshape(-1, n_heads, 1), per_head)
    dt_col0 = d_xa // LANES
    y_fwd = ssd_scan(xbc, tail, bias_t, alog_t, j, n_heads=n_heads, dt_col0=dt_col0)
    y = ssd_scan(xbc, tail, bias_t, alog_t, j, n_heads=n_heads, dt_col0=dt_col0,
                 epilogue=(y_fwd, proj3, jnp.repeat(d_skip[j], SSD_HEADDIM).reshape(1, d_ssd),
                           gate_g[j].reshape(1, d_ssd))).reshape(m, d_ssd)
    o_x = mem_attention(tail, 0, kv3).reshape(m, d_xa)
    return y, o_x


def na_mixer_layer(h2, kv3, b_sz, t_len, j, na_w_in, na_rpb):
    m, d = h2.shape
    n_heads = na_rpb.shape[1]
    d_na = n_heads * NA_DH
    d_xa = XA_HEADS * XA_DH
    n_in = 3 * d_na + d_xa
    tn = 1024
    proj = matmul_ws([h2], [(na_w_in, (1, d, tn), lambda n: (j, 0, n))], n_in,
                     tn=tn, tm=512, out_dtype=BF16, name="na_in_proj")
    proj3 = proj.reshape(b_sz, t_len, n_in)
    rpb = na_rpb[j]
    table = na_bias_table(jnp.pad(rpb, ((0, 0), (0, 0), (0, LANES - rpb.shape[-1]))))
    o_na = na_attention(proj3, table, n_heads).reshape(m, d_na)
    o_x = mem_attention(proj3, 3 * d_na // d_xa, kv3).reshape(m, d_xa)
    return o_na, o_x


def kernel(x, mem, norm_mix_g, norm_ffn_g, norm_final_g, mem_norm_g, ssd_w_in, ssd_conv_w, ssd_conv_b,
           ssd_dt_bias, ssd_a_log, ssd_d, ssd_gate_norm_g, ssd_w_out, na_w_in, na_rpb, na_w_out, xa_w_kv,
           moe_w_router, moe_w1, moe_w3, moe_w2):
    b_sz, t_len, d = x.shape
    mem_len = mem.shape[1]
    depth = norm_mix_g.shape[0]
    m = b_sz * t_len
    d_kv = xa_w_kv.shape[-1]
    mem_n = rmsnorm_rows(mem.reshape(b_sz * mem_len, d), mem_norm_g, BF16)
    x2 = x.reshape(m, d)
    h2 = rmsnorm_rows(x2, norm_mix_g[0], BF16)
    for i in range(depth):
        j = i // N_MIXERS
        kv3 = matmul_ws([mem_n], [(xa_w_kv, (1, d, d_kv), lambda n, i=i: (i, 0, 0))], d_kv,
                        tn=d_kv, tm=512, out_dtype=BF16, name="xa_kv_proj").reshape(b_sz, mem_len, d_kv)
        if i % N_MIXERS == 0:
            heads, o_x = ssd_mixer_layer(h2, kv3, b_sz, t_len, j, ssd_w_in, ssd_conv_w, ssd_conv_b,
                                         ssd_dt_bias, ssd_a_log, ssd_d, ssd_gate_norm_g)
            w_out = ssd_w_out
        else:
            heads, o_x = na_mixer_layer(h2, kv3, b_sz, t_len, j, na_w_in, na_rpb)
            w_out = na_w_out
        x2, h_ffn, aff2 = out_proj_router(heads, o_x, w_out, j, x2, norm_ffn_g[i], moe_w_router, i)
        last = i == depth - 1
        x3, normed = ec_moe_layer(x2.reshape(b_sz, t_len, d), h_ffn, aff2, moe_w1, moe_w3, moe_w2, i,
                                  norm_final_g if last else norm_mix_g[i + 1], F32 if last else BF16)
        x2, h2 = x3.reshape(m, d), normed.reshape(m, d)
    return normed
```

```python
import functools

import jax
import jax.numpy as jnp
from jax import lax
from jax.experimental import pallas as pl
from jax.experimental.pallas import tpu as pltpu

F32 = jnp.float32
BF16 = jnp.bfloat16
I32 = jnp.int32
HIGHEST = lax.Precision.HIGHEST

EPS = 1e-6
LANES = 128
VMEM_LIMIT_BYTES = 56 * 1024 * 1024

XA_HEADS = 4
XA_DH = 128
SSD_HEADDIM = 64
SSD_GROUPS = 4
SSD_STATE = 128
SSD_CONV = 5
SSD_CHUNK = 128
NA_DH = 128
NA_KR = 8
NA_KC = 16
GRID_W = 64
EC_FACTOR = 2
N_MIXERS = 2


def _params(n_grid_dims):
    return pltpu.CompilerParams(
        dimension_semantics=("arbitrary",) * n_grid_dims,
        vmem_limit_bytes=VMEM_LIMIT_BYTES)


def _nt_dot(a, b):
    return lax.dot_general(a, b, (((1,), (1,)), ((), ())), preferred_element_type=F32)


def _rmsnorm_body(x_ref, g_ref, o_ref):
    x = x_ref[...]
    ms = jnp.mean(x * x, axis=-1, keepdims=True)
    o_ref[...] = (x * lax.rsqrt(ms + EPS) * g_ref[...]).astype(o_ref.dtype)


def rmsnorm_rows(x2d, g, out_dtype, tm=256):
    m, d = x2d.shape
    return pl.pallas_call(
        _rmsnorm_body,
        grid=(m // tm,),
        in_specs=[pl.BlockSpec((tm, d), lambda i: (i, 0)),
                  pl.BlockSpec((1, d), lambda i: (0, 0))],
        out_specs=pl.BlockSpec((tm, d), lambda i: (i, 0)),
        out_shape=jax.ShapeDtypeStruct((m, d), out_dtype),
        compiler_params=_params(1),
        name="rmsnorm",
    )(x2d, g.reshape(1, d))


def _matmul_body(*refs, n_in, has_res):
    a_refs = refs[:n_in]
    w_refs = refs[n_in:2 * n_in]
    res_ref = refs[2 * n_in] if has_res else None
    o_ref = refs[2 * n_in + has_res]
    wbf_refs = refs[2 * n_in + has_res + 1:]

    @pl.when(pl.program_id(1) == 0)
    def _():
        for w_ref, wbf_ref in zip(w_refs, wbf_refs):
            lead = (0,) * (len(w_ref.shape) - 2)
            wbf_ref[...] = w_ref[lead + (slice(None), slice(None))].astype(BF16)

    acc = None
    for a_ref, wbf_ref in zip(a_refs, wbf_refs):
        t = jnp.dot(a_ref[...], wbf_ref[...], preferred_element_type=F32)
        acc = t if acc is None else acc + t
    if has_res:
        acc = acc + res_ref[...]
    o_ref[...] = acc.astype(o_ref.dtype)


def matmul_ws(a_list, w_list, n_cols, *, tn, tm, out_dtype, res=None, name="matmul"):
    m = a_list[0].shape[0]
    n_in = len(a_list)
    in_specs = [pl.BlockSpec((tm, a.shape[1]), lambda n, i: (i, 0)) for a in a_list]
    scratch = []
    for _, blk, imap in w_list:
        in_specs.append(pl.BlockSpec(blk, lambda n, i, imap=imap: imap(n)))
        scratch.append(pltpu.VMEM(blk[-2:], BF16))
    args = list(a_list) + [w for w, _, _ in w_list]
    if res is not None:
        in_specs.append(pl.BlockSpec((tm, tn), lambda n, i: (i, n)))
        args.append(res)
    return pl.pallas_call(
        functools.partial(_matmul_body, n_in=n_in, has_res=res is not None),
        grid=(n_cols // tn, m // tm),
        in_specs=in_specs,
        out_specs=pl.BlockSpec((tm, tn), lambda n, i: (i, n)),
        out_shape=jax.ShapeDtypeStruct((m, n_cols), out_dtype),
        scratch_shapes=scratch,
        compiler_params=_params(2),
        name=name,
    )(*args)


def _conv_silu_body(u_ref, w_ref, b_ref, o_ref, pad_ref, *, t_len, k_conv, rows):
    halo = 8
    ch = u_ref.shape[-1]
    pad_ref[0:halo, :] = jnp.zeros((halo, ch), F32)
    pad_ref[t_len + halo:t_len + 2 * halo, :] = jnp.zeros((halo, ch), F32)
    pad_ref[halo:t_len + halo, :] = u_ref[0]
    w = w_ref[0]
    b = b_ref[0]
    first = halo - k_conv // 2

    def body(i, carry):
        r0 = pl.multiple_of(i * rows, rows)
        win = pad_ref[pl.ds(r0, rows + 2 * halo), :]
        n_win = rows + 2 * halo
        acc = b
        for k in range(k_conv):
            off = first + k
            tap = win if off % 8 == 0 else pltpu.roll(win, n_win - off % 8, 0)
            lo = off - off % 8
            acc = acc + w[k:k + 1, :] * tap[lo:lo + rows]
        o_ref[0, pl.ds(r0, rows), :] = acc * jax.nn.sigmoid(acc)
        return carry

    lax.fori_loop(0, t_len // rows, body, 0)


def conv_silu(proj3, conv_w, conv_b, layer, col0, n_ch, tc=512, rows=128):
    b_sz, t_len, _ = proj3.shape
    k_conv = conv_w.shape[1]
    c0 = col0 // tc
    return pl.pallas_call(
        functools.partial(_conv_silu_body, t_len=t_len, k_conv=k_conv, rows=rows),
        grid=(b_sz, n_ch // tc),
        in_specs=[pl.BlockSpec((1, t_len, tc), lambda b, j: (b, 0, c0 + j)),
                  pl.BlockSpec((1, k_conv, tc), lambda b, j: (layer, 0, j)),
                  pl.BlockSpec((1, 1, tc), lambda b, j: (layer, 0, j))],
        out_specs=pl.BlockSpec((1, t_len, tc), lambda b, j: (b, 0, j)),
        out_shape=jax.ShapeDtypeStruct((b_sz, t_len, n_ch), F32),
        scratch_shapes=[pltpu.VMEM((t_len + 16, tc), F32)],
        compiler_params=_params(2),
        name="ssd_conv_silu",
    )(proj3, conv_w, conv_b.reshape(conv_b.shape[0], 1, conv_b.shape[1]))


def _softplus(x):
    return jnp.maximum(x, 0.0) + jnp.log1p(jnp.exp(-jnp.abs(x)))


def _ssd_body(*refs, chunk, n_heads, head_dim, n_groups, backward):
    if backward:
        (xs_ref, b_ref, c_ref, dt_ref, bias_ref, alog_ref, yf_ref, z_ref, dskip_ref, gnorm_ref,
         y_ref, state_ref) = refs
    else:
        xs_ref, b_ref, c_ref, dt_ref, bias_ref, alog_ref, y_ref, state_ref = refs
    L = chunk
    hpg = n_heads // n_groups
    gw = hpg * head_dim

    @pl.when(pl.program_id(1) == 0)
    def _():
        state_ref[...] = jnp.zeros_like(state_ref)

    dt_t = _softplus(dt_ref[0].T[0:n_heads, :] + bias_ref[0])
    da_t = dt_t * (-jnp.exp(alog_ref[0]))
    row = lax.broadcasted_iota(I32, (L, L), 0)
    col = lax.broadcasted_iota(I32, (L, L), 1)
    keep = row <= col if backward else row >= col
    keep_t = col <= row if backward else col >= row
    x_t = jnp.dot(da_t, keep_t.astype(F32), precision=HIGHEST, preferred_element_type=F32)
    edge = 0 if backward else L - 1
    w_t = dt_t * jnp.exp(x_t[:, edge:edge + 1] - x_t)
    x_c = jnp.concatenate([x_t, jnp.zeros((LANES - n_heads, L), F32)], axis=0).T
    e_tot = jnp.exp(x_c[edge:edge + 1, :])

    lo = lax.broadcasted_iota(I32, (L, LANES), 1) < head_dim
    lo_row = lo[0:1, :]

    for g in range(n_groups):
        b_gt = b_ref[0, :, g * SSD_STATE:(g + 1) * SSD_STATE].T
        c_g = c_ref[0, :, g * SSD_STATE:(g + 1) * SSD_STATE].astype(BF16)
        cb = jnp.dot(c_g, b_gt.astype(BF16), preferred_element_type=F32)
        y_off = jnp.dot(c_g, state_ref[g].astype(BF16), preferred_element_type=F32)
        y_pairs = []
        for pp in range(hpg // 2):
            h0 = g * hpg + 2 * pp
            h1 = h0 + 1
            c0 = h0 * head_dim
            lc = 2 * pp * head_dim
            xs_pair = xs_ref[0, :, c0:c0 + LANES]
            rhs = jnp.concatenate([jnp.where(lo, xs_pair, 0.0).astype(BF16),
                                   jnp.where(lo, 0.0, xs_pair).astype(BF16)], axis=0)
            xc0 = jnp.broadcast_to(x_c[:, h0:h0 + 1], (L, L))
            xc1 = jnp.broadcast_to(x_c[:, h1:h1 + 1], (L, L))
            m0 = cb * jnp.exp(jnp.where(keep, xc0 - x_t[h0:h0 + 1, :], -jnp.inf)) * dt_t[h0:h0 + 1, :]
            m1 = cb * jnp.exp(jnp.where(keep, xc1 - x_t[h1:h1 + 1, :], -jnp.inf)) * dt_t[h1:h1 + 1, :]
            y_diag = jnp.dot(jnp.concatenate([m0.astype(BF16), m1.astype(BF16)], axis=1), rhs,
                             preferred_element_type=F32)
            e_out = jnp.where(lo, jnp.exp(xc0), jnp.exp(xc1))
            y_pair = y_diag + y_off[:, lc:lc + LANES] * e_out
            if backward:
                y_pairs.append(y_pair)
            else:
                y_ref[0, :, c0:c0 + LANES] = y_pair
            lhs_b = jnp.concatenate([(b_gt * w_t[h0:h0 + 1, :]).astype(BF16),
                                     (b_gt * w_t[h1:h1 + 1, :]).astype(BF16)], axis=1)
            dec = jnp.where(lo_row, e_tot[:, h0:h0 + 1], e_tot[:, h1:h1 + 1])
            state_ref[g, :, lc:lc + LANES] = (state_ref[g, :, lc:lc + LANES] * dec
                                              + jnp.dot(lhs_b, rhs, preferred_element_type=F32))
        if backward:
            cols = slice(g * gw, (g + 1) * gw)
            y = (yf_ref[0, :, cols] + jnp.concatenate(y_pairs, axis=1)
                 + dskip_ref[:, cols] * xs_ref[0, :, cols])
            z = z_ref[0, :, cols]
            y = y * (z * jax.nn.sigmoid(z))
            ms = jnp.mean(y * y, axis=-1, keepdims=True)
            y_ref[0, :, cols] = (y * lax.rsqrt(ms + EPS) * gnorm_ref[:, cols]).astype(y_ref.dtype)


def ssd_scan(xbc, dt_raw, dt_bias, a_log, layer, *, n_heads, dt_col0=0, epilogue=None):
    b_sz, t_len, _ = xbc.shape
    d_ssd = n_heads * SSD_HEADDIM
    gn = SSD_GROUPS * SSD_STATE
    L = SSD_CHUNK
    assert L == LANES and 2 * SSD_HEADDIM == LANES and (n_heads // SSD_GROUPS) % 2 == 0
    nc = t_len // L
    xb = d_ssd // gn
    backward = epilogue is not None
    direction = int(backward)

    def chunk_of(c):
        return nc - 1 - c if backward else c

    in_specs = [pl.BlockSpec((1, L, d_ssd), lambda b, c: (b, chunk_of(c), 0)),
                pl.BlockSpec((1, L, gn), lambda b, c: (b, chunk_of(c), xb)),
                pl.BlockSpec((1, L, gn), lambda b, c: (b, chunk_of(c), xb + 1)),
                pl.BlockSpec((1, L, LANES), lambda b, c: (b, chunk_of(c), dt_col0 + direction)),
                pl.BlockSpec((1, n_heads, L), lambda b, c: (layer * 2 + direction, 0, 0)),
                pl.BlockSpec((1, n_heads, L), lambda b, c: (layer * 2 + direction, 0, 0))]
    args = [xbc, xbc, xbc, dt_raw, dt_bias, a_log]
    if backward:
        in_specs += [pl.BlockSpec((1, L, d_ssd), lambda b, c: (b, chunk_of(c), 0)),
                     pl.BlockSpec((1, L, d_ssd), lambda b, c: (b, chunk_of(c), 0)),
                     pl.BlockSpec((1, d_ssd), lambda b, c: (0, 0)),
                     pl.BlockSpec((1, d_ssd), lambda b, c: (0, 0))]
        args += list(epilogue)
    return pl.pallas_call(
        functools.partial(_ssd_body, chunk=L, n_heads=n_heads, head_dim=SSD_HEADDIM,
                          n_groups=SSD_GROUPS, backward=backward),
        grid=(b_sz, nc),
        in_specs=in_specs,
        out_specs=pl.BlockSpec((1, L, d_ssd), lambda b, c: (b, chunk_of(c), 0)),
        out_shape=jax.ShapeDtypeStruct((b_sz, t_len, d_ssd), BF16 if backward else F32),
        scratch_shapes=[pltpu.VMEM((SSD_GROUPS, SSD_STATE, d_ssd // SSD_GROUPS), F32)],
        compiler_params=_params(2),
        name="ssd_scan_bwd_gate" if backward else "ssd_scan_fwd",
    )(*args)


def _softmax_rows(s):
    m = jnp.max(s, axis=-1, keepdims=True)
    e = jnp.exp(s - m)
    return e / jnp.sum(e, axis=-1, keepdims=True)


def _mem_attn_body(q_ref, kv_ref, o_ref, *, n_heads, dh):
    scale = dh ** -0.5
    for h in range(n_heads):
        q = q_ref[0, :, h * dh:(h + 1) * dh].astype(BF16)
        k = kv_ref[0, :, h * dh:(h + 1) * dh]
        v = kv_ref[0, :, (n_heads + h) * dh:(n_heads + h + 1) * dh]
        p = _softmax_rows(_nt_dot(q, k) * scale).astype(BF16)
        o_ref[0, :, h * dh:(h + 1) * dh] = jnp.dot(p, v, preferred_element_type=F32).astype(o_ref.dtype)


def mem_attention(q3, q_col_block, kv3, tm=512):
    b_sz, t_len, _ = q3.shape
    d_xa = XA_HEADS * XA_DH
    mem_len = kv3.shape[1]
    return pl.pallas_call(
        functools.partial(_mem_attn_body, n_heads=XA_HEADS, dh=XA_DH),
        grid=(b_sz, t_len // tm),
        in_specs=[pl.BlockSpec((1, tm, d_xa), lambda b, i: (b, i, q_col_block)),
                  pl.BlockSpec((1, mem_len, 2 * d_xa), lambda b, i: (b, 0, 0))],
        out_specs=pl.BlockSpec((1, tm, d_xa), lambda b, i: (b, i, 0)),
        out_shape=jax.ShapeDtypeStruct((b_sz, t_len, d_xa), BF16),
        compiler_params=_params(2),
        name="mem_attention",
    )(q3, kv3)


NA_QROWS = 4
NA_KROWS = 12
NEG_MASK = -1e30


def _na_bias_body(rpb_ref, o_ref, *, n_dr, width, kc):
    c_idx = lax.broadcasted_iota(I32, (width, LANES), 0)
    lane = lax.broadcasted_iota(I32, (width, LANES), 1)
    first = lane < width
    k_idx = jnp.where(first, lane, lane - width)
    w_start = jnp.clip(c_idx - kc // 2, 0, width - kc)
    valid = (k_idx >= w_start) & (k_idx < w_start + kc)
    base = LANES - (kc - 1)
    neg = jnp.full((width, LANES), NEG_MASK, F32)
    lo_half, hi_half = [], []
    for d in range(n_dr):
        x = jnp.broadcast_to(rpb_ref[0, d:d + 1, :], (width, LANES))
        lo_half.append(jnp.where(valid & first, pltpu.roll(x, base, 1, stride=1, stride_axis=0), neg))
        hi_half.append(jnp.where(valid & jnp.logical_not(first),
                                 pltpu.roll(x, (base + width) % LANES, 1, stride=1, stride_axis=0), neg))
    for d in range(n_dr - 1):
        o_ref[0, d] = jnp.where(first, lo_half[d], hi_half[d + 1])
    for d in range(n_dr):
        o_ref[0, n_dr - 1 + d] = lo_half[d]
        o_ref[0, 2 * n_dr - 1 + d] = hi_half[d]
    o_ref[0, 3 * n_dr - 1] = neg


def na_bias_table(rpb_padded):
    n_heads, n_dr, _ = rpb_padded.shape
    n_ent = 3 * n_dr
    return pl.pallas_call(
        functools.partial(_na_bias_body, n_dr=n_dr, width=GRID_W, kc=NA_KC),
        grid=(n_heads,),
        in_specs=[pl.BlockSpec((1, n_dr, LANES), lambda h: (h, 0, 0))],
        out_specs=pl.BlockSpec((1, n_ent, GRID_W, 2 * GRID_W), lambda h: (h, 0, 0, 0)),
        out_shape=jax.ShapeDtypeStruct((n_heads, n_ent, GRID_W, 2 * GRID_W), F32),
        compiler_params=_params(1),
        name="na_bias_table",
    )(rpb_padded)


def _na_table_entry(r, ka, rows, kr, n_dr):
    rs = min(max(r - kr // 2, 0), rows - kr)
    in_a = rs <= ka < rs + kr
    in_b = rs <= ka + 1 < rs + kr
    d_a = ka - r + (NA_KR - 1)
    if in_a and in_b:
        return d_a
    if in_a:
        return n_dr - 1 + d_a
    if in_b:
        return 2 * n_dr - 1 + d_a + 1
    return 3 * n_dr - 1


def _na_body(q_ref, k_ref, v_ref, tb_ref, o_ref, *, rows, width, kr, dh):
    scale = dh ** -0.5
    n_dr = 2 * NA_KR - 1
    nq = NA_QROWS * width
    nk = NA_KROWS * width
    for blk in range(rows // NA_QROWS):
        r0 = blk * NA_QROWS
        k_row0 = min(max(r0 - kr // 2, 0), rows - NA_KROWS)
        q = q_ref[0, r0 * width:r0 * width + nq, :]
        kw = k_ref[0, k_row0 * width:k_row0 * width + nk, :]
        vw = v_ref[0, k_row0 * width:k_row0 * width + nk, :]
        s = _nt_dot(q, kw)
        e_rows, dens = [], []
        for i in range(NA_QROWS):
            entries = [_na_table_entry(r0 + i, k_row0 + 2 * j, rows, kr, n_dr) for j in range(nk // LANES)]
            live = [j for j, ent in enumerate(entries) if ent != 3 * n_dr - 1]
            tiles = {j: s[i * width:(i + 1) * width, j * LANES:(j + 1) * LANES] * scale + tb_ref[0, entries[j]]
                     for j in live}
            m = jnp.max(functools.reduce(jnp.maximum, tiles.values()), axis=-1, keepdims=True)
            e_tiles = {j: jnp.exp(t - m) for j, t in tiles.items()}
            dens.append(jnp.sum(functools.reduce(jnp.add, e_tiles.values()), axis=-1, keepdims=True))
            e_rows.append(jnp.concatenate(
                [e_tiles[j].astype(BF16) if j in e_tiles else jnp.zeros((width, LANES), BF16)
                 for j in range(nk // LANES)], axis=1))
        o = (jnp.dot(jnp.concatenate(e_rows, axis=0), vw, preferred_element_type=F32)
             / jnp.concatenate(dens, axis=0))
        o_ref[0, r0 * width:r0 * width + nq, :] = o.astype(o_ref.dtype)


def na_attention(proj3, table, n_heads):
    b_sz, t_len, _ = proj3.shape
    rows = t_len // GRID_W
    kr = min(NA_KR, rows)
    assert kr == NA_KR and rows % NA_QROWS == 0 and rows >= NA_KROWS and (rows - NA_KROWS) % 2 == 0
    assert NA_KROWS >= kr + NA_QROWS - 1 and 2 * GRID_W == LANES
    n_dr2 = table.shape[1]
    return pl.pallas_call(
        functools.partial(_na_body, rows=rows, width=GRID_W, kr=kr, dh=NA_DH),
        grid=(b_sz, n_heads),
        in_specs=[pl.BlockSpec((1, t_len, NA_DH), lambda b, h: (b, 0, h)),
                  pl.BlockSpec((1, t_len, NA_DH), lambda b, h: (b, 0, n_heads + h)),
                  pl.BlockSpec((1, t_len, NA_DH), lambda b, h: (b, 0, 2 * n_heads + h)),
                  pl.BlockSpec((1, n_dr2, GRID_W, 2 * GRID_W), lambda b, h: (h, 0, 0, 0))],
        out_specs=pl.BlockSpec((1, t_len, NA_DH), lambda b, h: (b, 0, h)),
        out_shape=jax.ShapeDtypeStruct((b_sz, t_len, n_heads * NA_DH), BF16),
        compiler_params=_params(2),
        name="na_attention",
    )(proj3, proj3, proj3, table)


def _norm_and_route(x, g_ref, w_ref, h_ref, aff_ref):
    ms = jnp.mean(x * x, axis=-1, keepdims=True)
    hn = x * lax.rsqrt(ms + EPS) * g_ref[...]
    hn_hi = hn.astype(BF16)
    h_ref[...] = hn_hi
    hn_lo = (hn - hn_hi.astype(F32)).astype(BF16)
    w = w_ref[0]
    w_hi = w.astype(BF16)
    w_lo = (w - w_hi.astype(F32)).astype(BF16)
    n_exp = w.shape[1]
    both = jnp.dot(hn_hi, jnp.concatenate([w_hi, w_lo], axis=1), preferred_element_type=F32)
    logits = both[:, :n_exp] + (both[:, n_exp:] + jnp.dot(hn_lo, w_hi, preferred_element_type=F32))
    aff_ref[...] = _softmax_rows(logits)


def _out_proj_router_body(a1_ref, a2_ref, w1_ref, w2_ref, res_ref, g_ref, wr_ref,
                          x_ref, h_ref, aff_ref, wbf1_ref, wbf2_ref, *, sub_rows):
    @pl.when(pl.program_id(0) == 0)
    def _():
        wbf1_ref[...] = w1_ref[0].astype(BF16)
        wbf2_ref[...] = w2_ref[0].astype(BF16)

    for r in range(0, a1_ref.shape[0], sub_rows):
        rows = slice(r, r + sub_rows)
        acc = (jnp.dot(a1_ref[rows, :], wbf1_ref[...], preferred_element_type=F32)
               + jnp.dot(a2_ref[rows, :], wbf2_ref[...], preferred_element_type=F32))
        x = acc + res_ref[rows, :]
        x_ref[rows, :] = x
        _norm_and_route(x, g_ref, wr_ref, h_ref.at[rows, :], aff_ref.at[rows, :])


def out_proj_router(a1, a2, w_out, layer_w, res, norm_g, w_router, layer, tm=512, sub_rows=256):
    m, k1 = a1.shape
    k2 = a2.shape[1]
    d = w_out.shape[-1]
    n_exp = w_router.shape[-1]
    assert k1 % k2 == 0
    once = pl.Buffered(1)
    return pl.pallas_call(
        functools.partial(_out_proj_router_body, sub_rows=sub_rows),
        grid=(m // tm,),
        in_specs=[pl.BlockSpec((tm, k1), lambda i: (i, 0)),
                  pl.BlockSpec((tm, k2), lambda i: (i, 0)),
                  pl.BlockSpec((1, k1, d), lambda i: (layer_w, 0, 0), pipeline_mode=once),
                  pl.BlockSpec((1, k2, d), lambda i: (layer_w, k1 // k2, 0), pipeline_mode=once),
                  pl.BlockSpec((tm, d), lambda i: (i, 0)),
                  pl.BlockSpec((1, d), lambda i: (0, 0)),
                  pl.BlockSpec((1, d, n_exp), lambda i: (layer, 0, 0))],
        out_specs=[pl.BlockSpec((tm, d), lambda i: (i, 0)),
                   pl.BlockSpec((tm, d), lambda i: (i, 0)),
                   pl.BlockSpec((tm, n_exp), lambda i: (i, 0))],
        out_shape=[jax.ShapeDtypeStruct((m, d), F32),
                   jax.ShapeDtypeStruct((m, d), BF16),
                   jax.ShapeDtypeStruct((m, n_exp), F32)],
        scratch_shapes=[pltpu.VMEM((k1, d), BF16), pltpu.VMEM((k2, d), BF16)],
        compiler_params=_params(1),
        name="out_proj_router",
    )(a1, a2, w_out, w_out, res, norm_g.reshape(1, d), w_router)


TOKEN_BLK = 128
SLOT_WIN = 64
SLOT_ALIGN_LOG2 = 4


def _slot_window(c0, c1, cap):
    st = jnp.minimum(lax.shift_left(lax.shift_right_logical(c0, SLOT_ALIGN_LOG2), SLOT_ALIGN_LOG2),
                     cap - SLOT_WIN)
    return st, c1 <= st + SLOT_WIN


def _select_body(aff_ref, enc_ref, encrow_ref, affrow_ref, cnt_ref, *, cap):
    a = aff_ref[0]
    t_len, n_exp = a.shape
    zero = jnp.zeros((1, n_exp), I32)

    def count(pred):
        return jnp.sum(pred.astype(I32), axis=0, keepdims=True)

    def value_bit(i, prefix):
        cand = prefix | jnp.left_shift(jnp.int32(1), 30 - i)
        return jnp.where(count(a >= lax.bitcast_convert_type(cand, F32)) >= cap, cand, prefix)

    thresh = lax.bitcast_convert_type(lax.fori_loop(0, 31, value_bit, zero), F32)
    above = a > thresh
    tied = a == thresh
    need = cap - count(above)
    idx = lax.broadcasted_iota(I32, (t_len, n_exp), 0)
    n_idx_bits = (t_len - 1).bit_length()

    def index_bit(i, last):
        cand = last | jnp.left_shift(jnp.int32(1), n_idx_bits - 1 - i)
        return jnp.where(count(tied & (idx < cand)) < need, cand, last)

    last = lax.fori_loop(0, n_idx_bits, index_bit, zero)
    mask = above | (tied & (idx <= last))
    mask_bf = mask.astype(BF16)

    n_rows = cnt_ref.shape[1]
    bound = lax.broadcasted_iota(I32, (n_rows, t_len), 0) * TOKEN_BLK
    tok = lax.broadcasted_iota(I32, (n_rows, t_len), 1)
    cnt = jnp.dot((tok < bound).astype(BF16), mask_bf, preferred_element_type=F32)
    cnt_ref[0] = cnt.astype(I32)
    earlier = (lax.broadcasted_iota(I32, (TOKEN_BLK, TOKEN_BLK), 0)
               > lax.broadcasted_iota(I32, (TOKEN_BLK, TOKEN_BLK), 1)).astype(BF16)
    for i in range(t_len // TOKEN_BLK):
        rows = slice(i * TOKEN_BLK, (i + 1) * TOKEN_BLK)
        before = jnp.dot(earlier, mask_bf[rows], preferred_element_type=F32) + cnt[i:i + 1, :]
        enc_ref[0, rows, :] = jnp.where(mask[rows], before + 1.0, 0.0).astype(I32)

    eye = (lax.broadcasted_iota(I32, (n_exp, n_exp), 0)
           == lax.broadcasted_iota(I32, (n_exp, n_exp), 1)).astype(BF16)
    enc_bf = enc_ref[0].astype(F32).astype(BF16)
    encrow_ref[0] = _nt_dot(eye, enc_bf).astype(I32)

    a_hi = a.astype(BF16)
    rest = a - a_hi.astype(F32)
    a_mid = rest.astype(BF16)
    a_lo = (rest - a_mid.astype(F32)).astype(BF16)
    affrow_ref[0] = _nt_dot(eye, a_hi) + (_nt_dot(eye, a_mid) + _nt_dot(eye, a_lo))


def moe_select(aff3, cap):
    b_sz, t_len, n_exp = aff3.shape
    assert cap <= 256, "slot codes must stay exactly representable in bf16"
    n_rows = -(-(t_len // TOKEN_BLK + 1) // 8) * 8
    return pl.pallas_call(
        functools.partial(_select_body, cap=cap),
        grid=(b_sz,),
        in_specs=[pl.BlockSpec((1, t_len, n_exp), lambda b: (b, 0, 0))],
        out_specs=[pl.BlockSpec((1, t_len, n_exp), lambda b: (b, 0, 0)),
                   pl.BlockSpec((1, n_exp, t_len), lambda b: (b, 0, 0)),
                   pl.BlockSpec((1, n_exp, t_len), lambda b: (b, 0, 0)),
                   pl.BlockSpec((1, n_rows, n_exp), lambda b: (b, 0, 0))],
        out_shape=[jax.ShapeDtypeStruct((b_sz, t_len, n_exp), I32),
                   jax.ShapeDtypeStruct((b_sz, n_exp, t_len), I32),
                   jax.ShapeDtypeStruct((b_sz, n_exp, t_len), F32),
                   jax.ShapeDtypeStruct((b_sz, n_rows, n_exp), I32)],
        compiler_params=_params(1),
        name="moe_select",
    )(aff3)


GATHER_GROUP = 4


def _gather_body(cnt_ref, h_ref, encrow_ref, affrow_ref, xs_ref, g_ref, acc_ref, gacc_ref, *, cap, kb):
    b = pl.program_id(0)
    e0 = pl.program_id(1) * GATHER_GROUP
    t_len = h_ref.shape[1]
    per = kb // TOKEN_BLK
    codes = [encrow_ref[0, pl.ds(e0 + i, 1), :] for i in range(GATHER_GROUP)]
    gates = [affrow_ref[0, pl.ds(e0 + i, 1), :] for i in range(GATHER_GROUP)]
    starts, fits = [], None
    for j in range(t_len // kb):
        row = []
        for i in range(GATHER_GROUP):
            st, ok = _slot_window(cnt_ref[b, j * per, e0 + i], cnt_ref[b, (j + 1) * per, e0 + i], cap)
            row.append(pl.multiple_of(st, 1 << SLOT_ALIGN_LOG2))
            fits = ok if fits is None else jnp.logical_and(fits, ok)
        starts.append(row)

    @pl.when(fits)
    def _():
        acc_ref[...] = jnp.zeros_like(acc_ref)
        gacc_ref[...] = jnp.zeros_like(gacc_ref)
        sub = lax.broadcasted_iota(I32, (SLOT_WIN, kb), 0) + 1
        for j, row in enumerate(starts):
            tok = slice(j * kb, (j + 1) * kb)
            hits = [(sub + row[i]) == codes[i][:, tok] for i in range(GATHER_GROUP)]
            picked = jnp.dot(jnp.concatenate([hit.astype(BF16) for hit in hits], axis=0), h_ref[0, tok, :],
                             preferred_element_type=F32)
            for i in range(GATHER_GROUP):
                acc_ref[i, pl.ds(row[i], SLOT_WIN), :] += picked[i * SLOT_WIN:(i + 1) * SLOT_WIN]
                g = jnp.sum(jnp.where(hits[i], gates[i][:, tok], 0.0), axis=1, keepdims=True)
                gacc_ref[i, pl.ds(row[i], SLOT_WIN), :] += jnp.broadcast_to(g, (SLOT_WIN, LANES))
        xs_ref[...] = acc_ref[...].astype(BF16)
        g_ref[...] = gacc_ref[...]

    @pl.when(jnp.logical_not(fits))
    def _():
        slot = lax.broadcasted_iota(I32, (cap, t_len), 0) + 1
        for i in range(GATHER_GROUP):
            hit = slot == codes[i]
            xs_ref[i] = jnp.dot(hit.astype(BF16), h_ref[0], preferred_element_type=F32).astype(BF16)
            g = jnp.sum(jnp.where(hit, gates[i], 0.0), axis=1, keepdims=True)
            g_ref[i] = jnp.broadcast_to(g, (cap, LANES))


def moe_gather(h3, encrow, affrow, cnt, cap, kb=256):
    b_sz, t_len, d = h3.shape
    n_exp = encrow.shape[1]
    assert cap >= SLOT_WIN and t_len % kb == 0 and kb % TOKEN_BLK == 0 and n_exp % GATHER_GROUP == 0
    grp = GATHER_GROUP
    return pl.pallas_call(
        functools.partial(_gather_body, cap=cap, kb=kb),
        grid_spec=pltpu.PrefetchScalarGridSpec(
            num_scalar_prefetch=1,
            grid=(b_sz, n_exp // grp),
            in_specs=[pl.BlockSpec((1, t_len, d), lambda b, q, cnt: (b, 0, 0)),
                      pl.BlockSpec((1, n_exp, t_len), lambda b, q, cnt: (b, 0, 0)),
                      pl.BlockSpec((1, n_exp, t_len), lambda b, q, cnt: (b, 0, 0))],
            out_specs=[pl.BlockSpec((grp, cap, d), lambda b, q, cnt: (q, b, 0)),
                       pl.BlockSpec((grp, cap, LANES), lambda b, q, cnt: (q, b, 0))],
            scratch_shapes=[pltpu.VMEM((grp, cap, d), F32), pltpu.VMEM((grp, cap, LANES), F32)]),
        out_shape=[jax.ShapeDtypeStruct((n_exp, b_sz * cap, d), BF16),
                   jax.ShapeDtypeStruct((n_exp, b_sz * cap, LANES), F32)],
        compiler_params=_params(2),
        name="moe_gather",
    )(cnt, h3, encrow, affrow)


def _ffn_up_body(xs_ref, w1_ref, w3_ref, hid_ref):
    xs = xs_ref[0]
    a = jnp.dot(xs, w1_ref[0, 0].astype(BF16), preferred_element_type=F32)
    b = jnp.dot(xs, w3_ref[0, 0].astype(BF16), preferred_element_type=F32)
    hid_ref[0] = (a * jax.nn.sigmoid(a) * b).astype(hid_ref.dtype)


def _ffn_down_body(hid_ref, w2_ref, g_ref, y_ref):
    y = jnp.dot(hid_ref[0], w2_ref[0, 0].astype(BF16), preferred_element_type=F32)
    gate = jnp.concatenate([g_ref[0]] * (y.shape[1] // LANES), axis=1)
    y_ref[0] = (y * gate).astype(y_ref.dtype)


def moe_ffn(xs, gate, w1, w3, w2, layer, tf=512, tn=2048):
    n_exp, rows, d = xs.shape
    d_exp = w1.shape[-1]
    hid = pl.pallas_call(
        _ffn_up_body,
        grid=(n_exp, d_exp // tf),
        in_specs=[pl.BlockSpec((1, rows, d), lambda e, f: (e, 0, 0)),
                  pl.BlockSpec((1, 1, d, tf), lambda e, f: (layer, e, 0, f)),
                  pl.BlockSpec((1, 1, d, tf), lambda e, f: (layer, e, 0, f))],
        out_specs=pl.BlockSpec((1, rows, tf), lambda e, f: (e, 0, f)),
        out_shape=jax.ShapeDtypeStruct((n_exp, rows, d_exp), BF16),
        compiler_params=_params(2),
        name="moe_ffn_up",
    )(xs, w1, w3)
    return pl.pallas_call(
        _ffn_down_body,
        grid=(n_exp, d // tn),
        in_specs=[pl.BlockSpec((1, rows, d_exp), lambda e, n: (e, 0, 0)),
                  pl.BlockSpec((1, 1, d_exp, tn), lambda e, n: (layer, e, 0, n)),
                  pl.BlockSpec((1, rows, LANES), lambda e, n: (e, 0, 0))],
        out_specs=pl.BlockSpec((1, rows, tn), lambda e, n: (e, 0, n)),
        out_shape=jax.ShapeDtypeStruct((n_exp, rows, d), BF16),
        compiler_params=_params(2),
        name="moe_ffn_down",
    )(hid, w2, gate)


SCATTER_GROUP = 4


SCATTER_SUB = 2


def _scatter_body(cnt_ref, x_ref, y_ref, enc_ref, gn_ref, o_ref, hn_ref, *, cap):
    b = pl.program_id(0)
    j0 = pl.program_id(1) * SCATTER_SUB
    n_exp = enc_ref.shape[-1]
    starts, fits = [], None
    for u in range(SCATTER_SUB):
        row = []
        for e in range(n_exp):
            st, ok = _slot_window(cnt_ref[b, j0 + u, e], cnt_ref[b, j0 + u + 1, e], cap)
            row.append(pl.multiple_of(st, 1 << SLOT_ALIGN_LOG2))
            fits = ok if fits is None else jnp.logical_and(fits, ok)
        starts.append(row)

    def finish(rows, x_new):
        o_ref[0, rows, :] = x_new
        ms = jnp.mean(x_new * x_new, axis=-1, keepdims=True)
        hn_ref[0, rows, :] = (x_new * lax.rsqrt(ms + EPS) * gn_ref[...]).astype(hn_ref.dtype)

    @pl.when(fits)
    def _():
        width = SCATTER_GROUP * SLOT_WIN
        lane = lax.broadcasted_iota(I32, (TOKEN_BLK, width), 1)
        lane_row = lane[0:1, :]
        within = jnp.bitwise_and(lane_row, SLOT_WIN - 1) + 1
        for u in range(SCATTER_SUB):
            rows = slice(u * TOKEN_BLK, (u + 1) * TOKEN_BLK)
            enc = enc_ref[0, rows, :]
            onehots, windows = [], []
            for q in range(n_exp // SCATTER_GROUP):
                es = range(q * SCATTER_GROUP, (q + 1) * SCATTER_GROUP)
                code = enc[:, es[-1]:es[-1] + 1]
                first = starts[u][es[-1]]
                for k in reversed(range(SCATTER_GROUP - 1)):
                    code = jnp.where(lane < (k + 1) * SLOT_WIN, enc[:, es[k]:es[k] + 1], code)
                    first = jnp.where(lane_row < (k + 1) * SLOT_WIN, starts[u][es[k]], first)
                onehots.append((code == first + within).astype(BF16))
                windows += [y_ref[e, pl.ds(starts[u][e], SLOT_WIN), :] for e in es]
            finish(rows, x_ref[0, rows, :] + jnp.dot(
                jnp.concatenate(onehots, axis=1), jnp.concatenate(windows, axis=0),
                preferred_element_type=F32))

    @pl.when(jnp.logical_not(fits))
    def _():
        slot = lax.broadcasted_iota(I32, (TOKEN_BLK, cap), 1) + 1
        for u in range(SCATTER_SUB):
            rows = slice(u * TOKEN_BLK, (u + 1) * TOKEN_BLK)
            enc = enc_ref[0, rows, :]
            acc = x_ref[0, rows, :]
            for e in range(n_exp):
                onehot = (enc[:, e:e + 1] == slot).astype(BF16)
                acc = acc + jnp.dot(onehot, y_ref[e], preferred_element_type=F32)
            finish(rows, acc)


def moe_scatter_add(x3, y, enc, cnt, norm_g, norm_dtype, cap):
    b_sz, t_len, d = x3.shape
    n_exp = enc.shape[-1]
    assert cap >= SLOT_WIN and n_exp % SCATTER_GROUP == 0 and SLOT_WIN & (SLOT_WIN - 1) == 0
    rows = SCATTER_SUB * TOKEN_BLK
    assert t_len % rows == 0
    return pl.pallas_call(
        functools.partial(_scatter_body, cap=cap),
        grid_spec=pltpu.PrefetchScalarGridSpec(
            num_scalar_prefetch=1,
            grid=(b_sz, t_len // rows),
            in_specs=[pl.BlockSpec((1, rows, d), lambda b, j, cnt: (b, j, 0)),
                      pl.BlockSpec((n_exp, cap, d), lambda b, j, cnt: (0, b, 0)),
                      pl.BlockSpec((1, rows, n_exp), lambda b, j, cnt: (b, j, 0)),
                      pl.BlockSpec((1, d), lambda b, j, cnt: (0, 0))],
            out_specs=[pl.BlockSpec((1, rows, d), lambda b, j, cnt: (b, j, 0)),
                       pl.BlockSpec((1, rows, d), lambda b, j, cnt: (b, j, 0))]),
        out_shape=[jax.ShapeDtypeStruct((b_sz, t_len, d), F32),
                   jax.ShapeDtypeStruct((b_sz, t_len, d), norm_dtype)],
        compiler_params=_params(2),
        name="moe_scatter_add",
    )(cnt, x3, y, enc, norm_g.reshape(1, d))


def ec_moe_layer(x3, h2, aff2, w1, w3, w2, layer, next_norm_g, next_norm_dtype):
    b_sz, t_len, d = x3.shape
    n_exp = aff2.shape[-1]
    cap = EC_FACTOR * t_len // n_exp
    enc, encrow, affrow, cnt = moe_select(aff2.reshape(b_sz, t_len, n_exp), cap)
    xs, gate = moe_gather(h2.reshape(b_sz, t_len, d), encrow, affrow, cnt, cap)
    y = moe_ffn(xs, gate, w1, w3, w2, layer)
    return moe_scatter_add(x3, y, enc, cnt, next_norm_g, next_norm_dtype, cap)


def _layer_cols(w, layer, col0, n_cols):
    return lax.slice(w, (layer, 0, col0), (layer + 1, w.shape[1], col0 + n_cols))[0]


def ssd_mixer_layer(h2, kv3, b_sz, t_len, j, ssd_w_in, conv_w, conv_b, dt_bias, a_log, d_skip, gate_g):
    m, d = h2.shape
    n_heads = d_skip.shape[-1]
    d_ssd = n_heads * SSD_HEADDIM
    gn = SSD_GROUPS * SSD_STATE
    conv_dim = d_ssd + 2 * gn
    d_xa = XA_HEADS * XA_DH
    n_main = d_ssd + conv_dim
    tn = 1024
    proj = matmul_ws([h2], [(ssd_w_in, (1, d, tn), lambda n: (j, 0, n))], n_main,
                     tn=tn, tm=1024, out_dtype=F32, name="ssd_in_proj")
    w_dt = _layer_cols(ssd_w_in, j, n_main, 2 * n_heads)
    w_dt = jnp.pad(w_dt.reshape(d, 2, n_heads), ((0, 0), (0, 0), (0, LANES - n_heads))).reshape(d, 2 * LANES)
    w_tail = jnp.concatenate([_layer_cols(ssd_w_in, j, n_main + 2 * n_heads, d_xa), w_dt], axis=1)
    n_tail = d_xa + 2 * LANES
    tail = matmul_ws([h2], [(w_tail, (d, n_tail), lambda n: (0, 0))], n_tail,
                     tn=n_tail, tm=512, out_dtype=F32, name="ssd_tail_proj").reshape(b_sz, t_len, n_tail)

    proj3 = proj.reshape(b_sz, t_len, n_main)
    xbc = conv_silu(proj3, conv_w, conv_b, j, d_ssd, conv_dim)
    per_head = (dt_bias.shape[0] * 2, n_heads, SSD_CHUNK)
    bias_t = jnp.broadcast_to(dt_bias.reshape(-1, n_heads, 1), per_head)
    alog_t = jnp.broadcast_to(a_log.reshape(-1, n_heads, 1), per_head)
    dt_col0 = d_xa // LANES
    y_fwd = ssd_scan(xbc, tail, bias_t, alog_t, j, n_heads=n_heads, dt_col0=dt_col0)
    y = ssd_scan(xbc, tail, bias_t, alog_t, j, n_heads=n_heads, dt_col0=dt_col0,
                 epilogue=(y_fwd, proj3, jnp.repeat(d_skip[j], SSD_HEADDIM).reshape(1, d_ssd),
                           gate_g[j].reshape(1, d_ssd))).reshape(m, d_ssd)
    o_x = mem_attention(tail, 0, kv3).reshape(m, d_xa)
    return y, o_x


def na_mixer_layer(h2, kv3, b_sz, t_len, j, na_w_in, na_rpb):
    m, d = h2.shape
    n_heads = na_rpb.shape[1]
    d_na = n_heads * NA_DH
    d_xa = XA_HEADS * XA_DH
    n_in = 3 * d_na + d_xa
    tn = 1024
    proj = matmul_ws([h2], [(na_w_in, (1, d, tn), lambda n: (j, 0, n))], n_in,
                     tn=tn, tm=1024, out_dtype=BF16, name="na_in_proj")
    proj3 = proj.reshape(b_sz, t_len, n_in)
    rpb = na_rpb[j]
    table = na_bias_table(jnp.pad(rpb, ((0, 0), (0, 0), (0, LANES - rpb.shape[-1]))))
    o_na = na_attention(proj3, table, n_heads).reshape(m, d_na)
    o_x = mem_attention(proj3, 3 * d_na // d_xa, kv3).reshape(m, d_xa)
    return o_na, o_x


def kernel(x, mem, norm_mix_g, norm_ffn_g, norm_final_g, mem_norm_g, ssd_w_in, ssd_conv_w, ssd_conv_b,
           ssd_dt_bias, ssd_a_log, ssd_d, ssd_gate_norm_g, ssd_w_out, na_w_in, na_rpb, na_w_out, xa_w_kv,
           moe_w_router, moe_w1, moe_w3, moe_w2):
    b_sz, t_len, d = x.shape
    mem_len = mem.shape[1]
    depth = norm_mix_g.shape[0]
    m = b_sz * t_len
    d_kv = xa_w_kv.shape[-1]
    mem_n = rmsnorm_rows(mem.reshape(b_sz * mem_len, d), mem_norm_g, BF16)
    x2 = x.reshape(m, d)
    h2 = rmsnorm_rows(x2, norm_mix_g[0], BF16)
    for i in range(depth):
        j = i // N_MIXERS
        kv3 = matmul_ws([mem_n], [(xa_w_kv, (1, d, d_kv), lambda n, i=i: (i, 0, 0))], d_kv,
                        tn=d_kv, tm=512, out_dtype=BF16, name="xa_kv_proj").reshape(b_sz, mem_len, d_kv)
        if i % N_MIXERS == 0:
            heads, o_x = ssd_mixer_layer(h2, kv3, b_sz, t_len, j, ssd_w_in, ssd_conv_w, ssd_conv_b,
                                         ssd_dt_bias, ssd_a_log, ssd_d, ssd_gate_norm_g)
            w_out = ssd_w_out
        else:
            heads, o_x = na_mixer_layer(h2, kv3, b_sz, t_len, j, na_w_in, na_rpb)
            w_out = na_w_out
        x2, h_ffn, aff2 = out_proj_router(heads, o_x, w_out, j, x2, norm_ffn_g[i], moe_w_router, i)
        last = i == depth - 1
        x3, normed = ec_moe_layer(x2.reshape(b_sz, t_len, d), h_ffn, aff2, moe_w1, moe_w3, moe_w2, i,
                                  norm_final_g if last else norm_mix_g[i + 1], F32 if last else BF16)
        x2, h2 = x3.reshape(m, d), normed.reshape(m, d)
    return normed
```

```python
import functools

import jax
import jax.numpy as jnp
from jax import lax
from jax.experimental import pallas as pl
from jax.experimental.pallas import tpu as pltpu

F32 = jnp.float32
BF16 = jnp.bfloat16
I32 = jnp.int32
HIGHEST = lax.Precision.HIGHEST

EPS = 1e-6
LANES = 128
VMEM_LIMIT_BYTES = 56 * 1024 * 1024

XA_HEADS = 4
XA_DH = 128
SSD_HEADDIM = 64
SSD_GROUPS = 4
SSD_STATE = 128
SSD_CONV = 5
SSD_CHUNK = 128
NA_DH = 128
NA_KR = 8
NA_KC = 16
GRID_W = 64
EC_FACTOR = 2
N_MIXERS = 2


def _params(n_grid_dims):
    return pltpu.CompilerParams(
        dimension_semantics=("arbitrary",) * n_grid_dims,
        vmem_limit_bytes=VMEM_LIMIT_BYTES)


def _nt_dot(a, b):
    return lax.dot_general(a, b, (((1,), (1,)), ((), ())), preferred_element_type=F32)


def _rmsnorm_body(x_ref, g_ref, o_ref):
    x = x_ref[...]
    ms = jnp.mean(x * x, axis=-1, keepdims=True)
    o_ref[...] = (x * lax.rsqrt(ms + EPS) * g_ref[...]).astype(o_ref.dtype)


def rmsnorm_rows(x2d, g, out_dtype, tm=256):
    m, d = x2d.shape
    return pl.pallas_call(
        _rmsnorm_body,
        grid=(m // tm,),
        in_specs=[pl.BlockSpec((tm, d), lambda i: (i, 0)),
                  pl.BlockSpec((1, d), lambda i: (0, 0))],
        out_specs=pl.BlockSpec((tm, d), lambda i: (i, 0)),
        out_shape=jax.ShapeDtypeStruct((m, d), out_dtype),
        compiler_params=_params(1),
        name="rmsnorm",
    )(x2d, g.reshape(1, d))


def _matmul_body(*refs, n_in, has_res):
    a_refs = refs[:n_in]
    w_refs = refs[n_in:2 * n_in]
    res_ref = refs[2 * n_in] if has_res else None
    o_ref = refs[2 * n_in + has_res]
    wbf_refs = refs[2 * n_in + has_res + 1:]

    @pl.when(pl.program_id(1) == 0)
    def _():
        for w_ref, wbf_ref in zip(w_refs, wbf_refs):
            lead = (0,) * (len(w_ref.shape) - 2)
            wbf_ref[...] = w_ref[lead + (slice(None), slice(None))].astype(BF16)

    acc = None
    for a_ref, wbf_ref in zip(a_refs, wbf_refs):
        t = jnp.dot(a_ref[...], wbf_ref[...], preferred_element_type=F32)
        acc = t if acc is None else acc + t
    if has_res:
        acc = acc + res_ref[...]
    o_ref[...] = acc.astype(o_ref.dtype)


def matmul_ws(a_list, w_list, n_cols, *, tn, tm, out_dtype, res=None, name="matmul"):
    m = a_list[0].shape[0]
    n_in = len(a_list)
    in_specs = [pl.BlockSpec((tm, a.shape[1]), lambda n, i: (i, 0)) for a in a_list]
    scratch = []
    for _, blk, imap in w_list:
        in_specs.append(pl.BlockSpec(blk, lambda n, i, imap=imap: imap(n)))
        scratch.append(pltpu.VMEM(blk[-2:], BF16))
    args = list(a_list) + [w for w, _, _ in w_list]
    if res is not None:
        in_specs.append(pl.BlockSpec((tm, tn), lambda n, i: (i, n)))
        args.append(res)
    return pl.pallas_call(
        functools.partial(_matmul_body, n_in=n_in, has_res=res is not None),
        grid=(n_cols // tn, m // tm),
        in_specs=in_specs,
        out_specs=pl.BlockSpec((tm, tn), lambda n, i: (i, n)),
        out_shape=jax.ShapeDtypeStruct((m, n_cols), out_dtype),
        scratch_shapes=scratch,
        compiler_params=_params(2),
        name=name,
    )(*args)


def _conv_silu_body(u_ref, w_ref, b_ref, o_ref, pad_ref, *, t_len, k_conv, rows):
    halo = 8
    ch = u_ref.shape[-1]
    pad_ref[0:halo, :] = jnp.zeros((halo, ch), F32)
    pad_ref[t_len + halo:t_len + 2 * halo, :] = jnp.zeros((halo, ch), F32)
    pad_ref[halo:t_len + halo, :] = u_ref[0]
    w = w_ref[0]
    b = b_ref[0]
    first = halo - k_conv // 2

    def body(i, carry):
        r0 = pl.multiple_of(i * rows, rows)
        win = pad_ref[pl.ds(r0, rows + 2 * halo), :]
        n_win = rows + 2 * halo
        acc = b
        for k in range(k_conv):
            off = first + k
            tap = win if off % 8 == 0 else pltpu.roll(win, n_win - off % 8, 0)
            lo = off - off % 8
            acc = acc + w[k:k + 1, :] * tap[lo:lo + rows]
        o_ref[0, pl.ds(r0, rows), :] = acc * jax.nn.sigmoid(acc)
        return carry

    lax.fori_loop(0, t_len // rows, body, 0)


def conv_silu(proj3, conv_w, conv_b, layer, col0, n_ch, tc=512, rows=128):
    b_sz, t_len, _ = proj3.shape
    k_conv = conv_w.shape[1]
    c0 = col0 // tc
    return pl.pallas_call(
        functools.partial(_conv_silu_body, t_len=t_len, k_conv=k_conv, rows=rows),
        grid=(b_sz, n_ch // tc),
        in_specs=[pl.BlockSpec((1, t_len, tc), lambda b, j: (b, 0, c0 + j)),
                  pl.BlockSpec((1, k_conv, tc), lambda b, j: (layer, 0, j)),
                  pl.BlockSpec((1, 1, tc), lambda b, j: (layer, 0, j))],
        out_specs=pl.BlockSpec((1, t_len, tc), lambda b, j: (b, 0, j)),
        out_shape=jax.ShapeDtypeStruct((b_sz, t_len, n_ch), F32),
        scratch_shapes=[pltpu.VMEM((t_len + 16, tc), F32)],
        compiler_params=_params(2),
        name="ssd_conv_silu",
    )(proj3, conv_w, conv_b.reshape(conv_b.shape[0], 1, conv_b.shape[1]))


def _softplus(x):
    return jnp.maximum(x, 0.0) + jnp.log1p(jnp.exp(-jnp.abs(x)))


SSD_BATCH_PER_STEP = 2


def _ssd_body(*refs, chunk, n_heads, head_dim, n_groups, backward):
    if backward:
        (xs_ref, b_ref, c_ref, dt_ref, bias_ref, alog_ref, yf_ref, z_ref, dskip_ref, gnorm_ref,
         y_ref, state_ref) = refs
    else:
        xs_ref, b_ref, c_ref, dt_ref, bias_ref, alog_ref, y_ref, state_ref = refs
    L = chunk
    hpg = n_heads // n_groups
    gw = hpg * head_dim

    @pl.when(pl.program_id(1) == 0)
    def _():
        state_ref[...] = jnp.zeros_like(state_ref)

    row = lax.broadcasted_iota(I32, (L, L), 0)
    col = lax.broadcasted_iota(I32, (L, L), 1)
    keep = row <= col if backward else row >= col
    keep_t = col <= row if backward else col >= row
    edge = 0 if backward else L - 1
    lo = lax.broadcasted_iota(I32, (L, LANES), 1) < head_dim
    lo_row = lo[0:1, :]
    for bi, g in [(bi, g) for bi in range(xs_ref.shape[0]) for g in range(n_groups)]:
        if g == 0:
            dt_t = _softplus(dt_ref[bi].T[0:n_heads, :] + bias_ref[0])
            da_t = dt_t * (-jnp.exp(alog_ref[0]))
            x_t = jnp.dot(da_t, keep_t.astype(F32), precision=HIGHEST, preferred_element_type=F32)
            w_t = dt_t * jnp.exp(x_t[:, edge:edge + 1] - x_t)
            x_c = jnp.concatenate([x_t, jnp.zeros((LANES - n_heads, L), F32)], axis=0).T
            e_tot = jnp.exp(x_c[edge:edge + 1, :])
        b_gt = b_ref[bi, :, g * SSD_STATE:(g + 1) * SSD_STATE].T
        c_g = c_ref[bi, :, g * SSD_STATE:(g + 1) * SSD_STATE].astype(BF16)
        cb = jnp.dot(c_g, b_gt.astype(BF16), preferred_element_type=F32)
        y_off = jnp.dot(c_g, state_ref[bi, g].astype(BF16), preferred_element_type=F32)
        y_pairs = []
        for pp in range(hpg // 2):
            h0 = g * hpg + 2 * pp
            h1 = h0 + 1
            c0 = h0 * head_dim
            lc = 2 * pp * head_dim
            xs_pair = xs_ref[bi, :, c0:c0 + LANES]
            rhs = jnp.concatenate([jnp.where(lo, xs_pair, 0.0).astype(BF16),
                                   jnp.where(lo, 0.0, xs_pair).astype(BF16)], axis=0)
            xc0 = jnp.broadcast_to(x_c[:, h0:h0 + 1], (L, L))
            xc1 = jnp.broadcast_to(x_c[:, h1:h1 + 1], (L, L))
            m0 = cb * jnp.exp(jnp.where(keep, xc0 - x_t[h0:h0 + 1, :], -jnp.inf)) * dt_t[h0:h0 + 1, :]
            m1 = cb * jnp.exp(jnp.where(keep, xc1 - x_t[h1:h1 + 1, :], -jnp.inf)) * dt_t[h1:h1 + 1, :]
            y_diag = jnp.dot(jnp.concatenate([m0.astype(BF16), m1.astype(BF16)], axis=1), rhs,
                             preferred_element_type=F32)
            e_out = jnp.where(lo, jnp.exp(xc0), jnp.exp(xc1))
            y_pair = y_diag + y_off[:, lc:lc + LANES] * e_out
            if backward:
                y_pairs.append(y_pair)
            else:
                y_ref[bi, :, c0:c0 + LANES] = y_pair
            lhs_b = jnp.concatenate([(b_gt * w_t[h0:h0 + 1, :]).astype(BF16),
                                     (b_gt * w_t[h1:h1 + 1, :]).astype(BF16)], axis=1)
            dec = jnp.where(lo_row, e_tot[:, h0:h0 + 1], e_tot[:, h1:h1 + 1])
            state_ref[bi, g, :, lc:lc + LANES] = (state_ref[bi, g, :, lc:lc + LANES] * dec
                                                  + jnp.dot(lhs_b, rhs, preferred_element_type=F32))
        if backward:
            cols = slice(g * gw, (g + 1) * gw)
            y = (yf_ref[bi, :, cols] + jnp.concatenate(y_pairs, axis=1)
                 + dskip_ref[:, cols] * xs_ref[bi, :, cols])
            z = z_ref[bi, :, cols]
            y = y * (z * jax.nn.sigmoid(z))
            ms = jnp.mean(y * y, axis=-1, keepdims=True)
            y_ref[bi, :, cols] = (y * lax.rsqrt(ms + EPS) * gnorm_ref[:, cols]).astype(y_ref.dtype)


def ssd_scan(xbc, dt_raw, dt_bias, a_log, layer, *, n_heads, dt_col0=0, epilogue=None):
    b_sz, t_len, _ = xbc.shape
    d_ssd = n_heads * SSD_HEADDIM
    gn = SSD_GROUPS * SSD_STATE
    L = SSD_CHUNK
    assert L == LANES and 2 * SSD_HEADDIM == LANES and (n_heads // SSD_GROUPS) % 2 == 0
    nc = t_len // L
    xb = d_ssd // gn
    backward = epilogue is not None
    direction = int(backward)

    def chunk_of(c):
        return nc - 1 - c if backward else c

    nb = SSD_BATCH_PER_STEP if b_sz % SSD_BATCH_PER_STEP == 0 else 1
    in_specs = [pl.BlockSpec((nb, L, d_ssd), lambda b, c: (b, chunk_of(c), 0)),
                pl.BlockSpec((nb, L, gn), lambda b, c: (b, chunk_of(c), xb)),
                pl.BlockSpec((nb, L, gn), lambda b, c: (b, chunk_of(c), xb + 1)),
                pl.BlockSpec((nb, L, LANES), lambda b, c: (b, chunk_of(c), dt_col0 + direction)),
                pl.BlockSpec((1, n_heads, L), lambda b, c: (layer * 2 + direction, 0, 0)),
                pl.BlockSpec((1, n_heads, L), lambda b, c: (layer * 2 + direction, 0, 0))]
    args = [xbc, xbc, xbc, dt_raw, dt_bias, a_log]
    if backward:
        in_specs += [pl.BlockSpec((nb, L, d_ssd), lambda b, c: (b, chunk_of(c), 0)),
                     pl.BlockSpec((nb, L, d_ssd), lambda b, c: (b, chunk_of(c), 0)),
                     pl.BlockSpec((1, d_ssd), lambda b, c: (0, 0)),
                     pl.BlockSpec((1, d_ssd), lambda b, c: (0, 0))]
        args += list(epilogue)
    return pl.pallas_call(
        functools.partial(_ssd_body, chunk=L, n_heads=n_heads, head_dim=SSD_HEADDIM,
                          n_groups=SSD_GROUPS, backward=backward),
        grid=(b_sz // nb, nc),
        in_specs=in_specs,
        out_specs=pl.BlockSpec((nb, L, d_ssd), lambda b, c: (b, chunk_of(c), 0)),
        out_shape=jax.ShapeDtypeStruct((b_sz, t_len, d_ssd), BF16 if backward else F32),
        scratch_shapes=[pltpu.VMEM((nb, SSD_GROUPS, SSD_STATE, d_ssd // SSD_GROUPS), F32)],
        compiler_params=_params(2),
        name="ssd_scan_bwd_gate" if backward else "ssd_scan_fwd",
    )(*args)


def _softmax_rows(s):
    m = jnp.max(s, axis=-1, keepdims=True)
    e = jnp.exp(s - m)
    return e / jnp.sum(e, axis=-1, keepdims=True)


def _mem_attn_body(q_ref, kv_ref, o_ref, *, n_heads, dh):
    scale = dh ** -0.5
    for h in range(n_heads):
        q = q_ref[0, :, h * dh:(h + 1) * dh].astype(BF16)
        k = kv_ref[0, :, h * dh:(h + 1) * dh]
        v = kv_ref[0, :, (n_heads + h) * dh:(n_heads + h + 1) * dh]
        p = _softmax_rows(_nt_dot(q, k) * scale).astype(BF16)
        o_ref[0, :, h * dh:(h + 1) * dh] = jnp.dot(p, v, preferred_element_type=F32).astype(o_ref.dtype)


def mem_attention(q3, q_col_block, kv3, tm=512):
    b_sz, t_len, _ = q3.shape
    d_xa = XA_HEADS * XA_DH
    mem_len = kv3.shape[1]
    return pl.pallas_call(
        functools.partial(_mem_attn_body, n_heads=XA_HEADS, dh=XA_DH),
        grid=(b_sz, t_len // tm),
        in_specs=[pl.BlockSpec((1, tm, d_xa), lambda b, i: (b, i, q_col_block)),
                  pl.BlockSpec((1, mem_len, 2 * d_xa), lambda b, i: (b, 0, 0))],
        out_specs=pl.BlockSpec((1, tm, d_xa), lambda b, i: (b, i, 0)),
        out_shape=jax.ShapeDtypeStruct((b_sz, t_len, d_xa), BF16),
        compiler_params=_params(2),
        name="mem_attention",
    )(q3, kv3)


NA_QROWS = 4
NA_KROWS = 12
NEG_MASK = -1e30


def _na_bias_body(rpb_ref, o_ref, *, n_dr, width, kc):
    c_idx = lax.broadcasted_iota(I32, (width, LANES), 0)
    lane = lax.broadcasted_iota(I32, (width, LANES), 1)
    first = lane < width
    k_idx = jnp.where(first, lane, lane - width)
    w_start = jnp.clip(c_idx - kc // 2, 0, width - kc)
    valid = (k_idx >= w_start) & (k_idx < w_start + kc)
    base = LANES - (kc - 1)
    neg = jnp.full((width, LANES), NEG_MASK, F32)
    lo_half, hi_half = [], []
    for d in range(n_dr):
        x = jnp.broadcast_to(rpb_ref[0, d:d + 1, :], (width, LANES))
        lo_half.append(jnp.where(valid & first, pltpu.roll(x, base, 1, stride=1, stride_axis=0), neg))
        hi_half.append(jnp.where(valid & jnp.logical_not(first),
                                 pltpu.roll(x, (base + width) % LANES, 1, stride=1, stride_axis=0), neg))
    for d in range(n_dr - 1):
        o_ref[0, d] = jnp.where(first, lo_half[d], hi_half[d + 1])
    for d in range(n_dr):
        o_ref[0, n_dr - 1 + d] = lo_half[d]
        o_ref[0, 2 * n_dr - 1 + d] = hi_half[d]
    o_ref[0, 3 * n_dr - 1] = neg


def na_bias_table(rpb_padded):
    n_heads, n_dr, _ = rpb_padded.shape
    n_ent = 3 * n_dr
    return pl.pallas_call(
        functools.partial(_na_bias_body, n_dr=n_dr, width=GRID_W, kc=NA_KC),
        grid=(n_heads,),
        in_specs=[pl.BlockSpec((1, n_dr, LANES), lambda h: (h, 0, 0))],
        out_specs=pl.BlockSpec((1, n_ent, GRID_W, 2 * GRID_W), lambda h: (h, 0, 0, 0)),
        out_shape=jax.ShapeDtypeStruct((n_heads, n_ent, GRID_W, 2 * GRID_W), F32),
        compiler_params=_params(1),
        name="na_bias_table",
    )(rpb_padded)


def _na_table_entry(r, ka, rows, kr, n_dr):
    rs = min(max(r - kr // 2, 0), rows - kr)
    in_a = rs <= ka < rs + kr
    in_b = rs <= ka + 1 < rs + kr
    d_a = ka - r + (NA_KR - 1)
    if in_a and in_b:
        return d_a
    if in_a:
        return n_dr - 1 + d_a
    if in_b:
        return 2 * n_dr - 1 + d_a + 1
    return 3 * n_dr - 1


def _na_body(q_ref, k_ref, v_ref, tb_ref, o_ref, *, rows, width, kr, dh):
    scale = dh ** -0.5
    n_dr = 2 * NA_KR - 1
    nq = NA_QROWS * width
    nk = NA_KROWS * width
    for blk in range(rows // NA_QROWS):
        r0 = blk * NA_QROWS
        k_row0 = min(max(r0 - kr // 2, 0), rows - NA_KROWS)
        q = q_ref[0, r0 * width:r0 * width + nq, :]
        kw = k_ref[0, k_row0 * width:k_row0 * width + nk, :]
        vw = v_ref[0, k_row0 * width:k_row0 * width + nk, :]
        s = _nt_dot(q, kw)
        e_rows, dens = [], []
        for i in range(NA_QROWS):
            entries = [_na_table_entry(r0 + i, k_row0 + 2 * j, rows, kr, n_dr) for j in range(nk // LANES)]
            live = [j for j, ent in enumerate(entries) if ent != 3 * n_dr - 1]
            tiles = {j: s[i * width:(i + 1) * width, j * LANES:(j + 1) * LANES] * scale + tb_ref[0, entries[j]]
                     for j in live}
            m = jnp.max(functools.reduce(jnp.maximum, tiles.values()), axis=-1, keepdims=True)
            e_tiles = {j: jnp.exp(t - m) for j, t in tiles.items()}
            dens.append(jnp.sum(functools.reduce(jnp.add, e_tiles.values()), axis=-1, keepdims=True))
            e_rows.append(jnp.concatenate(
                [e_tiles[j].astype(BF16) if j in e_tiles else jnp.zeros((width, LANES), BF16)
                 for j in range(nk // LANES)], axis=1))
        o = (jnp.dot(jnp.concatenate(e_rows, axis=0), vw, preferred_element_type=F32)
             / jnp.concatenate(dens, axis=0))
        o_ref[0, r0 * width:r0 * width + nq, :] = o.astype(o_ref.dtype)


def na_attention(proj3, table, n_heads):
    b_sz, t_len, _ = proj3.shape
    rows = t_len // GRID_W
    kr = min(NA_KR, rows)
    assert kr == NA_KR and rows % NA_QROWS == 0 and rows >= NA_KROWS and (rows - NA_KROWS) % 2 == 0
    assert NA_KROWS >= kr + NA_QROWS - 1 and 2 * GRID_W == LANES
    n_dr2 = table.shape[1]
    return pl.pallas_call(
        functools.partial(_na_body, rows=rows, width=GRID_W, kr=kr, dh=NA_DH),
        grid=(b_sz, n_heads),
        in_specs=[pl.BlockSpec((1, t_len, NA_DH), lambda b, h: (b, 0, h)),
                  pl.BlockSpec((1, t_len, NA_DH), lambda b, h: (b, 0, n_heads + h)),
                  pl.BlockSpec((1, t_len, NA_DH), lambda b, h: (b, 0, 2 * n_heads + h)),
                  pl.BlockSpec((1, n_dr2, GRID_W, 2 * GRID_W), lambda b, h: (h, 0, 0, 0))],
        out_specs=pl.BlockSpec((1, t_len, NA_DH), lambda b, h: (b, 0, h)),
        out_shape=jax.ShapeDtypeStruct((b_sz, t_len, n_heads * NA_DH), BF16),
        compiler_params=_params(2),
        name="na_attention",
    )(proj3, proj3, proj3, table)


def _norm_and_route(x, g_ref, w_ref, h_ref, aff_ref):
    ms = jnp.mean(x * x, axis=-1, keepdims=True)
    hn = x * lax.rsqrt(ms + EPS) * g_ref[...]
    hn_hi = hn.astype(BF16)
    h_ref[...] = hn_hi
    hn_lo = (hn - hn_hi.astype(F32)).astype(BF16)
    w = w_ref[0]
    w_hi = w.astype(BF16)
    w_lo = (w - w_hi.astype(F32)).astype(BF16)
    n_exp = w.shape[1]
    both = jnp.dot(hn_hi, jnp.concatenate([w_hi, w_lo], axis=1), preferred_element_type=F32)
    logits = both[:, :n_exp] + (both[:, n_exp:] + jnp.dot(hn_lo, w_hi, preferred_element_type=F32))
    aff_ref[...] = _softmax_rows(logits)


def _out_proj_router_body(a1_ref, a2_ref, w1_ref, w2_ref, res_ref, g_ref, wr_ref,
                          x_ref, h_ref, aff_ref, wbf1_ref, wbf2_ref, *, sub_rows):
    @pl.when(pl.program_id(0) == 0)
    def _():
        wbf1_ref[...] = w1_ref[0].astype(BF16)
        wbf2_ref[...] = w2_ref[0].astype(BF16)

    for r in range(0, a1_ref.shape[0], sub_rows):
        rows = slice(r, r + sub_rows)
        acc = (jnp.dot(a1_ref[rows, :], wbf1_ref[...], preferred_element_type=F32)
               + jnp.dot(a2_ref[rows, :], wbf2_ref[...], preferred_element_type=F32))
        x = acc + res_ref[rows, :]
        x_ref[rows, :] = x
        _norm_and_route(x, g_ref, wr_ref, h_ref.at[rows, :], aff_ref.at[rows, :])


def out_proj_router(a1, a2, w_out, layer_w, res, norm_g, w_router, layer, tm=512, sub_rows=256):
    m, k1 = a1.shape
    k2 = a2.shape[1]
    d = w_out.shape[-1]
    n_exp = w_router.shape[-1]
    assert k1 % k2 == 0
    once = pl.Buffered(1)
    return pl.pallas_call(
        functools.partial(_out_proj_router_body, sub_rows=sub_rows),
        grid=(m // tm,),
        in_specs=[pl.BlockSpec((tm, k1), lambda i: (i, 0)),
                  pl.BlockSpec((tm, k2), lambda i: (i, 0)),
                  pl.BlockSpec((1, k1, d), lambda i: (layer_w, 0, 0), pipeline_mode=once),
                  pl.BlockSpec((1, k2, d), lambda i: (layer_w, k1 // k2, 0), pipeline_mode=once),
                  pl.BlockSpec((tm, d), lambda i: (i, 0)),
                  pl.BlockSpec((1, d), lambda i: (0, 0)),
                  pl.BlockSpec((1, d, n_exp), lambda i: (layer, 0, 0))],
        out_specs=[pl.BlockSpec((tm, d), lambda i: (i, 0)),
                   pl.BlockSpec((tm, d), lambda i: (i, 0)),
                   pl.BlockSpec((tm, n_exp), lambda i: (i, 0))],
        out_shape=[jax.ShapeDtypeStruct((m, d), F32),
                   jax.ShapeDtypeStruct((m, d), BF16),
                   jax.ShapeDtypeStruct((m, n_exp), F32)],
        scratch_shapes=[pltpu.VMEM((k1, d), BF16), pltpu.VMEM((k2, d), BF16)],
        compiler_params=_params(1),
        name="out_proj_router",
    )(a1, a2, w_out, w_out, res, norm_g.reshape(1, d), w_router)


TOKEN_BLK = 128
SLOT_WIN = 64
SLOT_ALIGN_LOG2 = 4


def _slot_window(c0, c1, cap):
    st = jnp.minimum(lax.shift_left(lax.shift_right_logical(c0, SLOT_ALIGN_LOG2), SLOT_ALIGN_LOG2),
                     cap - SLOT_WIN)
    return st, c1 <= st + SLOT_WIN


def _select_body(aff_ref, enc_ref, encrow_ref, affrow_ref, cnt_ref, *, cap):
    a = aff_ref[0]
    t_len, n_exp = a.shape
    zero = jnp.zeros((1, n_exp), I32)

    def count(pred):
        return jnp.sum(pred.astype(I32), axis=0, keepdims=True)

    def value_bit(i, prefix):
        cand = prefix | jnp.left_shift(jnp.int32(1), 30 - i)
        return jnp.where(count(a >= lax.bitcast_convert_type(cand, F32)) >= cap, cand, prefix)

    thresh = lax.bitcast_convert_type(lax.fori_loop(0, 31, value_bit, zero), F32)
    above = a > thresh
    tied = a == thresh
    need = cap - count(above)
    idx = lax.broadcasted_iota(I32, (t_len, n_exp), 0)
    n_idx_bits = (t_len - 1).bit_length()

    def index_bit(i, last):
        cand = last | jnp.left_shift(jnp.int32(1), n_idx_bits - 1 - i)
        return jnp.where(count(tied & (idx < cand)) < need, cand, last)

    last = lax.fori_loop(0, n_idx_bits, index_bit, zero)
    mask = above | (tied & (idx <= last))
    mask_bf = mask.astype(BF16)

    n_rows = cnt_ref.shape[1]
    bound = lax.broadcasted_iota(I32, (n_rows, t_len), 0) * TOKEN_BLK
    tok = lax.broadcasted_iota(I32, (n_rows, t_len), 1)
    cnt = jnp.dot((tok < bound).astype(BF16), mask_bf, preferred_element_type=F32)
    cnt_ref[0] = cnt.astype(I32)
    earlier = (lax.broadcasted_iota(I32, (TOKEN_BLK, TOKEN_BLK), 0)
               > lax.broadcasted_iota(I32, (TOKEN_BLK, TOKEN_BLK), 1)).astype(BF16)
    for i in range(t_len // TOKEN_BLK):
        rows = slice(i * TOKEN_BLK, (i + 1) * TOKEN_BLK)
        before = jnp.dot(earlier, mask_bf[rows], preferred_element_type=F32) + cnt[i:i + 1, :]
        enc_ref[0, rows, :] = jnp.where(mask[rows], before + 1.0, 0.0).astype(I32)

    eye = (lax.broadcasted_iota(I32, (n_exp, n_exp), 0)
           == lax.broadcasted_iota(I32, (n_exp, n_exp), 1)).astype(BF16)
    enc_bf = enc_ref[0].astype(F32).astype(BF16)
    encrow_ref[0] = _nt_dot(eye, enc_bf).astype(I32)

    a_hi = a.astype(BF16)
    rest = a - a_hi.astype(F32)
    a_mid = rest.astype(BF16)
    a_lo = (rest - a_mid.astype(F32)).astype(BF16)
    affrow_ref[0] = _nt_dot(eye, a_hi) + (_nt_dot(eye, a_mid) + _nt_dot(eye, a_lo))


def moe_select(aff3, cap):
    b_sz, t_len, n_exp = aff3.shape
    assert cap <= 256, "slot codes must stay exactly representable in bf16"
    n_rows = -(-(t_len // TOKEN_BLK + 1) // 8) * 8
    return pl.pallas_call(
        functools.partial(_select_body, cap=cap),
        grid=(b_sz,),
        in_specs=[pl.BlockSpec((1, t_len, n_exp), lambda b: (b, 0, 0))],
        out_specs=[pl.BlockSpec((1, t_len, n_exp), lambda b: (b, 0, 0)),
                   pl.BlockSpec((1, n_exp, t_len), lambda b: (b, 0, 0)),
                   pl.BlockSpec((1, n_exp, t_len), lambda b: (b, 0, 0)),
                   pl.BlockSpec((1, n_rows, n_exp), lambda b: (b, 0, 0))],
        out_shape=[jax.ShapeDtypeStruct((b_sz, t_len, n_exp), I32),
                   jax.ShapeDtypeStruct((b_sz, n_exp, t_len), I32),
                   jax.ShapeDtypeStruct((b_sz, n_exp, t_len), F32),
                   jax.ShapeDtypeStruct((b_sz, n_rows, n_exp), I32)],
        compiler_params=_params(1),
        name="moe_select",
    )(aff3)


GATHER_GROUP = 4


def _gather_body(cnt_ref, h_ref, encrow_ref, affrow_ref, xs_ref, g_ref, acc_ref, gacc_ref, *, cap, kb):
    b = pl.program_id(0)
    e0 = pl.program_id(1) * GATHER_GROUP
    t_len = h_ref.shape[1]
    per = kb // TOKEN_BLK
    codes = [encrow_ref[0, pl.ds(e0 + i, 1), :] for i in range(GATHER_GROUP)]
    gates = [affrow_ref[0, pl.ds(e0 + i, 1), :] for i in range(GATHER_GROUP)]
    starts, fits = [], None
    for j in range(t_len // kb):
        row = []
        for i in range(GATHER_GROUP):
            st, ok = _slot_window(cnt_ref[b, j * per, e0 + i], cnt_ref[b, (j + 1) * per, e0 + i], cap)
            row.append(pl.multiple_of(st, 1 << SLOT_ALIGN_LOG2))
            fits = ok if fits is None else jnp.logical_and(fits, ok)
        starts.append(row)

    @pl.when(fits)
    def _():
        acc_ref[...] = jnp.zeros_like(acc_ref)
        gacc_ref[...] = jnp.zeros_like(gacc_ref)
        sub = lax.broadcasted_iota(I32, (SLOT_WIN, kb), 0) + 1
        for j, row in enumerate(starts):
            tok = slice(j * kb, (j + 1) * kb)
            hits = [(sub + row[i]) == codes[i][:, tok] for i in range(GATHER_GROUP)]
            picked = jnp.dot(jnp.concatenate([hit.astype(BF16) for hit in hits], axis=0), h_ref[0, tok, :],
                             preferred_element_type=F32)
            for i in range(GATHER_GROUP):
                acc_ref[i, pl.ds(row[i], SLOT_WIN), :] += picked[i * SLOT_WIN:(i + 1) * SLOT_WIN]
                g = jnp.sum(jnp.where(hits[i], gates[i][:, tok], 0.0), axis=1, keepdims=True)
                gacc_ref[i, pl.ds(row[i], SLOT_WIN), :] += jnp.broadcast_to(g, (SLOT_WIN, LANES))
        xs_ref[...] = acc_ref[...].astype(BF16)
        g_ref[...] = gacc_ref[...]

    @pl.when(jnp.logical_not(fits))
    def _():
        slot = lax.broadcasted_iota(I32, (cap, t_len), 0) + 1
        for i in range(GATHER_GROUP):
            hit = slot == codes[i]
            xs_ref[i] = jnp.dot(hit.astype(BF16), h_ref[0], preferred_element_type=F32).astype(BF16)
            g = jnp.sum(jnp.where(hit, gates[i], 0.0), axis=1, keepdims=True)
            g_ref[i] = jnp.broadcast_to(g, (cap, LANES))


def moe_gather(h3, encrow, affrow, cnt, cap, kb=256):
    b_sz, t_len, d = h3.shape
    n_exp = encrow.shape[1]
    assert cap >= SLOT_WIN and t_len % kb == 0 and kb % TOKEN_BLK == 0 and n_exp % GATHER_GROUP == 0
    grp = GATHER_GROUP
    return pl.pallas_call(
        functools.partial(_gather_body, cap=cap, kb=kb),
        grid_spec=pltpu.PrefetchScalarGridSpec(
            num_scalar_prefetch=1,
            grid=(b_sz, n_exp // grp),
            in_specs=[pl.BlockSpec((1, t_len, d), lambda b, q, cnt: (b, 0, 0)),
                      pl.BlockSpec((1, n_exp, t_len), lambda b, q, cnt: (b, 0, 0)),
                      pl.BlockSpec((1, n_exp, t_len), lambda b, q, cnt: (b, 0, 0))],
            out_specs=[pl.BlockSpec((grp, cap, d), lambda b, q, cnt: (q, b, 0)),
                       pl.BlockSpec((grp, cap, LANES), lambda b, q, cnt: (q, b, 0))],
            scratch_shapes=[pltpu.VMEM((grp, cap, d), F32), pltpu.VMEM((grp, cap, LANES), F32)]),
        out_shape=[jax.ShapeDtypeStruct((n_exp, b_sz * cap, d), BF16),
                   jax.ShapeDtypeStruct((n_exp, b_sz * cap, LANES), F32)],
        compiler_params=_params(2),
        name="moe_gather",
    )(cnt, h3, encrow, affrow)


def _ffn_up_body(xs_ref, w1_ref, w3_ref, hid_ref):
    xs = xs_ref[0]
    a = jnp.dot(xs, w1_ref[0, 0].astype(BF16), preferred_element_type=F32)
    b = jnp.dot(xs, w3_ref[0, 0].astype(BF16), preferred_element_type=F32)
    hid_ref[0] = (a * jax.nn.sigmoid(a) * b).astype(hid_ref.dtype)


def _ffn_down_body(hid_ref, w2_ref, g_ref, y_ref):
    y = jnp.dot(hid_ref[0], w2_ref[0, 0].astype(BF16), preferred_element_type=F32)
    gate = jnp.concatenate([g_ref[0]] * (y.shape[1] // LANES), axis=1)
    y_ref[0] = (y * gate).astype(y_ref.dtype)


def moe_ffn(xs, gate, w1, w3, w2, layer, tf=512, tn=2048):
    n_exp, rows, d = xs.shape
    d_exp = w1.shape[-1]
    hid = pl.pallas_call(
        _ffn_up_body,
        grid=(n_exp, d_exp // tf),
        in_specs=[pl.BlockSpec((1, rows, d), lambda e, f: (e, 0, 0)),
                  pl.BlockSpec((1, 1, d, tf), lambda e, f: (layer, e, 0, f)),
                  pl.BlockSpec((1, 1, d, tf), lambda e, f: (layer, e, 0, f))],
        out_specs=pl.BlockSpec((1, rows, tf), lambda e, f: (e, 0, f)),
        out_shape=jax.ShapeDtypeStruct((n_exp, rows, d_exp), BF16),
        compiler_params=_params(2),
        name="moe_ffn_up",
    )(xs, w1, w3)
    return pl.pallas_call(
        _ffn_down_body,
        grid=(n_exp, d // tn),
        in_specs=[pl.BlockSpec((1, rows, d_exp), lambda e, n: (e, 0, 0)),
                  pl.BlockSpec((1, 1, d_exp, tn), lambda e, n: (layer, e, 0, n)),
                  pl.BlockSpec((1, rows, LANES), lambda e, n: (e, 0, 0))],
        out_specs=pl.BlockSpec((1, rows, tn), lambda e, n: (e, 0, n)),
        out_shape=jax.ShapeDtypeStruct((n_exp, rows, d), BF16),
        compiler_params=_params(2),
        name="moe_ffn_down",
    )(hid, w2, gate)


SCATTER_GROUP = 4


SCATTER_SUB = 2


def _scatter_body(cnt_ref, x_ref, y_ref, enc_ref, gn_ref, o_ref, hn_ref, *, cap):
    b = pl.program_id(0)
    j0 = pl.program_id(1) * SCATTER_SUB
    n_exp = enc_ref.shape[-1]
    starts, fits = [], None
    for u in range(SCATTER_SUB):
        row = []
        for e in range(n_exp):
            st, ok = _slot_window(cnt_ref[b, j0 + u, e], cnt_ref[b, j0 + u + 1, e], cap)
            row.append(pl.multiple_of(st, 1 << SLOT_ALIGN_LOG2))
            fits = ok if fits is None else jnp.logical_and(fits, ok)
        starts.append(row)

    def finish(rows, x_new):
        o_ref[0, rows, :] = x_new
        ms = jnp.mean(x_new * x_new, axis=-1, keepdims=True)
        hn_ref[0, rows, :] = (x_new * lax.rsqrt(ms + EPS) * gn_ref[...]).astype(hn_ref.dtype)

    @pl.when(fits)
    def _():
        width = SCATTER_GROUP * SLOT_WIN
        lane = lax.broadcasted_iota(I32, (TOKEN_BLK, width), 1)
        lane_row = lane[0:1, :]
        within = jnp.bitwise_and(lane_row, SLOT_WIN - 1) + 1
        for u in range(SCATTER_SUB):
            rows = slice(u * TOKEN_BLK, (u + 1) * TOKEN_BLK)
            enc = enc_ref[0, rows, :]
            onehots, windows = [], []
            for q in range(n_exp // SCATTER_GROUP):
                es = range(q * SCATTER_GROUP, (q + 1) * SCATTER_GROUP)
                code = enc[:, es[-1]:es[-1] + 1]
                first = starts[u][es[-1]]
                for k in reversed(range(SCATTER_GROUP - 1)):
                    code = jnp.where(lane < (k + 1) * SLOT_WIN, enc[:, es[k]:es[k] + 1], code)
                    first = jnp.where(lane_row < (k + 1) * SLOT_WIN, starts[u][es[k]], first)
                onehots.append((code == first + within).astype(BF16))
                windows += [y_ref[e, pl.ds(starts[u][e], SLOT_WIN), :] for e in es]
            finish(rows, x_ref[0, rows, :] + jnp.dot(
                jnp.concatenate(onehots, axis=1), jnp.concatenate(windows, axis=0),
                preferred_element_type=F32))

    @pl.when(jnp.logical_not(fits))
    def _():
        slot = lax.broadcasted_iota(I32, (TOKEN_BLK, cap), 1) + 1
        for u in range(SCATTER_SUB):
            rows = slice(u * TOKEN_BLK, (u + 1) * TOKEN_BLK)
            enc = enc_ref[0, rows, :]
            acc = x_ref[0, rows, :]
            for e in range(n_exp):
                onehot = (enc[:, e:e + 1] == slot).astype(BF16)
                acc = acc + jnp.dot(onehot, y_ref[e], preferred_element_type=F32)
            finish(rows, acc)


def moe_scatter_add(x3, y, enc, cnt, norm_g, norm_dtype, cap):
    b_sz, t_len, d = x3.shape
    n_exp = enc.shape[-1]
    assert cap >= SLOT_WIN and n_exp % SCATTER_GROUP == 0 and SLOT_WIN & (SLOT_WIN - 1) == 0
    rows = SCATTER_SUB * TOKEN_BLK
    assert t_len % rows == 0
    return pl.pallas_call(
        functools.partial(_scatter_body, cap=cap),
        grid_spec=pltpu.PrefetchScalarGridSpec(
            num_scalar_prefetch=1,
            grid=(b_sz, t_len // rows),
            in_specs=[pl.BlockSpec((1, rows, d), lambda b, j, cnt: (b, j, 0)),
                      pl.BlockSpec((n_exp, cap, d), lambda b, j, cnt: (0, b, 0)),
                      pl.BlockSpec((1, rows, n_exp), lambda b, j, cnt: (b, j, 0)),
                      pl.BlockSpec((1, d), lambda b, j, cnt: (0, 0))],
            out_specs=[pl.BlockSpec((1, rows, d), lambda b, j, cnt: (b, j, 0)),
                       pl.BlockSpec((1, rows, d), lambda b, j, cnt: (b, j, 0))]),
        out_shape=[jax.ShapeDtypeStruct((b_sz, t_len, d), F32),
                   jax.ShapeDtypeStruct((b_sz, t_len, d), norm_dtype)],
        compiler_params=_params(2),
        name="moe_scatter_add",
    )(cnt, x3, y, enc, norm_g.reshape(1, d))


def ec_moe_layer(x3, h2, aff2, w1, w3, w2, layer, next_norm_g, next_norm_dtype):
    b_sz, t_len, d = x3.shape
    n_exp = aff2.shape[-1]
    cap = EC_FACTOR * t_len // n_exp
    enc, encrow, affrow, cnt = moe_select(aff2.reshape(b_sz, t_len, n_exp), cap)
    xs, gate = moe_gather(h2.reshape(b_sz, t_len, d), encrow, affrow, cnt, cap)
    y = moe_ffn(xs, gate, w1, w3, w2, layer)
    return moe_scatter_add(x3, y, enc, cnt, next_norm_g, next_norm_dtype, cap)


def _layer_cols(w, layer, col0, n_cols):
    return lax.slice(w, (layer, 0, col0), (layer + 1, w.shape[1], col0 + n_cols))[0]


def ssd_mixer_layer(h2, kv3, b_sz, t_len, j, ssd_w_in, conv_w, conv_b, dt_bias, a_log, d_skip, gate_g):
    m, d = h2.shape
    n_heads = d_skip.shape[-1]
    d_ssd = n_heads * SSD_HEADDIM
    gn = SSD_GROUPS * SSD_STATE
    conv_dim = d_ssd + 2 * gn
    d_xa = XA_HEADS * XA_DH
    n_main = d_ssd + conv_dim
    tn = 1024
    proj = matmul_ws([h2], [(ssd_w_in, (1, d, tn), lambda n: (j, 0, n))], n_main,
                     tn=tn, tm=1024, out_dtype=F32, name="ssd_in_proj")
    w_dt = _layer_cols(ssd_w_in, j, n_main, 2 * n_heads)
    w_dt = jnp.pad(w_dt.reshape(d, 2, n_heads), ((0, 0), (0, 0), (0, LANES - n_heads))).reshape(d, 2 * LANES)
    w_tail = jnp.concatenate([_layer_cols(ssd_w_in, j, n_main + 2 * n_heads, d_xa), w_dt], axis=1)
    n_tail = d_xa + 2 * LANES
    tail = matmul_ws([h2], [(w_tail, (d, n_tail), lambda n: (0, 0))], n_tail,
                     tn=n_tail, tm=512, out_dtype=F32, name="ssd_tail_proj").reshape(b_sz, t_len, n_tail)

    proj3 = proj.reshape(b_sz, t_len, n_main)
    xbc = conv_silu(proj3, conv_w, conv_b, j, d_ssd, conv_dim)
    per_head = (dt_bias.shape[0] * 2, n_heads, SSD_CHUNK)
    bias_t = jnp.broadcast_to(dt_bias.reshape(-1, n_heads, 1), per_head)
    alog_t = jnp.broadcast_to(a_log.reshape(-1, n_heads, 1), per_head)
    dt_col0 = d_xa // LANES
    y_fwd = ssd_scan(xbc, tail, bias_t, alog_t, j, n_heads=n_heads, dt_col0=dt_col0)
    y = ssd_scan(xbc, tail, bias_t, alog_t, j, n_heads=n_heads, dt_col0=dt_col0,
                 epilogue=(y_fwd, proj3, jnp.repeat(d_skip[j], SSD_HEADDIM).reshape(1, d_ssd),
                           gate_g[j].reshape(1, d_ssd))).reshape(m, d_ssd)
    o_x = mem_attention(tail, 0, kv3).reshape(m, d_xa)
    return y, o_x


def na_mixer_layer(h2, kv3, b_sz, t_len, j, na_w_in, na_rpb):
    m, d = h2.shape
    n_heads = na_rpb.shape[1]
    d_na = n_heads * NA_DH
    d_xa = XA_HEADS * XA_DH
    n_in = 3 * d_na + d_xa
    tn = 1024
    proj = matmul_ws([h2], [(na_w_in, (1, d, tn), lambda n: (j, 0, n))], n_in,
                     tn=tn, tm=1024, out_dtype=BF16, name="na_in_proj")
    proj3 = proj.reshape(b_sz, t_len, n_in)
    rpb = na_rpb[j]
    table = na_bias_table(jnp.pad(rpb, ((0, 0), (0, 0), (0, LANES - rpb.shape[-1]))))
    o_na = na_attention(proj3, table, n_heads).reshape(m, d_na)
    o_x = mem_attention(proj3, 3 * d_na // d_xa, kv3).reshape(m, d_xa)
    return o_na, o_x


def kernel(x, mem, norm_mix_g, norm_ffn_g, norm_final_g, mem_norm_g, ssd_w_in, ssd_conv_w, ssd_conv_b,
           ssd_dt_bias, ssd_a_log, ssd_d, ssd_gate_norm_g, ssd_w_out, na_w_in, na_rpb, na_w_out, xa_w_kv,
           moe_w_router, moe_w1, moe_w3, moe_w2):
    b_sz, t_len, d = x.shape
    mem_len = mem.shape[1]
    depth = norm_mix_g.shape[0]
    m = b_sz * t_len
    d_kv = xa_w_kv.shape[-1]
    mem_n = rmsnorm_rows(mem.reshape(b_sz * mem_len, d), mem_norm_g, BF16)
    x2 = x.reshape(m, d)
    h2 = rmsnorm_rows(x2, norm_mix_g[0], BF16)
    for i in range(depth):
        j = i // N_MIXERS
        kv3 = matmul_ws([mem_n], [(xa_w_kv, (1, d, d_kv), lambda n, i=i: (i, 0, 0))], d_kv,
                        tn=d_kv, tm=512, out_dtype=BF16, name="xa_kv_proj").reshape(b_sz, mem_len, d_kv)
        if i % N_MIXERS == 0:
            heads, o_x = ssd_mixer_layer(h2, kv3, b_sz, t_len, j, ssd_w_in, ssd_conv_w, ssd_conv_b,
                                         ssd_dt_bias, ssd_a_log, ssd_d, ssd_gate_norm_g)
            w_out = ssd_w_out
        else:
            heads, o_x = na_mixer_layer(h2, kv3, b_sz, t_len, j, na_w_in, na_rpb)
            w_out = na_w_out
        x2, h_ffn, aff2 = out_proj_router(heads, o_x, w_out, j, x2, norm_ffn_g[i], moe_w_router, i)
        last = i == depth - 1
        x3, normed = ec_moe_layer(x2.reshape(b_sz, t_len, d), h_ffn, aff2, moe_w1, moe_w3, moe_w2, i,
                                  norm_final_g if last else norm_mix_g[i + 1], F32 if last else BF16)
        x2, h2 = x3.reshape(m, d), normed.reshape(m, d)
    return normed
```

```python
import functools

import jax
import jax.numpy as jnp
from jax import lax
from jax.experimental import pallas as pl
from jax.experimental.pallas import tpu as pltpu

F32 = jnp.float32
BF16 = jnp.bfloat16
I32 = jnp.int32
HIGHEST = lax.Precision.HIGHEST

EPS = 1e-6
LANES = 128
VMEM_LIMIT_BYTES = 56 * 1024 * 1024

XA_HEADS = 4
XA_DH = 128
SSD_HEADDIM = 64
SSD_GROUPS = 4
SSD_STATE = 128
SSD_CONV = 5
SSD_CHUNK = 128
NA_DH = 128
NA_KR = 8
NA_KC = 16
GRID_W = 64
EC_FACTOR = 2
N_MIXERS = 2


def _params(n_grid_dims):
    return pltpu.CompilerParams(
        dimension_semantics=("arbitrary",) * n_grid_dims,
        vmem_limit_bytes=VMEM_LIMIT_BYTES)


def _nt_dot(a, b):
    return lax.dot_general(a, b, (((1,), (1,)), ((), ())), preferred_element_type=F32)


def _rmsnorm_body(x_ref, g_ref, o_ref):
    x = x_ref[...]
    ms = jnp.mean(x * x, axis=-1, keepdims=True)
    o_ref[...] = (x * lax.rsqrt(ms + EPS) * g_ref[...]).astype(o_ref.dtype)


def rmsnorm_rows(x2d, g, out_dtype, tm=256):
    m, d = x2d.shape
    return pl.pallas_call(
        _rmsnorm_body,
        grid=(m // tm,),
        in_specs=[pl.BlockSpec((tm, d), lambda i: (i, 0)),
                  pl.BlockSpec((1, d), lambda i: (0, 0))],
        out_specs=pl.BlockSpec((tm, d), lambda i: (i, 0)),
        out_shape=jax.ShapeDtypeStruct((m, d), out_dtype),
        compiler_params=_params(1),
        name="rmsnorm",
    )(x2d, g.reshape(1, d))


def _matmul_body(*refs, n_in, has_res):
    a_refs = refs[:n_in]
    w_refs = refs[n_in:2 * n_in]
    res_ref = refs[2 * n_in] if has_res else None
    o_ref = refs[2 * n_in + has_res]
    wbf_refs = refs[2 * n_in + has_res + 1:]

    @pl.when(pl.program_id(1) == 0)
    def _():
        for w_ref, wbf_ref in zip(w_refs, wbf_refs):
            lead = (0,) * (len(w_ref.shape) - 2)
            wbf_ref[...] = w_ref[lead + (slice(None), slice(None))].astype(BF16)

    acc = None
    for a_ref, wbf_ref in zip(a_refs, wbf_refs):
        t = jnp.dot(a_ref[...], wbf_ref[...], preferred_element_type=F32)
        acc = t if acc is None else acc + t
    if has_res:
        acc = acc + res_ref[...]
    o_ref[...] = acc.astype(o_ref.dtype)


def matmul_ws(a_list, w_list, n_cols, *, tn, tm, out_dtype, res=None, name="matmul"):
    m = a_list[0].shape[0]
    n_in = len(a_list)
    in_specs = [pl.BlockSpec((tm, a.shape[1]), lambda n, i: (i, 0)) for a in a_list]
    scratch = []
    for _, blk, imap in w_list:
        in_specs.append(pl.BlockSpec(blk, lambda n, i, imap=imap: imap(n)))
        scratch.append(pltpu.VMEM(blk[-2:], BF16))
    args = list(a_list) + [w for w, _, _ in w_list]
    if res is not None:
        in_specs.append(pl.BlockSpec((tm, tn), lambda n, i: (i, n)))
        args.append(res)
    return pl.pallas_call(
        functools.partial(_matmul_body, n_in=n_in, has_res=res is not None),
        grid=(n_cols // tn, m // tm),
        in_specs=in_specs,
        out_specs=pl.BlockSpec((tm, tn), lambda n, i: (i, n)),
        out_shape=jax.ShapeDtypeStruct((m, n_cols), out_dtype),
        scratch_shapes=scratch,
        compiler_params=_params(2),
        name=name,
    )(*args)


def _conv_silu_body(u_ref, w_ref, b_ref, o_ref, pad_ref, *, t_len, k_conv, rows):
    halo = 8
    ch = u_ref.shape[-1]
    pad_ref[0:halo, :] = jnp.zeros((halo, ch), F32)
    pad_ref[t_len + halo:t_len + 2 * halo, :] = jnp.zeros((halo, ch), F32)
    pad_ref[halo:t_len + halo, :] = u_ref[0]
    w = w_ref[0]
    b = b_ref[0]
    first = halo - k_conv // 2

    def body(i, carry):
        r0 = pl.multiple_of(i * rows, rows)
        win = pad_ref[pl.ds(r0, rows + 2 * halo), :]
        n_win = rows + 2 * halo
        acc = b
        for k in range(k_conv):
            off = first + k
            tap = win if off % 8 == 0 else pltpu.roll(win, n_win - off % 8, 0)
            lo = off - off % 8
            acc = acc + w[k:k + 1, :] * tap[lo:lo + rows]
        o_ref[0, pl.ds(r0, rows), :] = acc * jax.nn.sigmoid(acc)
        return carry

    lax.fori_loop(0, t_len // rows, body, 0)


def conv_silu(proj3, conv_w, conv_b, layer, col0, n_ch, tc=512, rows=128):
    b_sz, t_len, _ = proj3.shape
    k_conv = conv_w.shape[1]
    c0 = col0 // tc
    return pl.pallas_call(
        functools.partial(_conv_silu_body, t_len=t_len, k_conv=k_conv, rows=rows),
        grid=(b_sz, n_ch // tc),
        in_specs=[pl.BlockSpec((1, t_len, tc), lambda b, j: (b, 0, c0 + j)),
                  pl.BlockSpec((1, k_conv, tc), lambda b, j: (layer, 0, j)),
                  pl.BlockSpec((1, 1, tc), lambda b, j: (layer, 0, j))],
        out_specs=pl.BlockSpec((1, t_len, tc), lambda b, j: (b, 0, j)),
        out_shape=jax.ShapeDtypeStruct((b_sz, t_len, n_ch), F32),
        scratch_shapes=[pltpu.VMEM((t_len + 16, tc), F32)],
        compiler_params=_params(2),
        name="ssd_conv_silu",
    )(proj3, conv_w, conv_b.reshape(conv_b.shape[0], 1, conv_b.shape[1]))


def _softplus(x):
    return jnp.maximum(x, 0.0) + jnp.log1p(jnp.exp(-jnp.abs(x)))


SSD_BATCH_PER_STEP = 4


def _ssd_body(*refs, chunk, n_heads, head_dim, n_groups, backward):
    if backward:
        (xs_ref, b_ref, c_ref, dt_ref, bias_ref, alog_ref, yf_ref, z_ref, dskip_ref, gnorm_ref,
         y_ref, state_ref) = refs
    else:
        xs_ref, b_ref, c_ref, dt_ref, bias_ref, alog_ref, y_ref, state_ref = refs
    L = chunk
    hpg = n_heads // n_groups
    gw = hpg * head_dim

    @pl.when(pl.program_id(1) == 0)
    def _():
        state_ref[...] = jnp.zeros_like(state_ref)

    row = lax.broadcasted_iota(I32, (L, L), 0)
    col = lax.broadcasted_iota(I32, (L, L), 1)
    keep = row <= col if backward else row >= col
    keep_t = col <= row if backward else col >= row
    edge = 0 if backward else L - 1
    lo = lax.broadcasted_iota(I32, (L, LANES), 1) < head_dim
    lo_row = lo[0:1, :]
    for bi, g in [(bi, g) for bi in range(xs_ref.shape[0]) for g in range(n_groups)]:
        if g == 0:
            dt_t = _softplus(dt_ref[bi].T[0:n_heads, :] + bias_ref[0])
            da_t = dt_t * (-jnp.exp(alog_ref[0]))
            x_t = jnp.dot(da_t, keep_t.astype(F32), precision=HIGHEST, preferred_element_type=F32)
            w_t = dt_t * jnp.exp(x_t[:, edge:edge + 1] - x_t)
            x_c = jnp.concatenate([x_t, jnp.zeros((LANES - n_heads, L), F32)], axis=0).T
            e_tot = jnp.exp(x_c[edge:edge + 1, :])
        b_gt = b_ref[bi, :, g * SSD_STATE:(g + 1) * SSD_STATE].T
        c_g = c_ref[bi, :, g * SSD_STATE:(g + 1) * SSD_STATE].astype(BF16)
        cb = jnp.dot(c_g, b_gt.astype(BF16), preferred_element_type=F32)
        y_off = jnp.dot(c_g, state_ref[bi, g].astype(BF16), preferred_element_type=F32)
        y_pairs = []
        for pp in range(hpg // 2):
            h0 = g * hpg + 2 * pp
            h1 = h0 + 1
            c0 = h0 * head_dim
            lc = 2 * pp * head_dim
            xs_pair = xs_ref[bi, :, c0:c0 + LANES]
            rhs = jnp.concatenate([jnp.where(lo, xs_pair, 0.0).astype(BF16),
                                   jnp.where(lo, 0.0, xs_pair).astype(BF16)], axis=0)
            xc0 = jnp.broadcast_to(x_c[:, h0:h0 + 1], (L, L))
            xc1 = jnp.broadcast_to(x_c[:, h1:h1 + 1], (L, L))
            m0 = cb * jnp.exp(jnp.where(keep, xc0 - x_t[h0:h0 + 1, :], -jnp.inf)) * dt_t[h0:h0 + 1, :]
            m1 = cb * jnp.exp(jnp.where(keep, xc1 - x_t[h1:h1 + 1, :], -jnp.inf)) * dt_t[h1:h1 + 1, :]
            y_diag = jnp.dot(jnp.concatenate([m0.astype(BF16), m1.astype(BF16)], axis=1), rhs,
                             preferred_element_type=F32)
            e_out = jnp.where(lo, jnp.exp(xc0), jnp.exp(xc1))
            y_pair = y_diag + y_off[:, lc:lc + LANES] * e_out
            if backward:
                y_pairs.append(y_pair)
            else:
                y_ref[bi, :, c0:c0 + LANES] = y_pair
            lhs_b = jnp.concatenate([(b_gt * w_t[h0:h0 + 1, :]).astype(BF16),
                                     (b_gt * w_t[h1:h1 + 1, :]).astype(BF16)], axis=1)
            dec = jnp.where(lo_row, e_tot[:, h0:h0 + 1], e_tot[:, h1:h1 + 1])
            state_ref[bi, g, :, lc:lc + LANES] = (state_ref[bi, g, :, lc:lc + LANES] * dec
                                                  + jnp.dot(lhs_b, rhs, preferred_element_type=F32))
        if backward:
            cols = slice(g * gw, (g + 1) * gw)
            y = (yf_ref[bi, :, cols] + jnp.concatenate(y_pairs, axis=1)
                 + dskip_ref[:, cols] * xs_ref[bi, :, cols])
            z = z_ref[bi, :, cols]
            y = y * (z * jax.nn.sigmoid(z))
            ms = jnp.mean(y * y, axis=-1, keepdims=True)
            y_ref[bi, :, cols] = (y * lax.rsqrt(ms + EPS) * gnorm_ref[:, cols]).astype(y_ref.dtype)


def ssd_scan(xbc, dt_raw, dt_bias, a_log, layer, *, n_heads, dt_col0=0, epilogue=None):
    b_sz, t_len, _ = xbc.shape
    d_ssd = n_heads * SSD_HEADDIM
    gn = SSD_GROUPS * SSD_STATE
    L = SSD_CHUNK
    assert L == LANES and 2 * SSD_HEADDIM == LANES and (n_heads // SSD_GROUPS) % 2 == 0
    nc = t_len // L
    xb = d_ssd // gn
    backward = epilogue is not None
    direction = int(backward)

    def chunk_of(c):
        return nc - 1 - c if backward else c

    nb = SSD_BATCH_PER_STEP if b_sz % SSD_BATCH_PER_STEP == 0 else 1
    in_specs = [pl.BlockSpec((nb, L, d_ssd), lambda b, c: (b, chunk_of(c), 0)),
                pl.BlockSpec((nb, L, gn), lambda b, c: (b, chunk_of(c), xb)),
                pl.BlockSpec((nb, L, gn), lambda b, c: (b, chunk_of(c), xb + 1)),
                pl.BlockSpec((nb, L, LANES), lambda b, c: (b, chunk_of(c), dt_col0 + direction)),
                pl.BlockSpec((1, n_heads, L), lambda b, c: (layer * 2 + direction, 0, 0)),
                pl.BlockSpec((1, n_heads, L), lambda b, c: (layer * 2 + direction, 0, 0))]
    args = [xbc, xbc, xbc, dt_raw, dt_bias, a_log]
    if backward:
        in_specs += [pl.BlockSpec((nb, L, d_ssd), lambda b, c: (b, chunk_of(c), 0)),
                     pl.BlockSpec((nb, L, d_ssd), lambda b, c: (b, chunk_of(c), 0)),
                     pl.BlockSpec((1, d_ssd), lambda b, c: (0, 0)),
                     pl.BlockSpec((1, d_ssd), lambda b, c: (0, 0))]
        args += list(epilogue)
    return pl.pallas_call(
        functools.partial(_ssd_body, chunk=L, n_heads=n_heads, head_dim=SSD_HEADDIM,
                          n_groups=SSD_GROUPS, backward=backward),
        grid=(b_sz // nb, nc),
        in_specs=in_specs,
        out_specs=pl.BlockSpec((nb, L, d_ssd), lambda b, c: (b, chunk_of(c), 0)),
        out_shape=jax.ShapeDtypeStruct((b_sz, t_len, d_ssd), BF16 if backward else F32),
        scratch_shapes=[pltpu.VMEM((nb, SSD_GROUPS, SSD_STATE, d_ssd // SSD_GROUPS), F32)],
        compiler_params=_params(2),
        name="ssd_scan_bwd_gate" if backward else "ssd_scan_fwd",
    )(*args)


def _softmax_rows(s):
    m = jnp.max(s, axis=-1, keepdims=True)
    e = jnp.exp(s - m)
    return e / jnp.sum(e, axis=-1, keepdims=True)


def _mem_attn_body(q_ref, kv_ref, o_ref, *, n_heads, dh):
    scale = dh ** -0.5
    for h in range(n_heads):
        q = q_ref[0, :, h * dh:(h + 1) * dh].astype(BF16)
        k = kv_ref[0, :, h * dh:(h + 1) * dh]
        v = kv_ref[0, :, (n_heads + h) * dh:(n_heads + h + 1) * dh]
        p = _softmax_rows(_nt_dot(q, k) * scale).astype(BF16)
        o_ref[0, :, h * dh:(h + 1) * dh] = jnp.dot(p, v, preferred_element_type=F32).astype(o_ref.dtype)


def mem_attention(q3, q_col_block, kv3, tm=512):
    b_sz, t_len, _ = q3.shape
    d_xa = XA_HEADS * XA_DH
    mem_len = kv3.shape[1]
    return pl.pallas_call(
        functools.partial(_mem_attn_body, n_heads=XA_HEADS, dh=XA_DH),
        grid=(b_sz, t_len // tm),
        in_specs=[pl.BlockSpec((1, tm, d_xa), lambda b, i: (b, i, q_col_block)),
                  pl.BlockSpec((1, mem_len, 2 * d_xa), lambda b, i: (b, 0, 0))],
        out_specs=pl.BlockSpec((1, tm, d_xa), lambda b, i: (b, i, 0)),
        out_shape=jax.ShapeDtypeStruct((b_sz, t_len, d_xa), BF16),
        compiler_params=_params(2),
        name="mem_attention",
    )(q3, kv3)


NA_QROWS = 4
NA_KROWS = 12
NEG_MASK = -1e30


def _na_bias_body(rpb_ref, o_ref, *, n_dr, width, kc):
    c_idx = lax.broadcasted_iota(I32, (width, LANES), 0)
    lane = lax.broadcasted_iota(I32, (width, LANES), 1)
    first = lane < width
    k_idx = jnp.where(first, lane, lane - width)
    w_start = jnp.clip(c_idx - kc // 2, 0, width - kc)
    valid = (k_idx >= w_start) & (k_idx < w_start + kc)
    base = LANES - (kc - 1)
    neg = jnp.full((width, LANES), NEG_MASK, F32)
    lo_half, hi_half = [], []
    for d in range(n_dr):
        x = jnp.broadcast_to(rpb_ref[0, d:d + 1, :], (width, LANES))
        lo_half.append(jnp.where(valid & first, pltpu.roll(x, base, 1, stride=1, stride_axis=0), neg))
        hi_half.append(jnp.where(valid & jnp.logical_not(first),
                                 pltpu.roll(x, (base + width) % LANES, 1, stride=1, stride_axis=0), neg))
    for d in range(n_dr - 1):
        o_ref[0, d] = jnp.where(first, lo_half[d], hi_half[d + 1])
    for d in range(n_dr):
        o_ref[0, n_dr - 1 + d] = lo_half[d]
        o_ref[0, 2 * n_dr - 1 + d] = hi_half[d]
    o_ref[0, 3 * n_dr - 1] = neg


def na_bias_table(rpb_padded):
    n_heads, n_dr, _ = rpb_padded.shape
    n_ent = 3 * n_dr
    return pl.pallas_call(
        functools.partial(_na_bias_body, n_dr=n_dr, width=GRID_W, kc=NA_KC),
        grid=(n_heads,),
        in_specs=[pl.BlockSpec((1, n_dr, LANES), lambda h: (h, 0, 0))],
        out_specs=pl.BlockSpec((1, n_ent, GRID_W, 2 * GRID_W), lambda h: (h, 0, 0, 0)),
        out_shape=jax.ShapeDtypeStruct((n_heads, n_ent, GRID_W, 2 * GRID_W), F32),
        compiler_params=_params(1),
        name="na_bias_table",
    )(rpb_padded)


def _na_table_entry(r, ka, rows, kr, n_dr):
    rs = min(max(r - kr // 2, 0), rows - kr)
    in_a = rs <= ka < rs + kr
    in_b = rs <= ka + 1 < rs + kr
    d_a = ka - r + (NA_KR - 1)
    if in_a and in_b:
        return d_a
    if in_a:
        return n_dr - 1 + d_a
    if in_b:
        return 2 * n_dr - 1 + d_a + 1
    return 3 * n_dr - 1


def _na_body(q_ref, k_ref, v_ref, tb_ref, o_ref, *, rows, width, kr, dh):
    scale = dh ** -0.5
    n_dr = 2 * NA_KR - 1
    nq = NA_QROWS * width
    nk = NA_KROWS * width
    for blk in range(rows // NA_QROWS):
        r0 = blk * NA_QROWS
        k_row0 = min(max(r0 - kr // 2, 0), rows - NA_KROWS)
        q = q_ref[0, r0 * width:r0 * width + nq, :]
        kw = k_ref[0, k_row0 * width:k_row0 * width + nk, :]
        vw = v_ref[0, k_row0 * width:k_row0 * width + nk, :]
        s = _nt_dot(q, kw)
        e_rows, dens = [], []
        for i in range(NA_QROWS):
            entries = [_na_table_entry(r0 + i, k_row0 + 2 * j, rows, kr, n_dr) for j in range(nk // LANES)]
            live = [j for j, ent in enumerate(entries) if ent != 3 * n_dr - 1]
            tiles = {j: s[i * width:(i + 1) * width, j * LANES:(j + 1) * LANES] * scale + tb_ref[0, entries[j]]
                     for j in live}
            m = jnp.max(functools.reduce(jnp.maximum, tiles.values()), axis=-1, keepdims=True)
            e_tiles = {j: jnp.exp(t - m) for j, t in tiles.items()}
            dens.append(jnp.sum(functools.reduce(jnp.add, e_tiles.values()), axis=-1, keepdims=True))
            e_rows.append(jnp.concatenate(
                [e_tiles[j].astype(BF16) if j in e_tiles else jnp.zeros((width, LANES), BF16)
                 for j in range(nk // LANES)], axis=1))
        o = (jnp.dot(jnp.concatenate(e_rows, axis=0), vw, preferred_element_type=F32)
             / jnp.concatenate(dens, axis=0))
        o_ref[0, r0 * width:r0 * width + nq, :] = o.astype(o_ref.dtype)


def na_attention(proj3, table, n_heads):
    b_sz, t_len, _ = proj3.shape
    rows = t_len // GRID_W
    kr = min(NA_KR, rows)
    assert kr == NA_KR and rows % NA_QROWS == 0 and rows >= NA_KROWS and (rows - NA_KROWS) % 2 == 0
    assert NA_KROWS >= kr + NA_QROWS - 1 and 2 * GRID_W == LANES
    n_dr2 = table.shape[1]
    return pl.pallas_call(
        functools.partial(_na_body, rows=rows, width=GRID_W, kr=kr, dh=NA_DH),
        grid=(b_sz, n_heads),
        in_specs=[pl.BlockSpec((1, t_len, NA_DH), lambda b, h: (b, 0, h)),
                  pl.BlockSpec((1, t_len, NA_DH), lambda b, h: (b, 0, n_heads + h)),
                  pl.BlockSpec((1, t_len, NA_DH), lambda b, h: (b, 0, 2 * n_heads + h)),
                  pl.BlockSpec((1, n_dr2, GRID_W, 2 * GRID_W), lambda b, h: (h, 0, 0, 0))],
        out_specs=pl.BlockSpec((1, t_len, NA_DH), lambda b, h: (b, 0, h)),
        out_shape=jax.ShapeDtypeStruct((b_sz, t_len, n_heads * NA_DH), BF16),
        compiler_params=_params(2),
        name="na_attention",
    )(proj3, proj3, proj3, table)


def _norm_and_route(x, g_ref, w_ref, h_ref, aff_ref):
    ms = jnp.mean(x * x, axis=-1, keepdims=True)
    hn = x * lax.rsqrt(ms + EPS) * g_ref[...]
    hn_hi = hn.astype(BF16)
    h_ref[...] = hn_hi
    hn_lo = (hn - hn_hi.astype(F32)).astype(BF16)
    w = w_ref[0]
    w_hi = w.astype(BF16)
    w_lo = (w - w_hi.astype(F32)).astype(BF16)
    n_exp = w.shape[1]
    both = jnp.dot(hn_hi, jnp.concatenate([w_hi, w_lo], axis=1), preferred_element_type=F32)
    logits = both[:, :n_exp] + (both[:, n_exp:] + jnp.dot(hn_lo, w_hi, preferred_element_type=F32))
    aff_ref[...] = _softmax_rows(logits)


def _out_proj_router_body(a1_ref, a2_ref, w1_ref, w2_ref, res_ref, g_ref, wr_ref,
                          x_ref, h_ref, aff_ref, wbf1_ref, wbf2_ref, *, sub_rows):
    @pl.when(pl.program_id(0) == 0)
    def _():
        wbf1_ref[...] = w1_ref[0].astype(BF16)
        wbf2_ref[...] = w2_ref[0].astype(BF16)

    for r in range(0, a1_ref.shape[0], sub_rows):
        rows = slice(r, r + sub_rows)
        acc = (jnp.dot(a1_ref[rows, :], wbf1_ref[...], preferred_element_type=F32)
               + jnp.dot(a2_ref[rows, :], wbf2_ref[...], preferred_element_type=F32))
        x = acc + res_ref[rows, :]
        x_ref[rows, :] = x
        _norm_and_route(x, g_ref, wr_ref, h_ref.at[rows, :], aff_ref.at[rows, :])


def out_proj_router(a1, a2, w_out, layer_w, res, norm_g, w_router, layer, tm=512, sub_rows=256):
    m, k1 = a1.shape
    k2 = a2.shape[1]
    d = w_out.shape[-1]
    n_exp = w_router.shape[-1]
    assert k1 % k2 == 0
    once = pl.Buffered(1)
    return pl.pallas_call(
        functools.partial(_out_proj_router_body, sub_rows=sub_rows),
        grid=(m // tm,),
        in_specs=[pl.BlockSpec((tm, k1), lambda i: (i, 0)),
                  pl.BlockSpec((tm, k2), lambda i: (i, 0)),
                  pl.BlockSpec((1, k1, d), lambda i: (layer_w, 0, 0), pipeline_mode=once),
                  pl.BlockSpec((1, k2, d), lambda i: (layer_w, k1 // k2, 0), pipeline_mode=once),
                  pl.BlockSpec((tm, d), lambda i: (i, 0)),
                  pl.BlockSpec((1, d), lambda i: (0, 0)),
                  pl.BlockSpec((1, d, n_exp), lambda i: (layer, 0, 0))],
        out_specs=[pl.BlockSpec((tm, d), lambda i: (i, 0)),
                   pl.BlockSpec((tm, d), lambda i: (i, 0)),
                   pl.BlockSpec((tm, n_exp), lambda i: (i, 0))],
        out_shape=[jax.ShapeDtypeStruct((m, d), F32),
                   jax.ShapeDtypeStruct((m, d), BF16),
                   jax.ShapeDtypeStruct((m, n_exp), F32)],
        scratch_shapes=[pltpu.VMEM((k1, d), BF16), pltpu.VMEM((k2, d), BF16)],
        compiler_params=_params(1),
        name="out_proj_router",
    )(a1, a2, w_out, w_out, res, norm_g.reshape(1, d), w_router)


TOKEN_BLK = 128
SLOT_WIN = 64
SLOT_ALIGN_LOG2 = 4


def _slot_window(c0, c1, cap):
    st = jnp.minimum(lax.shift_left(lax.shift_right_logical(c0, SLOT_ALIGN_LOG2), SLOT_ALIGN_LOG2),
                     cap - SLOT_WIN)
    return st, c1 <= st + SLOT_WIN


def _select_body(aff_ref, enc_ref, encrow_ref, affrow_ref, cnt_ref, *, cap):
    a = aff_ref[0]
    t_len, n_exp = a.shape
    zero = jnp.zeros((1, n_exp), I32)

    def count(pred):
        return jnp.sum(pred.astype(I32), axis=0, keepdims=True)

    def value_bit(i, prefix):
        cand = prefix | jnp.left_shift(jnp.int32(1), 30 - i)
        return jnp.where(count(a >= lax.bitcast_convert_type(cand, F32)) >= cap, cand, prefix)

    thresh = lax.bitcast_convert_type(lax.fori_loop(0, 31, value_bit, zero), F32)
    above = a > thresh
    tied = a == thresh
    need = cap - count(above)
    idx = lax.broadcasted_iota(I32, (t_len, n_exp), 0)
    n_idx_bits = (t_len - 1).bit_length()

    def index_bit(i, last):
        cand = last | jnp.left_shift(jnp.int32(1), n_idx_bits - 1 - i)
        return jnp.where(count(tied & (idx < cand)) < need, cand, last)

    last = lax.fori_loop(0, n_idx_bits, index_bit, zero)
    mask = above | (tied & (idx <= last))
    mask_bf = mask.astype(BF16)

    n_rows = cnt_ref.shape[1]
    bound = lax.broadcasted_iota(I32, (n_rows, t_len), 0) * TOKEN_BLK
    tok = lax.broadcasted_iota(I32, (n_rows, t_len), 1)
    cnt = jnp.dot((tok < bound).astype(BF16), mask_bf, preferred_element_type=F32)
    cnt_ref[0] = cnt.astype(I32)
    earlier = (lax.broadcasted_iota(I32, (TOKEN_BLK, TOKEN_BLK), 0)
               > lax.broadcasted_iota(I32, (TOKEN_BLK, TOKEN_BLK), 1)).astype(BF16)
    for i in range(t_len // TOKEN_BLK):
        rows = slice(i * TOKEN_BLK, (i + 1) * TOKEN_BLK)
        before = jnp.dot(earlier, mask_bf[rows], preferred_element_type=F32) + cnt[i:i + 1, :]
        enc_ref[0, rows, :] = jnp.where(mask[rows], before + 1.0, 0.0).astype(I32)

    eye = (lax.broadcasted_iota(I32, (n_exp, n_exp), 0)
           == lax.broadcasted_iota(I32, (n_exp, n_exp), 1)).astype(BF16)
    enc_bf = enc_ref[0].astype(F32).astype(BF16)
    encrow_ref[0] = _nt_dot(eye, enc_bf).astype(I32)

    a_hi = a.astype(BF16)
    rest = a - a_hi.astype(F32)
    a_mid = rest.astype(BF16)
    a_lo = (rest - a_mid.astype(F32)).astype(BF16)
    affrow_ref[0] = _nt_dot(eye, a_hi) + (_nt_dot(eye, a_mid) + _nt_dot(eye, a_lo))


def moe_select(aff3, cap):
    b_sz, t_len, n_exp = aff3.shape
    assert cap <= 256, "slot codes must stay exactly representable in bf16"
    n_rows = -(-(t_len // TOKEN_BLK + 1) // 8) * 8
    return pl.pallas_call(
        functools.partial(_select_body, cap=cap),
        grid=(b_sz,),
        in_specs=[pl.BlockSpec((1, t_len, n_exp), lambda b: (b, 0, 0))],
        out_specs=[pl.BlockSpec((1, t_len, n_exp), lambda b: (b, 0, 0)),
                   pl.BlockSpec((1, n_exp, t_len), lambda b: (b, 0, 0)),
                   pl.BlockSpec((1, n_exp, t_len), lambda b: (b, 0, 0)),
                   pl.BlockSpec((1, n_rows, n_exp), lambda b: (b, 0, 0))],
        out_shape=[jax.ShapeDtypeStruct((b_sz, t_len, n_exp), I32),
                   jax.ShapeDtypeStruct((b_sz, n_exp, t_len), I32),
                   jax.ShapeDtypeStruct((b_sz, n_exp, t_len), F32),
                   jax.ShapeDtypeStruct((b_sz, n_rows, n_exp), I32)],
        compiler_params=_params(1),
        name="moe_select",
    )(aff3)


GATHER_GROUP = 4


def _gather_body(cnt_ref, h_ref, encrow_ref, affrow_ref, xs_ref, g_ref, acc_ref, gacc_ref, *, cap, kb):
    b = pl.program_id(0)
    e0 = pl.program_id(1) * GATHER_GROUP
    t_len = h_ref.shape[1]
    per = kb // TOKEN_BLK
    codes = [encrow_ref[0, pl.ds(e0 + i, 1), :] for i in range(GATHER_GROUP)]
    gates = [affrow_ref[0, pl.ds(e0 + i, 1), :] for i in range(GATHER_GROUP)]
    starts, fits = [], None
    for j in range(t_len // kb):
        row = []
        for i in range(GATHER_GROUP):
            st, ok = _slot_window(cnt_ref[b, j * per, e0 + i], cnt_ref[b, (j + 1) * per, e0 + i], cap)
            row.append(pl.multiple_of(st, 1 << SLOT_ALIGN_LOG2))
            fits = ok if fits is None else jnp.logical_and(fits, ok)
        starts.append(row)

    @pl.when(fits)
    def _():
        acc_ref[...] = jnp.zeros_like(acc_ref)
        gacc_ref[...] = jnp.zeros_like(gacc_ref)
        sub = lax.broadcasted_iota(I32, (SLOT_WIN, kb), 0) + 1
        for j, row in enumerate(starts):
            tok = slice(j * kb, (j + 1) * kb)
            hits = [(sub + row[i]) == codes[i][:, tok] for i in range(GATHER_GROUP)]
            picked = jnp.dot(jnp.concatenate([hit.astype(BF16) for hit in hits], axis=0), h_ref[0, tok, :],
                             preferred_element_type=F32)
            for i in range(GATHER_GROUP):
                acc_ref[i, pl.ds(row[i], SLOT_WIN), :] += picked[i * SLOT_WIN:(i + 1) * SLOT_WIN]
                g = jnp.sum(jnp.where(hits[i], gates[i][:, tok], 0.0), axis=1, keepdims=True)
                gacc_ref[i, pl.ds(row[i], SLOT_WIN), :] += jnp.broadcast_to(g, (SLOT_WIN, LANES))
        xs_ref[...] = acc_ref[...].astype(BF16)
        g_ref[...] = gacc_ref[...]

    @pl.when(jnp.logical_not(fits))
    def _():
        slot = lax.broadcasted_iota(I32, (cap, t_len), 0) + 1
        for i in range(GATHER_GROUP):
            hit = slot == codes[i]
            xs_ref[i] = jnp.dot(hit.astype(BF16), h_ref[0], preferred_element_type=F32).astype(BF16)
            g = jnp.sum(jnp.where(hit, gates[i], 0.0), axis=1, keepdims=True)
            g_ref[i] = jnp.broadcast_to(g, (cap, LANES))


def moe_gather(h3, encrow, affrow, cnt, cap, kb=256):
    b_sz, t_len, d = h3.shape
    n_exp = encrow.shape[1]
    assert cap >= SLOT_WIN and t_len % kb == 0 and kb % TOKEN_BLK == 0 and n_exp % GATHER_GROUP == 0
    grp = GATHER_GROUP
    return pl.pallas_call(
        functools.partial(_gather_body, cap=cap, kb=kb),
        grid_spec=pltpu.PrefetchScalarGridSpec(
            num_scalar_prefetch=1,
            grid=(b_sz, n_exp // grp),
            in_specs=[pl.BlockSpec((1, t_len, d), lambda b, q, cnt: (b, 0, 0)),
                      pl.BlockSpec((1, n_exp, t_len), lambda b, q, cnt: (b, 0, 0)),
                      pl.BlockSpec((1, n_exp, t_len), lambda b, q, cnt: (b, 0, 0))],
            out_specs=[pl.BlockSpec((grp, cap, d), lambda b, q, cnt: (q, b, 0)),
                       pl.BlockSpec((grp, cap, LANES), lambda b, q, cnt: (q, b, 0))],
            scratch_shapes=[pltpu.VMEM((grp, cap, d), F32), pltpu.VMEM((grp, cap, LANES), F32)]),
        out_shape=[jax.ShapeDtypeStruct((n_exp, b_sz * cap, d), BF16),
                   jax.ShapeDtypeStruct((n_exp, b_sz * cap, LANES), F32)],
        compiler_params=_params(2),
        name="moe_gather",
    )(cnt, h3, encrow, affrow)


def _ffn_up_body(xs_ref, w1_ref, w3_ref, hid_ref):
    xs = xs_ref[0]
    a = jnp.dot(xs, w1_ref[0, 0].astype(BF16), preferred_element_type=F32)
    b = jnp.dot(xs, w3_ref[0, 0].astype(BF16), preferred_element_type=F32)
    hid_ref[0] = (a * jax.nn.sigmoid(a) * b).astype(hid_ref.dtype)


def _ffn_down_body(hid_ref, w2_ref, g_ref, y_ref):
    y = jnp.dot(hid_ref[0], w2_ref[0, 0].astype(BF16), preferred_element_type=F32)
    gate = jnp.concatenate([g_ref[0]] * (y.shape[1] // LANES), axis=1)
    y_ref[0] = (y * gate).astype(y_ref.dtype)


def moe_ffn(xs, gate, w1, w3, w2, layer, tf=512, tn=2048):
    n_exp, rows, d = xs.shape
    d_exp = w1.shape[-1]
    hid = pl.pallas_call(
        _ffn_up_body,
        grid=(n_exp, d_exp // tf),
        in_specs=[pl.BlockSpec((1, rows, d), lambda e, f: (e, 0, 0)),
                  pl.BlockSpec((1, 1, d, tf), lambda e, f: (layer, e, 0, f)),
                  pl.BlockSpec((1, 1, d, tf), lambda e, f: (layer, e, 0, f))],
        out_specs=pl.BlockSpec((1, rows, tf), lambda e, f: (e, 0, f)),
        out_shape=jax.ShapeDtypeStruct((n_exp, rows, d_exp), BF16),
        compiler_params=_params(2),
        name="moe_ffn_up",
    )(xs, w1, w3)
    return pl.pallas_call(
        _ffn_down_body,
        grid=(n_exp, d // tn),
        in_specs=[pl.BlockSpec((1, rows, d_exp), lambda e, n: (e, 0, 0)),
                  pl.BlockSpec((1, 1, d_exp, tn), lambda e, n: (layer, e, 0, n)),
                  pl.BlockSpec((1, rows, LANES), lambda e, n: (e, 0, 0))],
        out_specs=pl.BlockSpec((1, rows, tn), lambda e, n: (e, 0, n)),
        out_shape=jax.ShapeDtypeStruct((n_exp, rows, d), BF16),
        compiler_params=_params(2),
        name="moe_ffn_down",
    )(hid, w2, gate)


SCATTER_GROUP = 4


SCATTER_SUB = 2


def _scatter_body(cnt_ref, x_ref, y_ref, enc_ref, gn_ref, o_ref, hn_ref, *, cap):
    b = pl.program_id(0)
    j0 = pl.program_id(1) * SCATTER_SUB
    n_exp = enc_ref.shape[-1]
    starts, fits = [], None
    for u in range(SCATTER_SUB):
        row = []
        for e in range(n_exp):
            st, ok = _slot_window(cnt_ref[b, j0 + u, e], cnt_ref[b, j0 + u + 1, e], cap)
            row.append(pl.multiple_of(st, 1 << SLOT_ALIGN_LOG2))
            fits = ok if fits is None else jnp.logical_and(fits, ok)
        starts.append(row)

    def finish(rows, x_new):
        o_ref[0, rows, :] = x_new
        ms = jnp.mean(x_new * x_new, axis=-1, keepdims=True)
        hn_ref[0, rows, :] = (x_new * lax.rsqrt(ms + EPS) * gn_ref[...]).astype(hn_ref.dtype)

    @pl.when(fits)
    def _():
        width = SCATTER_GROUP * SLOT_WIN
        lane = lax.broadcasted_iota(I32, (TOKEN_BLK, width), 1)
        lane_row = lane[0:1, :]
        within = jnp.bitwise_and(lane_row, SLOT_WIN - 1) + 1
        for u in range(SCATTER_SUB):
            rows = slice(u * TOKEN_BLK, (u + 1) * TOKEN_BLK)
            enc = enc_ref[0, rows, :]
            onehots, windows = [], []
            for q in range(n_exp // SCATTER_GROUP):
                es = range(q * SCATTER_GROUP, (q + 1) * SCATTER_GROUP)
                code = enc[:, es[-1]:es[-1] + 1]
                first = starts[u][es[-1]]
                for k in reversed(range(SCATTER_GROUP - 1)):
                    code = jnp.where(lane < (k + 1) * SLOT_WIN, enc[:, es[k]:es[k] + 1], code)
                    first = jnp.where(lane_row < (k + 1) * SLOT_WIN, starts[u][es[k]], first)
                onehots.append((code == first + within).astype(BF16))
                windows += [y_ref[e, pl.ds(starts[u][e], SLOT_WIN), :] for e in es]
            finish(rows, x_ref[0, rows, :] + jnp.dot(
                jnp.concatenate(onehots, axis=1), jnp.concatenate(windows, axis=0),
                preferred_element_type=F32))

    @pl.when(jnp.logical_not(fits))
    def _():
        slot = lax.broadcasted_iota(I32, (TOKEN_BLK, cap), 1) + 1
        for u in range(SCATTER_SUB):
            rows = slice(u * TOKEN_BLK, (u + 1) * TOKEN_BLK)
            enc = enc_ref[0, rows, :]
            acc = x_ref[0, rows, :]
            for e in range(n_exp):
                onehot = (enc[:, e:e + 1] == slot).astype(BF16)
                acc = acc + jnp.dot(onehot, y_ref[e], preferred_element_type=F32)
            finish(rows, acc)


def moe_scatter_add(x3, y, enc, cnt, norm_g, norm_dtype, cap):
    b_sz, t_len, d = x3.shape
    n_exp = enc.shape[-1]
    assert cap >= SLOT_WIN and n_exp % SCATTER_GROUP == 0 and SLOT_WIN & (SLOT_WIN - 1) == 0
    rows = SCATTER_SUB * TOKEN_BLK
    assert t_len % rows == 0
    return pl.pallas_call(
        functools.partial(_scatter_body, cap=cap),
        grid_spec=pltpu.PrefetchScalarGridSpec(
            num_scalar_prefetch=1,
            grid=(b_sz, t_len // rows),
            in_specs=[pl.BlockSpec((1, rows, d), lambda b, j, cnt: (b, j, 0)),
                      pl.BlockSpec((n_exp, cap, d), lambda b, j, cnt: (0, b, 0)),
                      pl.BlockSpec((1, rows, n_exp), lambda b, j, cnt: (b, j, 0)),
                      pl.BlockSpec((1, d), lambda b, j, cnt: (0, 0))],
            out_specs=[pl.BlockSpec((1, rows, d), lambda b, j, cnt: (b, j, 0)),
                       pl.BlockSpec((1, rows, d), lambda b, j, cnt: (b, j, 0))]),
        out_shape=[jax.ShapeDtypeStruct((b_sz, t_len, d), F32),
                   jax.ShapeDtypeStruct((b_sz, t_len, d), norm_dtype)],
        compiler_params=_params(2),
        name="moe_scatter_add",
    )(cnt, x3, y, enc, norm_g.reshape(1, d))


def ec_moe_layer(x3, h2, aff2, w1, w3, w2, layer, next_norm_g, next_norm_dtype):
    b_sz, t_len, d = x3.shape
    n_exp = aff2.shape[-1]
    cap = EC_FACTOR * t_len // n_exp
    enc, encrow, affrow, cnt = moe_select(aff2.reshape(b_sz, t_len, n_exp), cap)
    xs, gate = moe_gather(h2.reshape(b_sz, t_len, d), encrow, affrow, cnt, cap)
    y = moe_ffn(xs, gate, w1, w3, w2, layer)
    return moe_scatter_add(x3, y, enc, cnt, next_norm_g, next_norm_dtype, cap)


def _layer_cols(w, layer, col0, n_cols):
    return lax.slice(w, (layer, 0, col0), (layer + 1, w.shape[1], col0 + n_cols))[0]


def ssd_mixer_layer(h2, kv3, b_sz, t_len, j, ssd_w_in, conv_w, conv_b, dt_bias, a_log, d_skip, gate_g):
    m, d = h2.shape
    n_heads = d_skip.shape[-1]
    d_ssd = n_heads * SSD_HEADDIM
    gn = SSD_GROUPS * SSD_STATE
    conv_dim = d_ssd + 2 * gn
    d_xa = XA_HEADS * XA_DH
    n_main = d_ssd + conv_dim
    tn = 1024
    proj = matmul_ws([h2], [(ssd_w_in, (1, d, tn), lambda n: (j, 0, n))], n_main,
                     tn=tn, tm=1024, out_dtype=F32, name="ssd_in_proj")
    w_dt = _layer_cols(ssd_w_in, j, n_main, 2 * n_heads)
    w_dt = jnp.pad(w_dt.reshape(d, 2, n_heads), ((0, 0), (0, 0), (0, LANES - n_heads))).reshape(d, 2 * LANES)
    w_tail = jnp.concatenate([_layer_cols(ssd_w_in, j, n_main + 2 * n_heads, d_xa), w_dt], axis=1)
    n_tail = d_xa + 2 * LANES
    tail = matmul_ws([h2], [(w_tail, (d, n_tail), lambda n: (0, 0))], n_tail,
                     tn=n_tail, tm=512, out_dtype=F32, name="ssd_tail_proj").reshape(b_sz, t_len, n_tail)

    proj3 = proj.reshape(b_sz, t_len, n_main)
    xbc = conv_silu(proj3, conv_w, conv_b, j, d_ssd, conv_dim)
    per_head = (dt_bias.shape[0] * 2, n_heads, SSD_CHUNK)
    bias_t = jnp.broadcast_to(dt_bias.reshape(-1, n_heads, 1), per_head)
    alog_t = jnp.broadcast_to(a_log.reshape(-1, n_heads, 1), per_head)
    dt_col0 = d_xa // LANES
    y_fwd = ssd_scan(xbc, tail, bias_t, alog_t, j, n_heads=n_heads, dt_col0=dt_col0)
    y = ssd_scan(xbc, tail, bias_t, alog_t, j, n_heads=n_heads, dt_col0=dt_col0,
                 epilogue=(y_fwd, proj3, jnp.repeat(d_skip[j], SSD_HEADDIM).reshape(1, d_ssd),
                           gate_g[j].reshape(1, d_ssd))).reshape(m, d_ssd)
    o_x = mem_attention(tail, 0, kv3).reshape(m, d_xa)
    return y, o_x


def na_mixer_layer(h2, kv3, b_sz, t_len, j, na_w_in, na_rpb):
    m, d = h2.shape
    n_heads = na_rpb.shape[1]
    d_na = n_heads * NA_DH
    d_xa = XA_HEADS * XA_DH
    n_in = 3 * d_na + d_xa
    tn = 1024
    proj = matmul_ws([h2], [(na_w_in, (1, d, tn), lambda n: (j, 0, n))], n_in,
                     tn=tn, tm=1024, out_dtype=BF16, name="na_in_proj")
    proj3 = proj.reshape(b_sz, t_len, n_in)
    rpb = na_rpb[j]
    table = na_bias_table(jnp.pad(rpb, ((0, 0), (0, 0), (0, LANES - rpb.shape[-1]))))
    o_na = na_attention(proj3, table, n_heads).reshape(m, d_na)
    o_x = mem_attention(proj3, 3 * d_na // d_xa, kv3).reshape(m, d_xa)
    return o_na, o_x


def kernel(x, mem, norm_mix_g, norm_ffn_g, norm_final_g, mem_norm_g, ssd_w_in, ssd_conv_w, ssd_conv_b,
           ssd_dt_bias, ssd_a_log, ssd_d, ssd_gate_norm_g, ssd_w_out, na_w_in, na_rpb, na_w_out, xa_w_kv,
           moe_w_router, moe_w1, moe_w3, moe_w2):
    b_sz, t_len, d = x.shape
    mem_len = mem.shape[1]
    depth = norm_mix_g.shape[0]
    m = b_sz * t_len
    d_kv = xa_w_kv.shape[-1]
    mem_n = rmsnorm_rows(mem.reshape(b_sz * mem_len, d), mem_norm_g, BF16)
    x2 = x.reshape(m, d)
    h2 = rmsnorm_rows(x2, norm_mix_g[0], BF16)
    for i in range(depth):
        j = i // N_MIXERS
        kv3 = matmul_ws([mem_n], [(xa_w_kv, (1, d, d_kv), lambda n, i=i: (i, 0, 0))], d_kv,
                        tn=d_kv, tm=512, out_dtype=BF16, name="xa_kv_proj").reshape(b_sz, mem_len, d_kv)
        if i % N_MIXERS == 0:
            heads, o_x = ssd_mixer_layer(h2, kv3, b_sz, t_len, j, ssd_w_in, ssd_conv_w, ssd_conv_b,
                                         ssd_dt_bias, ssd_a_log, ssd_d, ssd_gate_norm_g)
            w_out = ssd_w_out
        else:
            heads, o_x = na_mixer_layer(h2, kv3, b_sz, t_len, j, na_w_in, na_rpb)
            w_out = na_w_out
        x2, h_ffn, aff2 = out_proj_router(heads, o_x, w_out, j, x2, norm_ffn_g[i], moe_w_router, i)
        last = i == depth - 1
        x3, normed = ec_moe_layer(x2.reshape(b_sz, t_len, d), h_ffn, aff2, moe_w1, moe_w3, moe_w2, i,
                                  norm_final_g if last else norm_mix_g[i + 1], F32 if last else BF16)
        x2, h2 = x3.reshape(m, d), normed.reshape(m, d)
    return normed
```
